```python
import jax, jax.numpy as jnp
from jax import lax
import numpy as np

D_MODEL = 2048
BATCH = 4
SEQ = 4096
DEPTH = 1

CHUNK = 64
Q_BLOCK = 128
D_MIX = D_MODEL
FOX_WIDTH = D_MIX // 2
GLA_WIDTH = D_MIX - FOX_WIDTH
FOX_HEAD_DIM = 128
FOX_HEADS = FOX_WIDTH // FOX_HEAD_DIM
GLA_HEADS = 4
GLA_DV = GLA_WIDTH // GLA_HEADS
GLA_DK = GLA_DV // 2
GLA_KEY_WIDTH = GLA_HEADS * GLA_DK
GLA_GATE_RANK = 16
GLA_GATE_TEMP = 16.0
D_FF = 4 * D_MODEL
N_MOD = 6
EPS = 1e-6

IN_SIZES = (FOX_WIDTH, FOX_WIDTH, FOX_WIDTH, FOX_HEADS,
            GLA_KEY_WIDTH, GLA_KEY_WIDTH, GLA_WIDTH, GLA_GATE_RANK, GLA_WIDTH)
IN_OFFSETS = tuple(int(o) for o in np.cumsum(IN_SIZES)[:-1])
D_IN_PROJ = int(sum(IN_SIZES))

kernel_name = "hymba_fox_gla_sandwich_adaln_block"


def rmsnorm(x, g):
    xf = x.astype(jnp.float32)
    y = xf * lax.rsqrt(jnp.mean(xf * xf, axis=-1, keepdims=True) + EPS)
    return (y * g.astype(jnp.float32)).astype(x.dtype)


def head_rmsnorm(x, g):
    xf = x.astype(jnp.float32)
    y = xf * lax.rsqrt(jnp.mean(xf * xf, axis=-1, keepdims=True) + EPS)
    return (y * g.astype(jnp.float32)).astype(x.dtype)


def forgetting_attention(q, k, v, log_f):
    s = q.shape[1]
    scale = FOX_HEAD_DIM ** -0.5
    cum = jnp.transpose(jnp.cumsum(log_f, axis=1), (0, 2, 1))
    neg = jnp.finfo(jnp.float32).min
    outs = []
    for i in range(s // Q_BLOCK):
        q0, q1 = i * Q_BLOCK, (i + 1) * Q_BLOCK
        qb, kb, vb = q[:, q0:q1], k[:, :q1], v[:, :q1]
        logits = jnp.einsum('bqhd,bkhd->bhqk', qb, kb,
                            preferred_element_type=jnp.float32) * scale
        bias = cum[:, :, q0:q1, None] - cum[:, :, None, :q1]
        q_pos = jnp.arange(q0, q1)[:, None]
        k_pos = jnp.arange(q1)[None, :]
        logits = jnp.where(q_pos >= k_pos, logits + bias, neg)
        p = jax.nn.softmax(logits, axis=-1)
        outs.append(jnp.einsum('bhqk,bkhd->bqhd', p.astype(vb.dtype), vb))
    return jnp.concatenate(outs, axis=1)


def gla_chunk_causal(q, k, v, log_a):
    b, s, h, dk = q.shape
    dv = v.shape[-1]
    nc = s // CHUNK
    qc = q.reshape(b, nc, CHUNK, h, dk).astype(jnp.float32) * (dk ** -0.5)
    kc = k.reshape(b, nc, CHUNK, h, dk).astype(jnp.float32)
    vc = v.reshape(b, nc, CHUNK, h, dv).astype(jnp.float32)
    la = log_a.reshape(b, nc, CHUNK, h, dk)
    cum = jnp.cumsum(la, axis=2)
    total = cum[:, :, -1]
    k_dec = kc * jnp.exp(total[:, :, None] - cum)
    u = jnp.einsum('bnchk,bnchv->nbhkv', k_dec, vc)
    decay = jnp.transpose(jnp.exp(total), (1, 0, 2, 3))

    def step(state, inp):
        d_n, u_n = inp
        state = d_n[..., None] * state + u_n
        return state, state

    init = jnp.zeros((b, h, dk, dv), jnp.float32)
    _, states = lax.scan(step, init, (decay, u))
    o = jnp.einsum('bnchk,nbhkv->bnchv', qc, states)
    return o.reshape(b, s, h, dv).astype(v.dtype)


def token_mixer(h, w_in, b_fgate, w_gla_a2, b_gla_a2, g_fox_out, g_gla_out, w_out):
    b, s, _ = h.shape
    proj = h @ w_in
    fq, fk, fv, ff, gq, gk, gv, ga, gr = jnp.split(proj, IN_OFFSETS, axis=-1)

    log_f = jax.nn.log_sigmoid((ff + b_fgate).astype(jnp.float32))
    fox = forgetting_attention(fq.reshape(b, s, FOX_HEADS, FOX_HEAD_DIM),
                               fk.reshape(b, s, FOX_HEADS, FOX_HEAD_DIM),
                               fv.reshape(b, s, FOX_HEADS, FOX_HEAD_DIM), log_f)
    fox = head_rmsnorm(fox, g_fox_out).reshape(b, s, FOX_WIDTH)

    log_a = jax.nn.log_sigmoid((ga @ w_gla_a2 + b_gla_a2).astype(jnp.float32)) / GLA_GATE_TEMP
    gla = gla_chunk_causal(gq.reshape(b, s, GLA_HEADS, GLA_DK),
                           gk.reshape(b, s, GLA_HEADS, GLA_DK),
                           gv.reshape(b, s, GLA_HEADS, GLA_DV),
                           log_a.reshape(b, s, GLA_HEADS, GLA_DK))
    gla = head_rmsnorm(gla, g_gla_out).reshape(b, s, GLA_WIDTH) * jax.nn.silu(gr)

    return jnp.concatenate([fox, gla], axis=-1) @ w_out


def squared_relu_mlp(h, w_mlp_in, w_mlp_out):
    return jnp.square(jax.nn.relu(h @ w_mlp_in)) @ w_mlp_out


def setup_inputs(seed: int = 0) -> dict:
    key = jax.random.key(seed)
    ks = jax.random.split(key, 20)
    nrm = lambda k, shape, s: jax.random.normal(k, shape, jnp.float32) * s
    gain = lambda k, shape: 1.0 + nrm(k, shape, 0.02)
    L = DEPTH
    return {
        "x": nrm(ks[0], (BATCH, SEQ, D_MODEL), 1.0),
        "c": nrm(ks[1], (BATCH, D_MODEL), 1.0),
        "w_ada": nrm(ks[2], (L, D_MODEL, N_MOD * D_MODEL), 0.5 * D_MODEL ** -0.5),
        "b_ada": nrm(ks[3], (L, N_MOD * D_MODEL), 0.02),
        "g_pre_mix": gain(ks[4], (L, D_MODEL)),
        "g_post_mix": gain(ks[5], (L, D_MODEL)),
        "w_in": nrm(ks[6], (L, D_MODEL, D_IN_PROJ), D_MODEL ** -0.5),
        "b_fgate": 2.0 + nrm(ks[7], (L, FOX_HEADS), 0.1),
        "w_gla_a2": nrm(ks[8], (L, GLA_GATE_RANK, GLA_KEY_WIDTH), GLA_GATE_RANK ** -0.5),
        "b_gla_a2": nrm(ks[9], (L, GLA_KEY_WIDTH), 0.02),
        "g_fox_out": gain(ks[10], (L, FOX_HEADS, FOX_HEAD_DIM)),
        "g_gla_out": gain(ks[11], (L, GLA_HEADS, GLA_DV)),
        "w_out": nrm(ks[12], (L, D_MIX, D_MODEL), D_MIX ** -0.5),
        "g_pre_mlp": gain(ks[13], (L, D_MODEL)),
        "g_post_mlp": gain(ks[14], (L, D_MODEL)),
        "w_mlp_in": nrm(ks[15], (L, D_MODEL, D_FF), D_MODEL ** -0.5),
        "w_mlp_out": nrm(ks[16], (L, D_FF, D_MODEL), D_FF ** -0.5),
    }


def reference(x, c, w_ada, b_ada, g_pre_mix, g_post_mix, w_in, b_fgate, w_gla_a2,
              b_gla_a2, g_fox_out, g_gla_out, w_out, g_pre_mlp, g_post_mlp,
              w_mlp_in, w_mlp_out):
    c_act = jax.nn.silu(c)
    for i in range(DEPTH):
        mod = (c_act @ w_ada[i] + b_ada[i])[:, None, :]
        shift_m, scale_m, gate_m, shift_f, scale_f, gate_f = jnp.split(mod, N_MOD, axis=-1)

        h = rmsnorm(x, g_pre_mix[i]) * (1.0 + scale_m) + shift_m
        y = token_mixer(h, w_in[i], b_fgate[i], w_gla_a2[i], b_gla_a2[i],
                        g_fox_out[i], g_gla_out[i], w_out[i])
        x = x + gate_m * rmsnorm(y, g_post_mix[i])

        h = rmsnorm(x, g_pre_mlp[i]) * (1.0 + scale_f) + shift_f
        y = squared_relu_mlp(h, w_mlp_in[i], w_mlp_out[i])
        x = x + gate_f * rmsnorm(y, g_post_mlp[i])
    return x
```

```python
import functools

import jax
import jax.numpy as jnp
from jax import lax
from jax.experimental import pallas as pl
from jax.experimental.pallas import tpu as pltpu

F32 = jnp.float32
BF16 = jnp.bfloat16

EPS = 1e-6
N_MOD = 6
FOX_HEADS = 8
FOX_HEAD_DIM = 128
GLA_HEADS = 4
GLA_DK = 128
GLA_DV = 256
GLA_GATE_RANK = 16
GLA_GATE_TEMP = 16.0
CHUNK = 64
LANE = 128
VMEM_LIMIT = 56 * 1024 * 1024

NEG_BIG = -1e30


def _nt_dot(a, b):
    return lax.dot_general(a, b, (((1,), (1,)), ((), ())), preferred_element_type=F32)


def _tn_dot(a, b):
    return lax.dot_general(a, b, (((0,), (0,)), ((), ())), preferred_element_type=F32)


def _log_sigmoid(z):
    return jnp.minimum(z, 0.0) - jnp.log(1.0 + jnp.exp(-jnp.abs(z)))


def _sigmoid(z):
    return 1.0 / (1.0 + jnp.exp(-z))


def _tri_cumsum(tri, v):
    hi = v.astype(BF16)
    lo = (v - hi.astype(F32)).astype(BF16)
    return (jnp.dot(tri, hi, preferred_element_type=F32)
            + jnp.dot(tri, lo, preferred_element_type=F32))


def _lower_tri(n):
    r = lax.broadcasted_iota(jnp.int32, (n, n), 0)
    c = lax.broadcasted_iota(jnp.int32, (n, n), 1)
    return jnp.where(r >= c, 1.0, 0.0).astype(BF16)


def _ada_kernel(c_ref, w_ref, b_ref, o_ref):
    c = c_ref[...]
    act = (c * _sigmoid(c)).astype(BF16)
    o_ref[...] = jnp.dot(act, w_ref[...].astype(BF16), preferred_element_type=F32) + b_ref[...]


def _ada(c_pad, w_ada, b_ada, tn=1024):
    m, d = c_pad.shape
    n = w_ada.shape[1]
    return pl.pallas_call(
        _ada_kernel,
        out_shape=jax.ShapeDtypeStruct((m, n), F32),
        grid=(n // tn,),
        in_specs=[pl.BlockSpec((m, d), lambda j: (0, 0)),
                  pl.BlockSpec((d, tn), lambda j: (0, j)),
                  pl.BlockSpec((1, tn), lambda j: (0, j))],
        out_specs=pl.BlockSpec((m, tn), lambda j: (0, j)),
        compiler_params=pltpu.CompilerParams(dimension_semantics=("parallel",),
                                             vmem_limit_bytes=VMEM_LIMIT),
        name="ada",
    )(c_pad, w_ada, b_ada)


def _modulated_norm(x, g, scale, shift):
    ms = jnp.mean(x * x, axis=-1, keepdims=True)
    y = x * lax.rsqrt(ms + EPS) * g
    return y * (1.0 + scale) + shift


def _inproj_kernel(x_ref, mod_ref, g_ref, w_ref, ws_ref, o_ref, os_ref, h_scr):
    @pl.when(pl.program_id(1) == 0)
    def _():
        h = _modulated_norm(x_ref[...], g_ref[...], mod_ref[0, 1:2, :], mod_ref[0, 0:1, :])
        hb = h.astype(BF16)
        h_scr[...] = hb
        os_ref[...] = jnp.dot(hb, ws_ref[...], preferred_element_type=F32)

    o_ref[...] = jnp.dot(h_scr[...], w_ref[...], preferred_element_type=F32).astype(BF16)


def _in_proj(x2, mod, g, w_main, w_small, rows_per_batch, tm=1024, tn=1024):
    t, d = x2.shape
    n = w_main.shape[1]
    tiles_per_batch = rows_per_batch // tm
    return pl.pallas_call(
        _inproj_kernel,
        out_shape=(jax.ShapeDtypeStruct((t, n), BF16),
                   jax.ShapeDtypeStruct((t, LANE), F32)),
        grid=(t // tm, n // tn),
        in_specs=[pl.BlockSpec((tm, d), lambda i, j: (i, 0)),
                  pl.BlockSpec((1, N_MOD, d), lambda i, j: (i // tiles_per_batch, 0, 0)),
                  pl.BlockSpec((1, d), lambda i, j: (0, 0)),
                  pl.BlockSpec((d, tn), lambda i, j: (0, j)),
                  pl.BlockSpec((d, LANE), lambda i, j: (0, 0))],
        out_specs=(pl.BlockSpec((tm, tn), lambda i, j: (i, j)),
                   pl.BlockSpec((tm, LANE), lambda i, j: (i, 0))),
        scratch_shapes=[pltpu.VMEM((tm, d), BF16)],
        compiler_params=pltpu.CompilerParams(dimension_semantics=("parallel", "arbitrary"),
                                             vmem_limit_bytes=VMEM_LIMIT),
        name="in_proj",
    )(x2, mod, g, w_main, w_small)


def _cum_kernel(s_ref, b_ref, o_ref, cum_scr, *, blk):
    s = s_ref.shape[1]
    tri = _lower_tri(blk)
    carry = jnp.zeros((1, LANE), F32)
    for r in range(s // blk):
        z = s_ref[0, r * blk:(r + 1) * blk, :] + b_ref[...]
        cs = _tri_cumsum(tri, _log_sigmoid(z)) + carry
        cum_scr[r * blk:(r + 1) * blk, :] = cs
        carry = cs[blk - 1:blk, :]
    o_ref[0] = cum_scr[...].T[0:FOX_HEADS, :]


def _fox_cum(small3, bias_row, blk=256):
    b, s, _ = small3.shape
    return pl.pallas_call(
        functools.partial(_cum_kernel, blk=blk),
        out_shape=jax.ShapeDtypeStruct((b, FOX_HEADS, s), F32),
        grid=(b,),
        in_specs=[pl.BlockSpec((1, s, LANE), lambda i: (i, 0, 0)),
                  pl.BlockSpec((1, LANE), lambda i: (0, 0))],
        out_specs=pl.BlockSpec((1, FOX_HEADS, s), lambda i: (i, 0, 0)),
        scratch_shapes=[pltpu.VMEM((s, LANE), F32)],
        compiler_params=pltpu.CompilerParams(dimension_semantics=("parallel",),
                                             vmem_limit_bytes=VMEM_LIMIT),
        name="fox_cum",
    )(small3, bias_row)


def _fox_kernel(q_ref, k_ref, v_ref, c_ref, g_ref, o_ref, *, tq):
    i = pl.program_id(2)
    scale = FOX_HEAD_DIM ** -0.5
    q = q_ref[0]

    def step(j, carry, masked):
        m, l, acc = carry
        start = pl.multiple_of(j * tq, tq)
        k = k_ref[0, pl.ds(start, tq), :]
        v = v_ref[0, pl.ds(start, tq), :]
        ck = c_ref[0, :, pl.ds(start, tq)]
        s = _nt_dot(q, k) * scale - ck
        if masked:
            r = lax.broadcasted_iota(jnp.int32, (tq, tq), 0)
            c = lax.broadcasted_iota(jnp.int32, (tq, tq), 1)
            s = jnp.where(r >= c, s, NEG_BIG)
        m_new = jnp.maximum(m, jnp.max(s, axis=-1, keepdims=True))
        alpha = jnp.exp(m - m_new)
        p = jnp.exp(s - m_new)
        l = alpha * l + jnp.sum(p, axis=-1, keepdims=True)
        acc = alpha * acc + jnp.dot(p.astype(BF16), v, preferred_element_type=F32)
        return m_new, l, acc

    init = (jnp.full((tq, 1), NEG_BIG, F32), jnp.zeros((tq, 1), F32),
            jnp.zeros((tq, FOX_HEAD_DIM), F32))
    carry = lax.fori_loop(0, i, lambda j, cr: step(j, cr, False), init)
    _, l, acc = step(i, carry, True)
    o = acc / l
    o = o * lax.rsqrt(jnp.mean(o * o, axis=-1, keepdims=True) + EPS) * g_ref[0]
    o_ref[0] = o.astype(BF16)


def _fox_attn(proj3, cum3, g_fox, tq=512):
    b, s, _ = proj3.shape
    h = FOX_HEADS
    return pl.pallas_call(
        functools.partial(_fox_kernel, tq=tq),
        out_shape=jax.ShapeDtypeStruct((b, s, h * FOX_HEAD_DIM), BF16),
        grid=(b, h, s // tq),
        in_specs=[pl.BlockSpec((1, tq, LANE), lambda bi, hi, qi: (bi, qi, hi)),
                  pl.BlockSpec((1, s, LANE), lambda bi, hi, qi: (bi, 0, h + hi)),
                  pl.BlockSpec((1, s, LANE), lambda bi, hi, qi: (bi, 0, 2 * h + hi)),
                  pl.BlockSpec((1, 1, s), lambda bi, hi, qi: (bi * h + hi, 0, 0)),
                  pl.BlockSpec((1, 1, LANE), lambda bi, hi, qi: (hi, 0, 0))],
        out_specs=pl.BlockSpec((1, tq, LANE), lambda bi, hi, qi: (bi, qi, hi)),
        compiler_params=pltpu.CompilerParams(
            dimension_semantics=("parallel", "parallel", "arbitrary"),
            vmem_limit_bytes=VMEM_LIMIT),
        name="fox_attn",
    )(proj3, proj3, proj3, cum3, g_fox)


def _gla_kernel(q_ref, k_ref, v_ref, r_ref, s_ref, wa_ref, ba_ref, gg_ref, o_ref, st_ref, *, ts):
    @pl.when(pl.program_id(2) == 0)
    def _():
        st_ref[...] = jnp.zeros_like(st_ref)

    pre = jnp.dot(s_ref[0].astype(BF16), wa_ref[...], preferred_element_type=F32) + ba_ref[...]
    la = _log_sigmoid(pre) * (1.0 / GLA_GATE_TEMP)
    tri = _lower_tri(CHUNK)
    st = st_ref[...]
    outs = []
    for c in range(ts // CHUNK):
        sl = slice(c * CHUNK, (c + 1) * CHUNK)
        cum = _tri_cumsum(tri, la[sl])
        tot = cum[CHUNK - 1:CHUNK, :]
        kd = (k_ref[0, sl, :].astype(F32) * jnp.exp(tot - cum)).astype(BF16)
        st = st * jnp.exp(tot) + _tn_dot(v_ref[0, sl, :], kd)
        outs.append(_nt_dot(q_ref[0, sl, :], st.astype(BF16)))
    st_ref[...] = st
    o = jnp.concatenate(outs, axis=0) * (GLA_DK ** -0.5)
    o = o * lax.rsqrt(jnp.mean(o * o, axis=-1, keepdims=True) + EPS) * gg_ref[0]
    r = r_ref[0].astype(F32)
    o_ref[0] = (o * (r * _sigmoid(r))).astype(BF16)


def _gla(proj3, small3, wa_pad, ba_row, g_gla, ts=512):
    b, s, _ = proj3.shape
    q_blk = 3 * FOX_HEADS
    k_blk = q_blk + GLA_HEADS
    v_blk = (k_blk + GLA_HEADS) // 2
    r_blk = v_blk + GLA_HEADS
    return pl.pallas_call(
        functools.partial(_gla_kernel, ts=ts),
        out_shape=jax.ShapeDtypeStruct((b, s, GLA_HEADS * GLA_DV), BF16),
        grid=(b, GLA_HEADS, s // ts),
        in_specs=[pl.BlockSpec((1, ts, GLA_DK), lambda bi, gi, ti: (bi, ti, q_blk + gi)),
                  pl.BlockSpec((1, ts, GLA_DK), lambda bi, gi, ti: (bi, ti, k_blk + gi)),
                  pl.BlockSpec((1, ts, GLA_DV), lambda bi, gi, ti: (bi, ti, v_blk + gi)),
                  pl.BlockSpec((1, ts, GLA_DV), lambda bi, gi, ti: (bi, ti, r_blk + gi)),
                  pl.BlockSpec((1, ts, LANE), lambda bi, gi, ti: (bi, ti, 0)),
                  pl.BlockSpec((LANE, GLA_DK), lambda bi, gi, ti: (0, gi)),
                  pl.BlockSpec((1, GLA_DK), lambda bi, gi, ti: (0, gi)),
                  pl.BlockSpec((1, 1, GLA_DV), lambda bi, gi, ti: (gi, 0, 0))],
        out_specs=pl.BlockSpec((1, ts, GLA_DV), lambda bi, gi, ti: (bi, ti, gi)),
        scratch_shapes=[pltpu.VMEM((GLA_DV, GLA_DK), F32)],
        compiler_params=pltpu.CompilerParams(
            dimension_semantics=("parallel", "parallel", "arbitrary"),
            vmem_limit_bytes=VMEM_LIMIT),
        name="gla",
    )(proj3, proj3, proj3, proj3, small3, wa_pad, ba_row, g_gla)


def _gated_residual(x, y, g, gate):
    ms = jnp.mean(y * y, axis=-1, keepdims=True)
    return x + gate * (y * lax.rsqrt(ms + EPS) * g)


def _outproj_kernel(fox_ref, gla_ref, w_ref, x_ref, mod_ref, g_ref, o_ref):
    half = fox_ref.shape[1]
    y = (jnp.dot(fox_ref[...], w_ref[0:half, :], preferred_element_type=F32)
         + jnp.dot(gla_ref[...], w_ref[half:, :], preferred_element_type=F32))
    o_ref[...] = _gated_residual(x_ref[...], y, g_ref[...], mod_ref[0, 2:3, :])


def _out_proj(fox2, gla2, w_out, x2, mod, g, rows_per_batch, tm=512):
    t, d = x2.shape
    half = fox2.shape[1]
    tiles_per_batch = rows_per_batch // tm
    return pl.pallas_call(
        _outproj_kernel,
        out_shape=jax.ShapeDtypeStruct((t, d), F32),
        grid=(t // tm,),
        in_specs=[pl.BlockSpec((tm, half), lambda i: (i, 0)),
                  pl.BlockSpec((tm, half), lambda i: (i, 0)),
                  pl.BlockSpec((2 * half, d), lambda i: (0, 0)),
                  pl.BlockSpec((tm, d), lambda i: (i, 0)),
                  pl.BlockSpec((1, N_MOD, d), lambda i: (i // tiles_per_batch, 0, 0)),
                  pl.BlockSpec((1, d), lambda i: (0, 0))],
        out_specs=pl.BlockSpec((tm, d), lambda i: (i, 0)),
        compiler_params=pltpu.CompilerParams(dimension_semantics=("parallel",),
                                             vmem_limit_bytes=VMEM_LIMIT),
        name="out_proj",
    )(fox2, gla2, w_out, x2, mod, g)


def _mlp_kernel(x_ref, mod_ref, gpre_ref, gpost_ref, w1_ref, w2_ref, o_ref, h_scr, acc_scr):
    f = pl.program_id(1)

    @pl.when(f == 0)
    def _():
        h = _modulated_norm(x_ref[...], gpre_ref[...], mod_ref[0, 4:5, :], mod_ref[0, 3:4, :])
        h_scr[...] = h.astype(BF16)

    u = jnp.maximum(jnp.dot(h_scr[...], w1_ref[...], preferred_element_type=F32), 0.0)
    part = jnp.dot((u * u).astype(BF16), w2_ref[...], preferred_element_type=F32)

    @pl.when(f == 0)
    def _():
        acc_scr[...] = part

    @pl.when(f > 0)
    def _():
        acc_scr[...] += part

    @pl.when(f == pl.num_programs(1) - 1)
    def _():
        o_ref[...] = _gated_residual(x_ref[...], acc_scr[...], gpost_ref[...], mod_ref[0, 5:6, :])


def _mlp(x1, mod, g_pre, g_post, w1, w2, rows_per_batch, tm=512, tf=512):
    t, d = x1.shape
    dff = w1.shape[1]
    tiles_per_batch = rows_per_batch // tm
    return pl.pallas_call(
        _mlp_kernel,
        out_shape=jax.ShapeDtypeStruct((t, d), F32),
        grid=(t // tm, dff // tf),
        in_specs=[pl.BlockSpec((tm, d), lambda i, f: (i, 0)),
                  pl.BlockSpec((1, N_MOD, d), lambda i, f: (i // tiles_per_batch, 0, 0)),
                  pl.BlockSpec((1, d), lambda i, f: (0, 0)),
                  pl.BlockSpec((1, d), lambda i, f: (0, 0)),
                  pl.BlockSpec((d, tf), lambda i, f: (0, f)),
                  pl.BlockSpec((tf, d), lambda i, f: (f, 0))],
        out_specs=pl.BlockSpec((tm, d), lambda i, f: (i, 0)),
        scratch_shapes=[pltpu.VMEM((tm, d), BF16), pltpu.VMEM((tm, d), F32)],
        compiler_params=pltpu.CompilerParams(dimension_semantics=("parallel", "arbitrary"),
                                             vmem_limit_bytes=VMEM_LIMIT),
        name="mlp",
    )(x1, mod, g_pre, g_post, w1, w2)


def _regroup_in_proj_weight(w_in):
    fw = FOX_HEADS * FOX_HEAD_DIM
    kw = GLA_HEADS * GLA_DK
    vw = GLA_HEADS * GLA_DV
    sizes = (fw, fw, fw, FOX_HEADS, kw, kw, vw, GLA_GATE_RANK, vw)
    offs = [0]
    for sz in sizes:
        offs.append(offs[-1] + sz)
    piece = lambda n: w_in[:, offs[n]:offs[n + 1]]
    main = jnp.concatenate([piece(0), piece(1), piece(2), piece(4), piece(5), piece(6), piece(8)],
                           axis=1).astype(BF16)
    pad = jnp.zeros((w_in.shape[0], LANE - FOX_HEADS - GLA_GATE_RANK), w_in.dtype)
    small = jnp.concatenate([piece(3), piece(7), pad], axis=1).astype(BF16)
    return main, small


def kernel(x, c, w_ada, b_ada, g_pre_mix, g_post_mix, w_in, b_fgate, w_gla_a2, b_gla_a2,
           g_fox_out, g_gla_out, w_out, g_pre_mlp, g_post_mlp, w_mlp_in, w_mlp_out):
    b, s, d = x.shape
    depth = w_ada.shape[0]
    row = lambda v: v.reshape(1, -1)
    for i in range(depth):
        c_pad = jnp.concatenate([c, jnp.zeros((8 - b, d), c.dtype)], axis=0)
        mod = _ada(c_pad, w_ada[i], row(b_ada[i]))[:b].reshape(b, N_MOD, d)

        w_main, w_small = _regroup_in_proj_weight(w_in[i])
        x2 = x.reshape(b * s, d)
        proj, small = _in_proj(x2, mod, row(g_pre_mix[i]), w_main, w_small, s)
        proj3 = proj.reshape(b, s, -1)
        small3 = small.reshape(b, s, LANE)

        fbias = jnp.concatenate([b_fgate[i], jnp.zeros((LANE - FOX_HEADS,), F32)]).reshape(1, LANE)
        cum = _fox_cum(small3, fbias)
        fox = _fox_attn(proj3, cum.reshape(b * FOX_HEADS, 1, s),
                        g_fox_out[i].reshape(FOX_HEADS, 1, FOX_HEAD_DIM))

        kw = GLA_HEADS * GLA_DK
        wa_pad = jnp.concatenate(
            [jnp.zeros((FOX_HEADS, kw), F32), w_gla_a2[i],
             jnp.zeros((LANE - FOX_HEADS - GLA_GATE_RANK, kw), F32)], axis=0).astype(BF16)
        gla = _gla(proj3, small3, wa_pad, row(b_gla_a2[i]),
                   g_gla_out[i].reshape(GLA_HEADS, 1, GLA_DV))

        x1 = _out_proj(fox.reshape(b * s, -1), gla.reshape(b * s, -1), w_out[i].astype(BF16),
                       x2, mod, row(g_post_mix[i]), s)
        x2 = _mlp(x1, mod, row(g_pre_mlp[i]), row(g_post_mlp[i]),
                  w_mlp_in[i].astype(BF16), w_mlp_out[i].astype(BF16), s)
        x = x2.reshape(b, s, d)
    return x
```

```python
import functools

import jax
import jax.numpy as jnp
from jax import lax
from jax.experimental import pallas as pl
from jax.experimental.pallas import tpu as pltpu

F32 = jnp.float32
BF16 = jnp.bfloat16

EPS = 1e-6
N_MOD = 6
FOX_HEADS = 8
FOX_HEAD_DIM = 128
GLA_HEADS = 4
GLA_DK = 128
GLA_DV = 256
GLA_GATE_RANK = 16
GLA_GATE_TEMP = 16.0
CHUNK = 64
LANE = 128
NORM_ROWS = 16
VMEM_LIMIT = 56 * 1024 * 1024

NEG_BIG = -1e30


def _nt_dot(a, b):
    return lax.dot_general(a, b, (((1,), (1,)), ((), ())), preferred_element_type=F32)


def _tn_dot(a, b):
    return lax.dot_general(a, b, (((0,), (0,)), ((), ())), preferred_element_type=F32)


def _log_sigmoid(z):
    return jnp.minimum(z, 0.0) - jnp.log(1.0 + jnp.exp(-jnp.abs(z)))


def _sigmoid(z):
    return 1.0 / (1.0 + jnp.exp(-z))


def _tri_cumsum(tri, v):
    hi = v.astype(BF16)
    lo = (v - hi.astype(F32)).astype(BF16)
    return (jnp.dot(tri, hi, preferred_element_type=F32)
            + jnp.dot(tri, lo, preferred_element_type=F32))


def _lower_tri(n):
    r = lax.broadcasted_iota(jnp.int32, (n, n), 0)
    c = lax.broadcasted_iota(jnp.int32, (n, n), 1)
    return jnp.where(r >= c, 1.0, 0.0).astype(BF16)


def _ada_kernel(c_ref, w_ref, b_ref, o_ref):
    c = c_ref[...]
    act = (c * _sigmoid(c)).astype(BF16)
    o_ref[...] = jnp.dot(act, w_ref[...].astype(BF16), preferred_element_type=F32) + b_ref[...]


def _ada(c_pad, w_ada, b_ada, tn=1024):
    m, d = c_pad.shape
    n = w_ada.shape[1]
    return pl.pallas_call(
        _ada_kernel,
        out_shape=jax.ShapeDtypeStruct((m, n), F32),
        grid=(n // tn,),
        in_specs=[pl.BlockSpec((m, d), lambda j: (0, 0)),
                  pl.BlockSpec((d, tn), lambda j: (0, j)),
                  pl.BlockSpec((1, tn), lambda j: (0, j))],
        out_specs=pl.BlockSpec((m, tn), lambda j: (0, j)),
        compiler_params=pltpu.CompilerParams(dimension_semantics=("parallel",),
                                             vmem_limit_bytes=VMEM_LIMIT),
        name="ada",
    )(c_pad, w_ada, b_ada)


def _row_strips(n_rows, body, unroll):
    def step(r, carry):
        body(pl.ds(pl.multiple_of(r * NORM_ROWS, NORM_ROWS), NORM_ROWS))
        return carry
    lax.fori_loop(0, n_rows // NORM_ROWS, step, 0, unroll=unroll)


def _inv_rms_rows(y_ref, inv_ref):
    def body(rows):
        y = y_ref[rows, :]
        inv = lax.rsqrt(jnp.mean(y * y, axis=-1, keepdims=True) + EPS)
        inv_ref[rows, :] = jnp.broadcast_to(inv, (NORM_ROWS, LANE))
    _row_strips(y_ref.shape[0], body, unroll=8)


def _modulated_norm_rows(x_ref, h_ref, inv_ref, gain, shift):
    _inv_rms_rows(x_ref, inv_ref)

    reps = x_ref.shape[1] // LANE

    def body(rows):
        inv = jnp.tile(inv_ref[rows, :], (1, reps))
        h_ref[rows, :] = (x_ref[rows, :] * inv * gain + shift).astype(BF16)
    _row_strips(x_ref.shape[0], body, unroll=4)


def _gated_residual_rows(x_ref, y_ref, o_ref, inv_ref, gain):
    _inv_rms_rows(y_ref, inv_ref)

    reps = x_ref.shape[1] // LANE

    def body(rows):
        inv = jnp.tile(inv_ref[rows, :], (1, reps))
        o_ref[rows, :] = x_ref[rows, :] + y_ref[rows, :] * inv * gain
    _row_strips(x_ref.shape[0], body, unroll=4)


def _inproj_kernel(x_ref, mod_ref, g_ref, w_ref, ws_ref, o_ref, os_ref, h_scr, inv_scr):
    @pl.when(pl.program_id(1) == 0)
    def _():
        gain = g_ref[...] * (1.0 + mod_ref[0, 1:2, :])
        _modulated_norm_rows(x_ref, h_scr, inv_scr, gain, mod_ref[0, 0:1, :])
        os_ref[...] = jnp.dot(h_scr[...], ws_ref[...], preferred_element_type=F32)

    o_ref[...] = jnp.dot(h_scr[...], w_ref[...], preferred_element_type=F32).astype(BF16)


def _in_proj(x2, mod, g, w_main, w_small, rows_per_batch, tm=1024, tn=1024):
    t, d = x2.shape
    n = w_main.shape[1]
    tiles_per_batch = rows_per_batch // tm
    return pl.pallas_call(
        _inproj_kernel,
        out_shape=(jax.ShapeDtypeStruct((t, n), BF16),
                   jax.ShapeDtypeStruct((t, LANE), F32)),
        grid=(t // tm, n // tn),
        in_specs=[pl.BlockSpec((tm, d), lambda i, j: (i, 0)),
                  pl.BlockSpec((1, N_MOD, d), lambda i, j: (i // tiles_per_batch, 0, 0)),
                  pl.BlockSpec((1, d), lambda i, j: (0, 0)),
                  pl.BlockSpec((d, tn), lambda i, j: (0, j)),
                  pl.BlockSpec((d, LANE), lambda i, j: (0, 0))],
        out_specs=(pl.BlockSpec((tm, tn), lambda i, j: (i, j)),
                   pl.BlockSpec((tm, LANE), lambda i, j: (i, 0))),
        scratch_shapes=[pltpu.VMEM((tm, d), BF16), pltpu.VMEM((tm, LANE), F32)],
        compiler_params=pltpu.CompilerParams(dimension_semantics=("parallel", "arbitrary"),
                                             vmem_limit_bytes=VMEM_LIMIT),
        name="in_proj",
    )(x2, mod, g, w_main, w_small)


def _cum_kernel(s_ref, b_ref, o_ref, cum_scr, *, blk):
    s = s_ref.shape[1]
    tri = _lower_tri(blk)
    carry = jnp.zeros((1, LANE), F32)
    for r in range(s // blk):
        z = s_ref[0, r * blk:(r + 1) * blk, :] + b_ref[...]
        cs = _tri_cumsum(tri, _log_sigmoid(z)) + carry
        cum_scr[r * blk:(r + 1) * blk, :] = cs
        carry = cs[blk - 1:blk, :]
    o_ref[0] = cum_scr[...].T[0:FOX_HEADS, :]


def _fox_cum(small3, bias_row, blk=256):
    b, s, _ = small3.shape
    return pl.pallas_call(
        functools.partial(_cum_kernel, blk=blk),
        out_shape=jax.ShapeDtypeStruct((b, FOX_HEADS, s), F32),
        grid=(b,),
        in_specs=[pl.BlockSpec((1, s, LANE), lambda i: (i, 0, 0)),
                  pl.BlockSpec((1, LANE), lambda i: (0, 0))],
        out_specs=pl.BlockSpec((1, FOX_HEADS, s), lambda i: (i, 0, 0)),
        scratch_shapes=[pltpu.VMEM((s, LANE), F32)],
        compiler_params=pltpu.CompilerParams(dimension_semantics=("parallel",),
                                             vmem_limit_bytes=VMEM_LIMIT),
        name="fox_cum",
    )(small3, bias_row)


def _fox_kernel(q_ref, k_ref, v_ref, c_ref, g_ref, o_ref, *, tq):
    i = pl.program_id(2)
    scale = FOX_HEAD_DIM ** -0.5
    q = q_ref[0]

    def step(j, carry, masked):
        m, l, acc = carry
        start = pl.multiple_of(j * tq, tq)
        k = k_ref[0, pl.ds(start, tq), :]
        v = v_ref[0, pl.ds(start, tq), :]
        ck = c_ref[0, :, pl.ds(start, tq)]
        s = _nt_dot(q, k) * scale - ck
        if masked:
            r = lax.broadcasted_iota(jnp.int32, (tq, tq), 0)
            c = lax.broadcasted_iota(jnp.int32, (tq, tq), 1)
            s = jnp.where(r >= c, s, NEG_BIG)
        m_new = jnp.maximum(m, jnp.max(s, axis=-1, keepdims=True))
        alpha = jnp.exp(m - m_new)
        p = jnp.exp(s - m_new)
        l = alpha * l + jnp.sum(p, axis=-1, keepdims=True)
        acc = alpha * acc + jnp.dot(p.astype(BF16), v, preferred_element_type=F32)
        return m_new, l, acc

    init = (jnp.full((tq, 1), NEG_BIG, F32), jnp.zeros((tq, 1), F32),
            jnp.zeros((tq, FOX_HEAD_DIM), F32))
    carry = lax.fori_loop(0, i, lambda j, cr: step(j, cr, False), init)
    _, l, acc = step(i, carry, True)
    o = acc / l
    o = o * lax.rsqrt(jnp.mean(o * o, axis=-1, keepdims=True) + EPS) * g_ref[0]
    o_ref[0] = o.astype(BF16)


def _fox_attn(proj3, cum3, g_fox, tq=512):
    b, s, _ = proj3.shape
    h = FOX_HEADS
    return pl.pallas_call(
        functools.partial(_fox_kernel, tq=tq),
        out_shape=jax.ShapeDtypeStruct((b, s, h * FOX_HEAD_DIM), BF16),
        grid=(b, h, s // tq),
        in_specs=[pl.BlockSpec((1, tq, LANE), lambda bi, hi, qi: (bi, qi, hi)),
                  pl.BlockSpec((1, s, LANE), lambda bi, hi, qi: (bi, 0, h + hi)),
                  pl.BlockSpec((1, s, LANE), lambda bi, hi, qi: (bi, 0, 2 * h + hi)),
                  pl.BlockSpec((1, 1, s), lambda bi, hi, qi: (bi * h + hi, 0, 0)),
                  pl.BlockSpec((1, 1, LANE), lambda bi, hi, qi: (hi, 0, 0))],
        out_specs=pl.BlockSpec((1, tq, LANE), lambda bi, hi, qi: (bi, qi, hi)),
        compiler_params=pltpu.CompilerParams(
            dimension_semantics=("parallel", "parallel", "arbitrary"),
            vmem_limit_bytes=VMEM_LIMIT),
        name="fox_attn",
    )(proj3, proj3, proj3, cum3, g_fox)


def _gla_kernel(q_ref, k_ref, v_ref, r_ref, s_ref, wa_ref, ba_ref, gg_ref, o_ref, st_ref, *, ts):
    @pl.when(pl.program_id(2) == 0)
    def _():
        st_ref[...] = jnp.zeros_like(st_ref)

    pre = jnp.dot(s_ref[0].astype(BF16), wa_ref[...], preferred_element_type=F32) + ba_ref[...]
    la = _log_sigmoid(pre) * (1.0 / GLA_GATE_TEMP)
    tri = _lower_tri(CHUNK)
    st = st_ref[...]
    outs = []
    for c in range(ts // CHUNK):
        sl = slice(c * CHUNK, (c + 1) * CHUNK)
        cum = _tri_cumsum(tri, la[sl])
        tot = cum[CHUNK - 1:CHUNK, :]
        kd = (k_ref[0, sl, :].astype(F32) * jnp.exp(tot - cum)).astype(BF16)
        st = st * jnp.exp(tot) + _tn_dot(v_ref[0, sl, :], kd)
        outs.append(_nt_dot(q_ref[0, sl, :], st.astype(BF16)))
    st_ref[...] = st
    o = jnp.concatenate(outs, axis=0) * (GLA_DK ** -0.5)
    o = o * lax.rsqrt(jnp.mean(o * o, axis=-1, keepdims=True) + EPS) * gg_ref[0]
    r = r_ref[0].astype(F32)
    o_ref[0] = (o * (r * _sigmoid(r))).astype(BF16)


def _gla(proj3, small3, wa_pad, ba_row, g_gla, ts=512):
    b, s, _ = proj3.shape
    q_blk = 3 * FOX_HEADS
    k_blk = q_blk + GLA_HEADS
    v_blk = (k_blk + GLA_HEADS) // 2
    r_blk = v_blk + GLA_HEADS
    return pl.pallas_call(
        functools.partial(_gla_kernel, ts=ts),
        out_shape=jax.ShapeDtypeStruct((b, s, GLA_HEADS * GLA_DV), BF16),
        grid=(b, GLA_HEADS, s // ts),
        in_specs=[pl.BlockSpec((1, ts, GLA_DK), lambda bi, gi, ti: (bi, ti, q_blk + gi)),
                  pl.BlockSpec((1, ts, GLA_DK), lambda bi, gi, ti: (bi, ti, k_blk + gi)),
                  pl.BlockSpec((1, ts, GLA_DV), lambda bi, gi, ti: (bi, ti, v_blk + gi)),
                  pl.BlockSpec((1, ts, GLA_DV), lambda bi, gi, ti: (bi, ti, r_blk + gi)),
                  pl.BlockSpec((1, ts, LANE), lambda bi, gi, ti: (bi, ti, 0)),
                  pl.BlockSpec((LANE, GLA_DK), lambda bi, gi, ti: (0, gi)),
                  pl.BlockSpec((1, GLA_DK), lambda bi, gi, ti: (0, gi)),
                  pl.BlockSpec((1, 1, GLA_DV), lambda bi, gi, ti: (gi, 0, 0))],
        out_specs=pl.BlockSpec((1, ts, GLA_DV), lambda bi, gi, ti: (bi, ti, gi)),
        scratch_shapes=[pltpu.VMEM((GLA_DV, GLA_DK), F32)],
        compiler_params=pltpu.CompilerParams(
            dimension_semantics=("parallel", "parallel", "arbitrary"),
            vmem_limit_bytes=VMEM_LIMIT),
        name="gla",
    )(proj3, proj3, proj3, proj3, small3, wa_pad, ba_row, g_gla)


def _outproj_kernel(fox_ref, gla_ref, w_ref, x_ref, mod_ref, g_ref, o_ref, y_scr, inv_scr):
    half = fox_ref.shape[1]
    y_scr[...] = (jnp.dot(fox_ref[...], w_ref[0:half, :], preferred_element_type=F32)
                  + jnp.dot(gla_ref[...], w_ref[half:, :], preferred_element_type=F32))
    _gated_residual_rows(x_ref, y_scr, o_ref, inv_scr, mod_ref[0, 2:3, :] * g_ref[...])


def _out_proj(fox2, gla2, w_out, x2, mod, g, rows_per_batch, tm=512):
    t, d = x2.shape
    half = fox2.shape[1]
    tiles_per_batch = rows_per_batch // tm
    return pl.pallas_call(
        _outproj_kernel,
        out_shape=jax.ShapeDtypeStruct((t, d), F32),
        grid=(t // tm,),
        in_specs=[pl.BlockSpec((tm, half), lambda i: (i, 0)),
                  pl.BlockSpec((tm, half), lambda i: (i, 0)),
                  pl.BlockSpec((2 * half, d), lambda i: (0, 0)),
                  pl.BlockSpec((tm, d), lambda i: (i, 0)),
                  pl.BlockSpec((1, N_MOD, d), lambda i: (i // tiles_per_batch, 0, 0)),
                  pl.BlockSpec((1, d), lambda i: (0, 0))],
        out_specs=pl.BlockSpec((tm, d), lambda i: (i, 0)),
        scratch_shapes=[pltpu.VMEM((tm, d), F32), pltpu.VMEM((tm, LANE), F32)],
        compiler_params=pltpu.CompilerParams(dimension_semantics=("parallel",),
                                             vmem_limit_bytes=VMEM_LIMIT),
        name="out_proj",
    )(fox2, gla2, w_out, x2, mod, g)


def _mlp_kernel(x_ref, mod_ref, gpre_ref, gpost_ref, w1_ref, w2_ref, o_ref, h_scr, acc_scr,
                inv_scr):
    f = pl.program_id(1)

    @pl.when(f == 0)
    def _():
        gain = gpre_ref[...] * (1.0 + mod_ref[0, 4:5, :])
        _modulated_norm_rows(x_ref, h_scr, inv_scr, gain, mod_ref[0, 3:4, :])
        acc_scr[...] = jnp.zeros_like(acc_scr)

    u = jnp.maximum(jnp.dot(h_scr[...], w1_ref[...], preferred_element_type=F32), 0.0)
    acc_scr[...] += jnp.dot((u * u).astype(BF16), w2_ref[...], preferred_element_type=F32)

    @pl.when(f == pl.num_programs(1) - 1)
    def _():
        _gated_residual_rows(x_ref, acc_scr, o_ref, inv_scr, mod_ref[0, 5:6, :] * gpost_ref[...])


def _mlp(x1, mod, g_pre, g_post, w1, w2, rows_per_batch, tm=512, tf=1024):
    t, d = x1.shape
    dff = w1.shape[1]
    tiles_per_batch = rows_per_batch // tm
    return pl.pallas_call(
        _mlp_kernel,
        out_shape=jax.ShapeDtypeStruct((t, d), F32),
        grid=(t // tm, dff // tf),
        in_specs=[pl.BlockSpec((tm, d), lambda i, f: (i, 0)),
                  pl.BlockSpec((1, N_MOD, d), lambda i, f: (i // tiles_per_batch, 0, 0)),
                  pl.BlockSpec((1, d), lambda i, f: (0, 0)),
                  pl.BlockSpec((1, d), lambda i, f: (0, 0)),
                  pl.BlockSpec((d, tf), lambda i, f: (0, f)),
                  pl.BlockSpec((tf, d), lambda i, f: (f, 0))],
        out_specs=pl.BlockSpec((tm, d), lambda i, f: (i, 0)),
        scratch_shapes=[pltpu.VMEM((tm, d), BF16), pltpu.VMEM((tm, d), F32),
                        pltpu.VMEM((tm, LANE), F32)],
        compiler_params=pltpu.CompilerParams(dimension_semantics=("parallel", "arbitrary"),
                                             vmem_limit_bytes=VMEM_LIMIT),
        name="mlp",
    )(x1, mod, g_pre, g_post, w1, w2)


def _regroup_in_proj_weight(w_in):
    fw = FOX_HEADS * FOX_HEAD_DIM
    kw = GLA_HEADS * GLA_DK
    vw = GLA_HEADS * GLA_DV
    sizes = (fw, fw, fw, FOX_HEADS, kw, kw, vw, GLA_GATE_RANK, vw)
    offs = [0]
    for sz in sizes:
        offs.append(offs[-1] + sz)
    piece = lambda n: w_in[:, offs[n]:offs[n + 1]]
    main = jnp.concatenate([piece(0), piece(1), piece(2), piece(4), piece(5), piece(6), piece(8)],
                           axis=1).astype(BF16)
    pad = jnp.zeros((w_in.shape[0], LANE - FOX_HEADS - GLA_GATE_RANK), w_in.dtype)
    small = jnp.concatenate([piece(3), piece(7), pad], axis=1).astype(BF16)
    return main, small


def kernel(x, c, w_ada, b_ada, g_pre_mix, g_post_mix, w_in, b_fgate, w_gla_a2, b_gla_a2,
           g_fox_out, g_gla_out, w_out, g_pre_mlp, g_post_mlp, w_mlp_in, w_mlp_out):
    b, s, d = x.shape
    depth = w_ada.shape[0]
    row = lambda v: v.reshape(1, -1)
    for i in range(depth):
        c_pad = jnp.concatenate([c, jnp.zeros((8 - b, d), c.dtype)], axis=0)
        mod = _ada(c_pad, w_ada[i], row(b_ada[i]))[:b].reshape(b, N_MOD, d)

        w_main, w_small = _regroup_in_proj_weight(w_in[i])
        x2 = x.reshape(b * s, d)
        proj, small = _in_proj(x2, mod, row(g_pre_mix[i]), w_main, w_small, s)
        proj3 = proj.reshape(b, s, -1)
        small3 = small.reshape(b, s, LANE)

        fbias = jnp.concatenate([b_fgate[i], jnp.zeros((LANE - FOX_HEADS,), F32)]).reshape(1, LANE)
        cum = _fox_cum(small3, fbias)
        fox = _fox_attn(proj3, cum.reshape(b * FOX_HEADS, 1, s),
                        g_fox_out[i].reshape(FOX_HEADS, 1, FOX_HEAD_DIM))

        kw = GLA_HEADS * GLA_DK
        wa_pad = jnp.concatenate(
            [jnp.zeros((FOX_HEADS, kw), F32), w_gla_a2[i],
             jnp.zeros((LANE - FOX_HEADS - GLA_GATE_RANK, kw), F32)], axis=0).astype(BF16)
        gla = _gla(proj3, small3, wa_pad, row(b_gla_a2[i]),
                   g_gla_out[i].reshape(GLA_HEADS, 1, GLA_DV))

        x1 = _out_proj(fox.reshape(b * s, -1), gla.reshape(b * s, -1), w_out[i].astype(BF16),
                       x2, mod, row(g_post_mix[i]), s)
        x2 = _mlp(x1, mod, row(g_pre_mlp[i]), row(g_post_mlp[i]),
                  w_mlp_in[i].astype(BF16), w_mlp_out[i].astype(BF16), s)
        x = x2.reshape(b, s, d)
    return x
```

```python
import functools

import jax
import jax.numpy as jnp
from jax import lax
from jax.experimental import pallas as pl
from jax.experimental.pallas import tpu as pltpu

F32 = jnp.float32
BF16 = jnp.bfloat16

EPS = 1e-6
N_MOD = 6
FOX_HEADS = 8
FOX_HEAD_DIM = 128
GLA_HEADS = 4
GLA_DK = 128
GLA_DV = 256
GLA_GATE_RANK = 16
GLA_GATE_TEMP = 16.0
CHUNK = 64
LANE = 128
NORM_ROWS = 16
VMEM_LIMIT = 56 * 1024 * 1024

NEG_BIG = -1e30
LOG2E = 1.4426950408889634


def _nt_dot(a, b):
    return lax.dot_general(a, b, (((1,), (1,)), ((), ())), preferred_element_type=F32)


def _tn_dot(a, b):
    return lax.dot_general(a, b, (((0,), (0,)), ((), ())), preferred_element_type=F32)


def _log_sigmoid(z):
    return jnp.minimum(z, 0.0) - jnp.log(1.0 + jnp.exp(-jnp.abs(z)))


def _sigmoid(z):
    return 1.0 / (1.0 + jnp.exp(-z))


def _tri_cumsum(tri, v):
    hi = v.astype(BF16)
    lo = (v - hi.astype(F32)).astype(BF16)
    return (jnp.dot(tri, hi, preferred_element_type=F32)
            + jnp.dot(tri, lo, preferred_element_type=F32))


def _lower_tri(n):
    r = lax.broadcasted_iota(jnp.int32, (n, n), 0)
    c = lax.broadcasted_iota(jnp.int32, (n, n), 1)
    return jnp.where(r >= c, 1.0, 0.0).astype(BF16)


def _ada_kernel(c_ref, w_ref, b_ref, o_ref):
    c = c_ref[...]
    act = (c * _sigmoid(c)).astype(BF16)
    o_ref[...] = jnp.dot(act, w_ref[...].astype(BF16), preferred_element_type=F32) + b_ref[...]


def _ada(c_pad, w_ada, b_ada, tn=1024):
    m, d = c_pad.shape
    n = w_ada.shape[1]
    return pl.pallas_call(
        _ada_kernel,
        out_shape=jax.ShapeDtypeStruct((m, n), F32),
        grid=(n // tn,),
        in_specs=[pl.BlockSpec((m, d), lambda j: (0, 0)),
                  pl.BlockSpec((d, tn), lambda j: (0, j)),
                  pl.BlockSpec((1, tn), lambda j: (0, j))],
        out_specs=pl.BlockSpec((m, tn), lambda j: (0, j)),
        compiler_params=pltpu.CompilerParams(dimension_semantics=("parallel",),
                                             vmem_limit_bytes=VMEM_LIMIT),
        name="ada",
    )(c_pad, w_ada, b_ada)


def _row_strips(n_rows, body, unroll):
    def step(r, carry):
        body(pl.ds(pl.multiple_of(r * NORM_ROWS, NORM_ROWS), NORM_ROWS))
        return carry
    lax.fori_loop(0, n_rows // NORM_ROWS, step, 0, unroll=unroll)


def _inv_rms_rows(y_ref, inv_ref):
    def body(rows):
        y = y_ref[rows, :]
        inv = lax.rsqrt(jnp.mean(y * y, axis=-1, keepdims=True) + EPS)
        inv_ref[rows, :] = jnp.broadcast_to(inv, (NORM_ROWS, LANE))
    _row_strips(y_ref.shape[0], body, unroll=8)


def _modulated_norm_rows(x_ref, h_ref, inv_ref, gain, shift):
    _inv_rms_rows(x_ref, inv_ref)

    reps = x_ref.shape[1] // LANE

    def body(rows):
        inv = jnp.tile(inv_ref[rows, :], (1, reps))
        h_ref[rows, :] = (x_ref[rows, :] * inv * gain + shift).astype(BF16)
    _row_strips(x_ref.shape[0], body, unroll=4)


def _gated_residual_rows(x_ref, y_ref, o_ref, inv_ref, gain):
    _inv_rms_rows(y_ref, inv_ref)

    reps = x_ref.shape[1] // LANE

    def body(rows):
        inv = jnp.tile(inv_ref[rows, :], (1, reps))
        o_ref[rows, :] = x_ref[rows, :] + y_ref[rows, :] * inv * gain
    _row_strips(x_ref.shape[0], body, unroll=4)


def _inproj_kernel(x_ref, mod_ref, g_ref, w_ref, ws_ref, o_ref, os_ref, h_scr, inv_scr):
    @pl.when(pl.program_id(1) == 0)
    def _():
        gain = g_ref[...] * (1.0 + mod_ref[0, 1:2, :])
        _modulated_norm_rows(x_ref, h_scr, inv_scr, gain, mod_ref[0, 0:1, :])
        os_ref[...] = jnp.dot(h_scr[...], ws_ref[...], preferred_element_type=F32)

    qmul = jnp.where(pl.program_id(1) == 0, FOX_HEAD_DIM ** -0.5 * LOG2E, 1.0)
    acc = jnp.dot(h_scr[...], w_ref[...], preferred_element_type=F32)
    o_ref[...] = (acc * qmul).astype(BF16)


def _in_proj(x2, mod, g, w_main, w_small, rows_per_batch, tm=1024, tn=1024):
    t, d = x2.shape
    n = w_main.shape[1]
    assert tn == FOX_HEADS * FOX_HEAD_DIM, "column block 0 must be exactly the FoX queries"
    tiles_per_batch = rows_per_batch // tm
    return pl.pallas_call(
        _inproj_kernel,
        out_shape=(jax.ShapeDtypeStruct((t, n), BF16),
                   jax.ShapeDtypeStruct((t, LANE), F32)),
        grid=(t // tm, n // tn),
        in_specs=[pl.BlockSpec((tm, d), lambda i, j: (i, 0)),
                  pl.BlockSpec((1, N_MOD, d), lambda i, j: (i // tiles_per_batch, 0, 0)),
                  pl.BlockSpec((1, d), lambda i, j: (0, 0)),
                  pl.BlockSpec((d, tn), lambda i, j: (0, j)),
                  pl.BlockSpec((d, LANE), lambda i, j: (0, 0))],
        out_specs=(pl.BlockSpec((tm, tn), lambda i, j: (i, j)),
                   pl.BlockSpec((tm, LANE), lambda i, j: (i, 0))),
        scratch_shapes=[pltpu.VMEM((tm, d), BF16), pltpu.VMEM((tm, LANE), F32)],
        compiler_params=pltpu.CompilerParams(dimension_semantics=("parallel", "arbitrary"),
                                             vmem_limit_bytes=VMEM_LIMIT),
        name="in_proj",
    )(x2, mod, g, w_main, w_small)


def _cum_kernel(s_ref, b_ref, o_ref, *, blk):
    s = s_ref.shape[1]
    tri = _lower_tri(blk)
    col = lax.broadcasted_iota(jnp.int32, (blk, LANE), 1)
    carry = jnp.zeros((1, LANE), F32)
    for r in range(s // blk):
        rows = slice(r * blk, (r + 1) * blk)
        z = s_ref[0, rows, :] + b_ref[...]
        cs = _tri_cumsum(tri, _log_sigmoid(z)) + carry
        carry = cs[blk - 1:blk, :]
        c2 = jnp.where(col < FOX_HEADS, cs * LOG2E, 0.0)
        hi = c2.astype(BF16).astype(F32)
        rem = c2 - hi
        mid = rem.astype(BF16).astype(F32)
        lo = (rem - mid).astype(BF16).astype(F32)
        pieces = hi + pltpu.roll(mid, FOX_HEADS, 1) + pltpu.roll(lo, 2 * FOX_HEADS, 1)
        o_ref[0, rows, :] = pieces.astype(BF16)


def _fox_cum(small3, bias_row, blk=256):
    b, s, _ = small3.shape
    return pl.pallas_call(
        functools.partial(_cum_kernel, blk=blk),
        out_shape=jax.ShapeDtypeStruct((b, s, LANE), BF16),
        grid=(b,),
        in_specs=[pl.BlockSpec((1, s, LANE), lambda i: (i, 0, 0)),
                  pl.BlockSpec((1, LANE), lambda i: (0, 0))],
        out_specs=pl.BlockSpec((1, s, LANE), lambda i: (i, 0, 0)),
        compiler_params=pltpu.CompilerParams(dimension_semantics=("parallel",),
                                             vmem_limit_bytes=VMEM_LIMIT),
        name="fox_cum",
    )(small3, bias_row)


def _fox_kernel(q_ref, k_ref, v_ref, a_ref, g_ref, o_ref, ka_scr, vt_scr, sa_scr, sb_scr, m_scr, l_scr,
                acc_scr, *, tq):
    h = pl.program_id(1)
    i = pl.program_id(2)

    @pl.when(i == 0)
    def _():
        ka_scr[:, 0:LANE] = k_ref[0]
        ka_scr[:, LANE:] = a_ref[0]
        vt_scr[...] = v_ref[0].T

    col = lax.broadcasted_iota(jnp.int32, (tq, LANE), 1)
    pick = (col == h) | (col == h + FOX_HEADS) | (col == h + 2 * FOX_HEADS)
    qa = jnp.concatenate([q_ref[0], jnp.where(pick, -1.0, 0.0).astype(BF16)], axis=1)

    def logits_to(s_ref, blk):
        s_ref[...] = _nt_dot(ka_scr[pl.ds(pl.multiple_of(blk * tq, tq), tq), :], qa)

    def absorb(s_ref, blk, masked):
        st = s_ref[...]
        if masked:
            kr = lax.broadcasted_iota(jnp.int32, (tq, tq), 0)
            qc = lax.broadcasted_iota(jnp.int32, (tq, tq), 1)
            st = jnp.where(kr <= qc, st, NEG_BIG)
        m = m_scr[...]
        m_new = jnp.maximum(m, jnp.max(st, axis=0, keepdims=True))
        alpha = jnp.exp2(m - m_new)
        p = jnp.exp2(st - m_new)
        m_scr[...] = m_new
        l_scr[...] = alpha * l_scr[...] + jnp.sum(p, axis=0, keepdims=True)
        pv = jnp.dot(vt_scr[:, pl.ds(pl.multiple_of(blk * tq, tq), tq)], p.astype(BF16),
                     preferred_element_type=F32)
        acc_scr[...] = alpha * acc_scr[...] + pv

    m_scr[...] = jnp.full_like(m_scr, NEG_BIG)
    l_scr[...] = jnp.zeros_like(l_scr)
    acc_scr[...] = jnp.zeros_like(acc_scr)

    logits_to(sa_scr, 0)

    def block_pair(t, carry):
        logits_to(sb_scr, 2 * t + 1)
        absorb(sa_scr, 2 * t, False)
        logits_to(sa_scr, 2 * t + 2)
        absorb(sb_scr, 2 * t + 1, False)
        return carry

    lax.fori_loop(0, i // 2, block_pair, 0)

    @pl.when(i % 2 == 0)
    def _():
        absorb(sa_scr, i, True)

    @pl.when(i % 2 == 1)
    def _():
        logits_to(sb_scr, i)
        absorb(sa_scr, i - 1, False)
        absorb(sb_scr, i, True)

    ot = acc_scr[...] / l_scr[...]
    ot = ot * lax.rsqrt(jnp.mean(ot * ot, axis=0, keepdims=True) + EPS)
    o_ref[0] = (ot.T * g_ref[0]).astype(BF16)


def _fox_attn(proj3, pieces3, g_fox, tq=512):
    b, s, _ = proj3.shape
    h = FOX_HEADS
    return pl.pallas_call(
        functools.partial(_fox_kernel, tq=tq),
        out_shape=jax.ShapeDtypeStruct((b, s, h * FOX_HEAD_DIM), BF16),
        grid=(b, h, s // tq),
        in_specs=[pl.BlockSpec((1, tq, LANE), lambda bi, hi, qi: (bi, qi, hi)),
                  pl.BlockSpec((1, s, LANE), lambda bi, hi, qi: (bi, 0, h + hi)),
                  pl.BlockSpec((1, s, LANE), lambda bi, hi, qi: (bi, 0, 2 * h + hi)),
                  pl.BlockSpec((1, s, LANE), lambda bi, hi, qi: (bi, 0, 0)),
                  pl.BlockSpec((1, 1, LANE), lambda bi, hi, qi: (hi, 0, 0))],
        out_specs=pl.BlockSpec((1, tq, LANE), lambda bi, hi, qi: (bi, qi, hi)),
        scratch_shapes=[pltpu.VMEM((s, 2 * LANE), BF16), pltpu.VMEM((FOX_HEAD_DIM, s), BF16),
                        pltpu.VMEM((tq, tq), F32), pltpu.VMEM((tq, tq), F32),
                        pltpu.VMEM((1, tq), F32), pltpu.VMEM((1, tq), F32),
                        pltpu.VMEM((FOX_HEAD_DIM, tq), F32)],
        compiler_params=pltpu.CompilerParams(
            dimension_semantics=("parallel", "parallel", "arbitrary"),
            vmem_limit_bytes=VMEM_LIMIT),
        name="fox_attn",
    )(proj3, proj3, proj3, pieces3, g_fox)


def _gla_kernel(q_ref, k_ref, v_ref, r_ref, s_ref, wa_ref, ba_ref, gg_ref, o_ref, st_ref, *, ts):
    @pl.when(pl.program_id(2) == 0)
    def _():
        st_ref[...] = jnp.zeros_like(st_ref)

    pre = jnp.dot(s_ref[0].astype(BF16), wa_ref[...], preferred_element_type=F32) + ba_ref[...]
    la = _log_sigmoid(pre) * (1.0 / GLA_GATE_TEMP)
    tri = _lower_tri(CHUNK)
    st = st_ref[...]
    outs = []
    for c in range(ts // CHUNK):
        sl = slice(c * CHUNK, (c + 1) * CHUNK)
        cum = _tri_cumsum(tri, la[sl])
        tot = cum[CHUNK - 1:CHUNK, :]
        kd = (k_ref[0, sl, :].astype(F32) * jnp.exp(tot - cum)).astype(BF16)
        st = st * jnp.exp(tot) + _tn_dot(v_ref[0, sl, :], kd)
        outs.append(_nt_dot(q_ref[0, sl, :], st.astype(BF16)))
    st_ref[...] = st
    o = jnp.concatenate(outs, axis=0) * (GLA_DK ** -0.5)
    o = o * lax.rsqrt(jnp.mean(o * o, axis=-1, keepdims=True) + EPS) * gg_ref[0]
    r = r_ref[0].astype(F32)
    o_ref[0] = (o * (r * _sigmoid(r))).astype(BF16)


def _gla(proj3, small3, wa_pad, ba_row, g_gla, ts=512):
    b, s, _ = proj3.shape
    q_blk = 3 * FOX_HEADS
    k_blk = q_blk + GLA_HEADS
    v_blk = (k_blk + GLA_HEADS) // 2
    r_blk = v_blk + GLA_HEADS
    return pl.pallas_call(
        functools.partial(_gla_kernel, ts=ts),
        out_shape=jax.ShapeDtypeStruct((b, s, GLA_HEADS * GLA_DV), BF16),
        grid=(b, GLA_HEADS, s // ts),
        in_specs=[pl.BlockSpec((1, ts, GLA_DK), lambda bi, gi, ti: (bi, ti, q_blk + gi)),
                  pl.BlockSpec((1, ts, GLA_DK), lambda bi, gi, ti: (bi, ti, k_blk + gi)),
                  pl.BlockSpec((1, ts, GLA_DV), lambda bi, gi, ti: (bi, ti, v_blk + gi)),
                  pl.BlockSpec((1, ts, GLA_DV), lambda bi, gi, ti: (bi, ti, r_blk + gi)),
                  pl.BlockSpec((1, ts, LANE), lambda bi, gi, ti: (bi, ti, 0)),
                  pl.BlockSpec((LANE, GLA_DK), lambda bi, gi, ti: (0, gi)),
                  pl.BlockSpec((1, GLA_DK), lambda bi, gi, ti: (0, gi)),
                  pl.BlockSpec((1, 1, GLA_DV), lambda bi, gi, ti: (gi, 0, 0))],
        out_specs=pl.BlockSpec((1, ts, GLA_DV), lambda bi, gi, ti: (bi, ti, gi)),
        scratch_shapes=[pltpu.VMEM((GLA_DV, GLA_DK), F32)],
        compiler_params=pltpu.CompilerParams(
            dimension_semantics=("parallel", "parallel", "arbitrary"),
            vmem_limit_bytes=VMEM_LIMIT),
        name="gla",
    )(proj3, proj3, proj3, proj3, small3, wa_pad, ba_row, g_gla)


def _outproj_kernel(fox_ref, gla_ref, w_ref, x_ref, mod_ref, g_ref, o_ref):
    half = fox_ref.shape[1]
    y = (jnp.dot(fox_ref[...], w_ref[0:half, :], preferred_element_type=F32)
         + jnp.dot(gla_ref[...], w_ref[half:, :], preferred_element_type=F32))
    inv = lax.rsqrt(jnp.mean(y * y, axis=-1, keepdims=True) + EPS)
    o_ref[...] = x_ref[...] + y * inv * (mod_ref[0, 2:3, :] * g_ref[...])


def _out_proj(fox2, gla2, w_out, x2, mod, g, rows_per_batch, tm=512):
    t, d = x2.shape
    half = fox2.shape[1]
    tiles_per_batch = rows_per_batch // tm
    return pl.pallas_call(
        _outproj_kernel,
        out_shape=jax.ShapeDtypeStruct((t, d), F32),
        grid=(t // tm,),
        in_specs=[pl.BlockSpec((tm, half), lambda i: (i, 0)),
                  pl.BlockSpec((tm, half), lambda i: (i, 0)),
                  pl.BlockSpec((2 * half, d), lambda i: (0, 0)),
                  pl.BlockSpec((tm, d), lambda i: (i, 0)),
                  pl.BlockSpec((1, N_MOD, d), lambda i: (i // tiles_per_batch, 0, 0)),
                  pl.BlockSpec((1, d), lambda i: (0, 0))],
        out_specs=pl.BlockSpec((tm, d), lambda i: (i, 0)),
        compiler_params=pltpu.CompilerParams(dimension_semantics=("parallel",),
                                             vmem_limit_bytes=VMEM_LIMIT),
        name="out_proj",
    )(fox2, gla2, w_out, x2, mod, g)


def _mlp_kernel(x_ref, mod_ref, gpre_ref, gpost_ref, w1_ref, w2_ref, o_ref, h_scr, acc_scr,
                inv_scr):
    f = pl.program_id(1)

    @pl.when(f == 0)
    def _():
        gain = gpre_ref[...] * (1.0 + mod_ref[0, 4:5, :])
        _modulated_norm_rows(x_ref, h_scr, inv_scr, gain, mod_ref[0, 3:4, :])
        acc_scr[...] = jnp.zeros_like(acc_scr)

    u = jnp.maximum(jnp.dot(h_scr[...], w1_ref[...], preferred_element_type=F32), 0.0)
    acc_scr[...] += jnp.dot((u * u).astype(BF16), w2_ref[...], preferred_element_type=F32)

    @pl.when(f == pl.num_programs(1) - 1)
    def _():
        _gated_residual_rows(x_ref, acc_scr, o_ref, inv_scr, mod_ref[0, 5:6, :] * gpost_ref[...])


def _mlp(x1, mod, g_pre, g_post, w1, w2, rows_per_batch, tm=512, tf=1024):
    t, d = x1.shape
    dff = w1.shape[1]
    tiles_per_batch = rows_per_batch // tm
    return pl.pallas_call(
        _mlp_kernel,
        out_shape=jax.ShapeDtypeStruct((t, d), F32),
        grid=(t // tm, dff // tf),
        in_specs=[pl.BlockSpec((tm, d), lambda i, f: (i, 0)),
                  pl.BlockSpec((1, N_MOD, d), lambda i, f: (i // tiles_per_batch, 0, 0)),
                  pl.BlockSpec((1, d), lambda i, f: (0, 0)),
                  pl.BlockSpec((1, d), lambda i, f: (0, 0)),
                  pl.BlockSpec((d, tf), lambda i, f: (0, f)),
                  pl.BlockSpec((tf, d), lambda i, f: (f, 0))],
        out_specs=pl.BlockSpec((tm, d), lambda i, f: (i, 0)),
        scratch_shapes=[pltpu.VMEM((tm, d), BF16), pltpu.VMEM((tm, d), F32),
                        pltpu.VMEM((tm, LANE), F32)],
        compiler_params=pltpu.CompilerParams(dimension_semantics=("parallel", "arbitrary"),
                                             vmem_limit_bytes=VMEM_LIMIT),
        name="mlp",
    )(x1, mod, g_pre, g_post, w1, w2)


def _regroup_in_proj_weight(w_in):
    fw = FOX_HEADS * FOX_HEAD_DIM
    kw = GLA_HEADS * GLA_DK
    vw = GLA_HEADS * GLA_DV
    sizes = (fw, fw, fw, FOX_HEADS, kw, kw, vw, GLA_GATE_RANK, vw)
    offs = [0]
    for sz in sizes:
        offs.append(offs[-1] + sz)
    piece = lambda n: w_in[:, offs[n]:offs[n + 1]]
    main = jnp.concatenate([piece(0), piece(1), piece(2), piece(4), piece(5), piece(6), piece(8)],
                           axis=1).astype(BF16)
    pad = jnp.zeros((w_in.shape[0], LANE - FOX_HEADS - GLA_GATE_RANK), w_in.dtype)
    small = jnp.concatenate([piece(3), piece(7), pad], axis=1).astype(BF16)
    return main, small


def kernel(x, c, w_ada, b_ada, g_pre_mix, g_post_mix, w_in, b_fgate, w_gla_a2, b_gla_a2,
           g_fox_out, g_gla_out, w_out, g_pre_mlp, g_post_mlp, w_mlp_in, w_mlp_out):
    b, s, d = x.shape
    depth = w_ada.shape[0]
    row = lambda v: v.reshape(1, -1)
    for i in range(depth):
        c_pad = jnp.concatenate([c, jnp.zeros((8 - b, d), c.dtype)], axis=0)
        mod = _ada(c_pad, w_ada[i], row(b_ada[i]))[:b].reshape(b, N_MOD, d)

        w_main, w_small = _regroup_in_proj_weight(w_in[i])
        x2 = x.reshape(b * s, d)
        proj, small = _in_proj(x2, mod, row(g_pre_mix[i]), w_main, w_small, s)
        proj3 = proj.reshape(b, s, -1)
        small3 = small.reshape(b, s, LANE)

        fbias = jnp.concatenate([b_fgate[i], jnp.zeros((LANE - FOX_HEADS,), F32)]).reshape(1, LANE)
        fox = _fox_attn(proj3, _fox_cum(small3, fbias),
                        g_fox_out[i].reshape(FOX_HEADS, 1, FOX_HEAD_DIM))

        kw = GLA_HEADS * GLA_DK
        wa_pad = jnp.concatenate(
            [jnp.zeros((FOX_HEADS, kw), F32), w_gla_a2[i],
             jnp.zeros((LANE - FOX_HEADS - GLA_GATE_RANK, kw), F32)], axis=0).astype(BF16)
        gla = _gla(proj3, small3, wa_pad, row(b_gla_a2[i]),
                   g_gla_out[i].reshape(GLA_HEADS, 1, GLA_DV))

        x1 = _out_proj(fox.reshape(b * s, -1), gla.reshape(b * s, -1), w_out[i].astype(BF16),
                       x2, mod, row(g_post_mix[i]), s)
        x2 = _mlp(x1, mod, row(g_pre_mlp[i]), row(g_post_mlp[i]),
                  w_mlp_in[i].astype(BF16), w_mlp_out[i].astype(BF16), s)
        x = x2.reshape(b, s, d)
    return x
```

```python
import functools

import jax
import jax.numpy as jnp
from jax import lax
from jax.experimental import pallas as pl
from jax.experimental.pallas import tpu as pltpu

F32 = jnp.float32
BF16 = jnp.bfloat16

EPS = 1e-6
N_MOD = 6
FOX_HEADS = 8
FOX_HEAD_DIM = 128
GLA_HEADS = 4
GLA_DK = 128
GLA_DV = 256
GLA_GATE_RANK = 16
GLA_GATE_TEMP = 16.0
CHUNK = 64
LANE = 128
NORM_ROWS = 16
VMEM_LIMIT = 56 * 1024 * 1024

NEG_BIG = -1e30
LOG2E = 1.4426950408889634


def _nt_dot(a, b):
    return lax.dot_general(a, b, (((1,), (1,)), ((), ())), preferred_element_type=F32)


def _tn_dot(a, b):
    return lax.dot_general(a, b, (((0,), (0,)), ((), ())), preferred_element_type=F32)


def _log_sigmoid(z):
    return jnp.minimum(z, 0.0) - jnp.log(1.0 + jnp.exp(-jnp.abs(z)))


def _sigmoid(z):
    return 1.0 / (1.0 + jnp.exp(-z))


def _tri_cumsum(tri, v):
    hi = v.astype(BF16)
    lo = (v - hi.astype(F32)).astype(BF16)
    return (jnp.dot(tri, hi, preferred_element_type=F32)
            + jnp.dot(tri, lo, preferred_element_type=F32))


def _lower_tri(n):
    r = lax.broadcasted_iota(jnp.int32, (n, n), 0)
    c = lax.broadcasted_iota(jnp.int32, (n, n), 1)
    return jnp.where(r >= c, 1.0, 0.0).astype(BF16)


def _ada_kernel(c_ref, w_ref, b_ref, o_ref):
    c = c_ref[...]
    act = (c * _sigmoid(c)).astype(BF16)
    o_ref[...] = jnp.dot(act, w_ref[...].astype(BF16), preferred_element_type=F32) + b_ref[...]


def _ada(c_pad, w_ada, b_ada, tn=1024):
    m, d = c_pad.shape
    n = w_ada.shape[1]
    return pl.pallas_call(
        _ada_kernel,
        out_shape=jax.ShapeDtypeStruct((m, n), F32),
        grid=(n // tn,),
        in_specs=[pl.BlockSpec((m, d), lambda j: (0, 0)),
                  pl.BlockSpec((d, tn), lambda j: (0, j)),
                  pl.BlockSpec((1, tn), lambda j: (0, j))],
        out_specs=pl.BlockSpec((m, tn), lambda j: (0, j)),
        compiler_params=pltpu.CompilerParams(dimension_semantics=("parallel",),
                                             vmem_limit_bytes=VMEM_LIMIT),
        name="ada",
    )(c_pad, w_ada, b_ada)


def _row_strips(n_rows, body, unroll):
    def step(r, carry):
        body(pl.ds(pl.multiple_of(r * NORM_ROWS, NORM_ROWS), NORM_ROWS))
        return carry
    lax.fori_loop(0, n_rows // NORM_ROWS, step, 0, unroll=unroll)


def _inv_rms_rows(y_ref, inv_ref):
    def body(rows):
        y = y_ref[rows, :]
        inv = lax.rsqrt(jnp.mean(y * y, axis=-1, keepdims=True) + EPS)
        inv_ref[rows, :] = jnp.broadcast_to(inv, (NORM_ROWS, LANE))
    _row_strips(y_ref.shape[0], body, unroll=8)


def _modulated_norm_rows(x_ref, h_ref, inv_ref, gain, shift):
    _inv_rms_rows(x_ref, inv_ref)

    reps = x_ref.shape[1] // LANE

    def body(rows):
        inv = jnp.tile(inv_ref[rows, :], (1, reps))
        h_ref[rows, :] = (x_ref[rows, :] * inv * gain + shift).astype(BF16)
    _row_strips(x_ref.shape[0], body, unroll=4)


def _gated_residual_rows(x_ref, y_ref, o_ref, inv_ref, gain):
    _inv_rms_rows(y_ref, inv_ref)

    reps = x_ref.shape[1] // LANE

    def body(rows):
        inv = jnp.tile(inv_ref[rows, :], (1, reps))
        o_ref[rows, :] = x_ref[rows, :] + y_ref[rows, :] * inv * gain
    _row_strips(x_ref.shape[0], body, unroll=4)


def _inproj_kernel(x_ref, mod_ref, g_ref, w_ref, ws_ref, o_ref, os_ref, h_scr, inv_scr):
    @pl.when(pl.program_id(1) == 0)
    def _():
        gain = g_ref[...] * (1.0 + mod_ref[0, 1:2, :])
        _modulated_norm_rows(x_ref, h_scr, inv_scr, gain, mod_ref[0, 0:1, :])
        os_ref[...] = jnp.dot(h_scr[...], ws_ref[...], preferred_element_type=F32)

    qmul = jnp.where(pl.program_id(1) == 0, FOX_HEAD_DIM ** -0.5 * LOG2E, 1.0)
    acc = jnp.dot(h_scr[...], w_ref[...], preferred_element_type=F32)
    o_ref[...] = (acc * qmul).astype(BF16)


def _in_proj(x2, mod, g, w_main, w_small, rows_per_batch, tm=1024, tn=1024):
    t, d = x2.shape
    n = w_main.shape[1]
    assert tn == FOX_HEADS * FOX_HEAD_DIM, "column block 0 must be exactly the FoX queries"
    tiles_per_batch = rows_per_batch // tm
    return pl.pallas_call(
        _inproj_kernel,
        out_shape=(jax.ShapeDtypeStruct((t, n), BF16),
                   jax.ShapeDtypeStruct((t, LANE), F32)),
        grid=(t // tm, n // tn),
        in_specs=[pl.BlockSpec((tm, d), lambda i, j: (i, 0)),
                  pl.BlockSpec((1, N_MOD, d), lambda i, j: (i // tiles_per_batch, 0, 0)),
                  pl.BlockSpec((1, d), lambda i, j: (0, 0)),
                  pl.BlockSpec((d, tn), lambda i, j: (0, j)),
                  pl.BlockSpec((d, LANE), lambda i, j: (0, 0))],
        out_specs=(pl.BlockSpec((tm, tn), lambda i, j: (i, j)),
                   pl.BlockSpec((tm, LANE), lambda i, j: (i, 0))),
        scratch_shapes=[pltpu.VMEM((tm, d), BF16), pltpu.VMEM((tm, LANE), F32)],
        compiler_params=pltpu.CompilerParams(dimension_semantics=("parallel", "arbitrary"),
                                             vmem_limit_bytes=VMEM_LIMIT),
        name="in_proj",
    )(x2, mod, g, w_main, w_small)


def _cum_kernel(s_ref, b_ref, o_ref, *, blk):
    s = s_ref.shape[1]
    tri = _lower_tri(blk)
    col = lax.broadcasted_iota(jnp.int32, (blk, LANE), 1)
    carry = jnp.zeros((1, LANE), F32)
    for r in range(s // blk):
        rows = slice(r * blk, (r + 1) * blk)
        z = s_ref[0, rows, :] + b_ref[...]
        cs = _tri_cumsum(tri, _log_sigmoid(z)) + carry
        carry = cs[blk - 1:blk, :]
        c2 = jnp.where(col < FOX_HEADS, cs * LOG2E, 0.0)
        hi = c2.astype(BF16).astype(F32)
        rem = c2 - hi
        mid = rem.astype(BF16).astype(F32)
        lo = (rem - mid).astype(BF16).astype(F32)
        pieces = hi + pltpu.roll(mid, FOX_HEADS, 1) + pltpu.roll(lo, 2 * FOX_HEADS, 1)
        o_ref[0, rows, :] = pieces.astype(BF16)


def _fox_cum(small3, bias_row, blk=256):
    b, s, _ = small3.shape
    return pl.pallas_call(
        functools.partial(_cum_kernel, blk=blk),
        out_shape=jax.ShapeDtypeStruct((b, s, LANE), BF16),
        grid=(b,),
        in_specs=[pl.BlockSpec((1, s, LANE), lambda i: (i, 0, 0)),
                  pl.BlockSpec((1, LANE), lambda i: (0, 0))],
        out_specs=pl.BlockSpec((1, s, LANE), lambda i: (i, 0, 0)),
        compiler_params=pltpu.CompilerParams(dimension_semantics=("parallel",),
                                             vmem_limit_bytes=VMEM_LIMIT),
        name="fox_cum",
    )(small3, bias_row)


def _fox_kernel(q_ref, k_ref, v_ref, a_ref, g_ref, o_ref, ka_scr, vt_scr, sa_scr, sb_scr, m_scr, l_scr,
                acc_scr, *, tq):
    h = pl.program_id(1)
    i = pl.program_id(2)

    @pl.when(i == 0)
    def _():
        ka_scr[:, 0:LANE] = k_ref[0]
        ka_scr[:, LANE:] = a_ref[0]
        vt_scr[...] = v_ref[0].T

    col = lax.broadcasted_iota(jnp.int32, (tq, LANE), 1)
    pick = (col == h) | (col == h + FOX_HEADS) | (col == h + 2 * FOX_HEADS)
    qa = jnp.concatenate([q_ref[0], jnp.where(pick, -1.0, 0.0).astype(BF16)], axis=1)

    def logits_to(s_ref, blk):
        s_ref[...] = _nt_dot(ka_scr[pl.ds(pl.multiple_of(blk * tq, tq), tq), :], qa)

    def absorb(s_ref, blk, masked):
        st = s_ref[...]
        if masked:
            kr = lax.broadcasted_iota(jnp.int32, (tq, tq), 0)
            qc = lax.broadcasted_iota(jnp.int32, (tq, tq), 1)
            st = jnp.where(kr <= qc, st, NEG_BIG)
        m = m_scr[...]
        m_new = jnp.maximum(m, jnp.max(st, axis=0, keepdims=True))
        alpha = jnp.exp2(m - m_new)
        p = jnp.exp2(st - m_new)
        m_scr[...] = m_new
        l_scr[...] = alpha * l_scr[...] + jnp.sum(p, axis=0, keepdims=True)
        pv = jnp.dot(vt_scr[:, pl.ds(pl.multiple_of(blk * tq, tq), tq)], p.astype(BF16),
                     preferred_element_type=F32)
        acc_scr[...] = alpha * acc_scr[...] + pv

    m_scr[...] = jnp.full_like(m_scr, NEG_BIG)
    l_scr[...] = jnp.zeros_like(l_scr)
    acc_scr[...] = jnp.zeros_like(acc_scr)

    logits_to(sa_scr, 0)

    def block_pair(t, carry):
        logits_to(sb_scr, 2 * t + 1)
        absorb(sa_scr, 2 * t, False)
        logits_to(sa_scr, 2 * t + 2)
        absorb(sb_scr, 2 * t + 1, False)
        return carry

    lax.fori_loop(0, i // 2, block_pair, 0)

    @pl.when(i % 2 == 0)
    def _():
        absorb(sa_scr, i, True)

    @pl.when(i % 2 == 1)
    def _():
        logits_to(sb_scr, i)
        absorb(sa_scr, i - 1, False)
        absorb(sb_scr, i, True)

    ot = acc_scr[...] / l_scr[...]
    ot = ot * lax.rsqrt(jnp.mean(ot * ot, axis=0, keepdims=True) + EPS)
    o_ref[0] = (ot.T * g_ref[0]).astype(BF16)


def _fox_attn(proj3, pieces3, g_fox, tq=512):
    b, s, _ = proj3.shape
    h = FOX_HEADS
    return pl.pallas_call(
        functools.partial(_fox_kernel, tq=tq),
        out_shape=jax.ShapeDtypeStruct((b, s, h * FOX_HEAD_DIM), BF16),
        grid=(b, h, s // tq),
        in_specs=[pl.BlockSpec((1, tq, LANE), lambda bi, hi, qi: (bi, qi, hi)),
                  pl.BlockSpec((1, s, LANE), lambda bi, hi, qi: (bi, 0, h + hi)),
                  pl.BlockSpec((1, s, LANE), lambda bi, hi, qi: (bi, 0, 2 * h + hi)),
                  pl.BlockSpec((1, s, LANE), lambda bi, hi, qi: (bi, 0, 0)),
                  pl.BlockSpec((1, 1, LANE), lambda bi, hi, qi: (hi, 0, 0))],
        out_specs=pl.BlockSpec((1, tq, LANE), lambda bi, hi, qi: (bi, qi, hi)),
        scratch_shapes=[pltpu.VMEM((s, 2 * LANE), BF16), pltpu.VMEM((FOX_HEAD_DIM, s), BF16),
                        pltpu.VMEM((tq, tq), F32), pltpu.VMEM((tq, tq), F32),
                        pltpu.VMEM((1, tq), F32), pltpu.VMEM((1, tq), F32),
                        pltpu.VMEM((FOX_HEAD_DIM, tq), F32)],
        compiler_params=pltpu.CompilerParams(
            dimension_semantics=("parallel", "parallel", "arbitrary"),
            vmem_limit_bytes=VMEM_LIMIT),
        name="fox_attn",
    )(proj3, proj3, proj3, pieces3, g_fox)


def _gla_kernel(q_ref, k_ref, v_ref, r_ref, s_ref, wa_ref, ba_ref, gg_ref, o_ref, st_ref, *, ts):
    @pl.when(pl.program_id(2) == 0)
    def _():
        st_ref[...] = jnp.zeros_like(st_ref)

    pre = jnp.dot(s_ref[0].astype(BF16), wa_ref[...], preferred_element_type=F32) + ba_ref[...]
    la = _log_sigmoid(pre) * (1.0 / GLA_GATE_TEMP)
    tri = _lower_tri(CHUNK)
    chunks = [slice(c * CHUNK, (c + 1) * CHUNK) for c in range(ts // CHUNK)]
    cums = [_tri_cumsum(tri, la[sl]) for sl in chunks]
    tots = [cum[CHUNK - 1:CHUNK, :] for cum in cums]
    ups = []
    for sl, cum, tot in zip(chunks, cums, tots):
        kd = (k_ref[0, sl, :].astype(F32) * jnp.exp(tot - cum)).astype(BF16)
        ups.append(_tn_dot(v_ref[0, sl, :], kd))
    st = st_ref[...]
    states = []
    for tot, up in zip(tots, ups):
        st = st * jnp.exp(tot) + up
        states.append(st.astype(BF16))
    st_ref[...] = st
    outs = [_nt_dot(q_ref[0, sl, :], sb) for sl, sb in zip(chunks, states)]
    o = jnp.concatenate(outs, axis=0) * (GLA_DK ** -0.5)
    o = o * lax.rsqrt(jnp.mean(o * o, axis=-1, keepdims=True) + EPS) * gg_ref[0]
    r = r_ref[0].astype(F32)
    o_ref[0] = (o * (r * _sigmoid(r))).astype(BF16)


def _gla(proj3, small3, wa_pad, ba_row, g_gla, ts=512):
    b, s, _ = proj3.shape
    q_blk = 3 * FOX_HEADS
    k_blk = q_blk + GLA_HEADS
    v_blk = (k_blk + GLA_HEADS) // 2
    r_blk = v_blk + GLA_HEADS
    return pl.pallas_call(
        functools.partial(_gla_kernel, ts=ts),
        out_shape=jax.ShapeDtypeStruct((b, s, GLA_HEADS * GLA_DV), BF16),
        grid=(b, GLA_HEADS, s // ts),
        in_specs=[pl.BlockSpec((1, ts, GLA_DK), lambda bi, gi, ti: (bi, ti, q_blk + gi)),
                  pl.BlockSpec((1, ts, GLA_DK), lambda bi, gi, ti: (bi, ti, k_blk + gi)),
                  pl.BlockSpec((1, ts, GLA_DV), lambda bi, gi, ti: (bi, ti, v_blk + gi)),
                  pl.BlockSpec((1, ts, GLA_DV), lambda bi, gi, ti: (bi, ti, r_blk + gi)),
                  pl.BlockSpec((1, ts, LANE), lambda bi, gi, ti: (bi, ti, 0)),
                  pl.BlockSpec((LANE, GLA_DK), lambda bi, gi, ti: (0, gi)),
                  pl.BlockSpec((1, GLA_DK), lambda bi, gi, ti: (0, gi)),
                  pl.BlockSpec((1, 1, GLA_DV), lambda bi, gi, ti: (gi, 0, 0))],
        out_specs=pl.BlockSpec((1, ts, GLA_DV), lambda bi, gi, ti: (bi, ti, gi)),
        scratch_shapes=[pltpu.VMEM((GLA_DV, GLA_DK), F32)],
        compiler_params=pltpu.CompilerParams(
            dimension_semantics=("parallel", "parallel", "arbitrary"),
            vmem_limit_bytes=VMEM_LIMIT),
        name="gla",
    )(proj3, proj3, proj3, proj3, small3, wa_pad, ba_row, g_gla)


def _outproj_kernel(fox_ref, gla_ref, w_ref, x_ref, mod_ref, g_ref, o_ref):
    half = fox_ref.shape[1]
    y = (jnp.dot(fox_ref[...], w_ref[0:half, :], preferred_element_type=F32)
         + jnp.dot(gla_ref[...], w_ref[half:, :], preferred_element_type=F32))
    inv = lax.rsqrt(jnp.mean(y * y, axis=-1, keepdims=True) + EPS)
    o_ref[...] = x_ref[...] + y * inv * (mod_ref[0, 2:3, :] * g_ref[...])


def _out_proj(fox2, gla2, w_out, x2, mod, g, rows_per_batch, tm=512):
    t, d = x2.shape
    half = fox2.shape[1]
    tiles_per_batch = rows_per_batch // tm
    return pl.pallas_call(
        _outproj_kernel,
        out_shape=jax.ShapeDtypeStruct((t, d), F32),
        grid=(t // tm,),
        in_specs=[pl.BlockSpec((tm, half), lambda i: (i, 0)),
                  pl.BlockSpec((tm, half), lambda i: (i, 0)),
                  pl.BlockSpec((2 * half, d), lambda i: (0, 0)),
                  pl.BlockSpec((tm, d), lambda i: (i, 0)),
                  pl.BlockSpec((1, N_MOD, d), lambda i: (i // tiles_per_batch, 0, 0)),
                  pl.BlockSpec((1, d), lambda i: (0, 0))],
        out_specs=pl.BlockSpec((tm, d), lambda i: (i, 0)),
        compiler_params=pltpu.CompilerParams(dimension_semantics=("parallel",),
                                             vmem_limit_bytes=VMEM_LIMIT),
        name="out_proj",
    )(fox2, gla2, w_out, x2, mod, g)


def _mlp_kernel(x_ref, mod_ref, gpre_ref, gpost_ref, w1_ref, w2_ref, o_ref, h_scr, acc_scr,
                inv_scr):
    f = pl.program_id(1)

    @pl.when(f == 0)
    def _():
        gain = gpre_ref[...] * (1.0 + mod_ref[0, 4:5, :])
        _modulated_norm_rows(x_ref, h_scr, inv_scr, gain, mod_ref[0, 3:4, :])
        acc_scr[...] = jnp.zeros_like(acc_scr)

    u = jnp.maximum(jnp.dot(h_scr[...], w1_ref[...], preferred_element_type=F32), 0.0)
    acc_scr[...] += jnp.dot((u * u).astype(BF16), w2_ref[...], preferred_element_type=F32)

    @pl.when(f == pl.num_programs(1) - 1)
    def _():
        _gated_residual_rows(x_ref, acc_scr, o_ref, inv_scr, mod_ref[0, 5:6, :] * gpost_ref[...])


def _mlp(x1, mod, g_pre, g_post, w1, w2, rows_per_batch, tm=512, tf=1024):
    t, d = x1.shape
    dff = w1.shape[1]
    tiles_per_batch = rows_per_batch // tm
    return pl.pallas_call(
        _mlp_kernel,
        out_shape=jax.ShapeDtypeStruct((t, d), F32),
        grid=(t // tm, dff // tf),
        in_specs=[pl.BlockSpec((tm, d), lambda i, f: (i, 0)),
                  pl.BlockSpec((1, N_MOD, d), lambda i, f: (i // tiles_per_batch, 0, 0)),
                  pl.BlockSpec((1, d), lambda i, f: (0, 0)),
                  pl.BlockSpec((1, d), lambda i, f: (0, 0)),
                  pl.BlockSpec((d, tf), lambda i, f: (0, f)),
                  pl.BlockSpec((tf, d), lambda i, f: (f, 0))],
        out_specs=pl.BlockSpec((tm, d), lambda i, f: (i, 0)),
        scratch_shapes=[pltpu.VMEM((tm, d), BF16), pltpu.VMEM((tm, d), F32),
                        pltpu.VMEM((tm, LANE), F32)],
        compiler_params=pltpu.CompilerParams(dimension_semantics=("parallel", "arbitrary"),
                                             vmem_limit_bytes=VMEM_LIMIT),
        name="mlp",
    )(x1, mod, g_pre, g_post, w1, w2)


def _regroup_in_proj_weight(w_in):
    fw = FOX_HEADS * FOX_HEAD_DIM
    kw = GLA_HEADS * GLA_DK
    vw = GLA_HEADS * GLA_DV
    sizes = (fw, fw, fw, FOX_HEADS, kw, kw, vw, GLA_GATE_RANK, vw)
    offs = [0]
    for sz in sizes:
        offs.append(offs[-1] + sz)
    piece = lambda n: w_in[:, offs[n]:offs[n + 1]]
    main = jnp.concatenate([piece(0), piece(1), piece(2), piece(4), piece(5), piece(6), piece(8)],
                           axis=1).astype(BF16)
    pad = jnp.zeros((w_in.shape[0], LANE - FOX_HEADS - GLA_GATE_RANK), w_in.dtype)
    small = jnp.concatenate([piece(3), piece(7), pad], axis=1).astype(BF16)
    return main, small


def kernel(x, c, w_ada, b_ada, g_pre_mix, g_post_mix, w_in, b_fgate, w_gla_a2, b_gla_a2,
           g_fox_out, g_gla_out, w_out, g_pre_mlp, g_post_mlp, w_mlp_in, w_mlp_out):
    b, s, d = x.shape
    depth = w_ada.shape[0]
    row = lambda v: v.reshape(1, -1)
    for i in range(depth):
        c_pad = jnp.concatenate([c, jnp.zeros((8 - b, d), c.dtype)], axis=0)
        mod = _ada(c_pad, w_ada[i], row(b_ada[i]))[:b].reshape(b, N_MOD, d)

        w_main, w_small = _regroup_in_proj_weight(w_in[i])
        x2 = x.reshape(b * s, d)
        proj, small = _in_proj(x2, mod, row(g_pre_mix[i]), w_main, w_small, s)
        proj3 = proj.reshape(b, s, -1)
        small3 = small.reshape(b, s, LANE)

        fbias = jnp.concatenate([b_fgate[i], jnp.zeros((LANE - FOX_HEADS,), F32)]).reshape(1, LANE)
        fox = _fox_attn(proj3, _fox_cum(small3, fbias),
                        g_fox_out[i].reshape(FOX_HEADS, 1, FOX_HEAD_DIM))

        kw = GLA_HEADS * GLA_DK
        wa_pad = jnp.concatenate(
            [jnp.zeros((FOX_HEADS, kw), F32), w_gla_a2[i],
             jnp.zeros((LANE - FOX_HEADS - GLA_GATE_RANK, kw), F32)], axis=0).astype(BF16)
        gla = _gla(proj3, small3, wa_pad, row(b_gla_a2[i]),
                   g_gla_out[i].reshape(GLA_HEADS, 1, GLA_DV))

        x1 = _out_proj(fox.reshape(b * s, -1), gla.reshape(b * s, -1), w_out[i].astype(BF16),
                       x2, mod, row(g_post_mix[i]), s)
        x2 = _mlp(x1, mod, row(g_pre_mlp[i]), row(g_post_mlp[i]),
                  w_mlp_in[i].astype(BF16), w_mlp_out[i].astype(BF16), s)
        x = x2.reshape(b, s, d)
    return x
```

```python
import functools

import jax
import jax.numpy as jnp
from jax import lax
from jax.experimental import pallas as pl
from jax.experimental.pallas import tpu as pltpu

F32 = jnp.float32
BF16 = jnp.bfloat16

EPS = 1e-6
N_MOD = 6
FOX_HEADS = 8
FOX_HEAD_DIM = 128
GLA_HEADS = 4
GLA_DK = 128
GLA_DV = 256
GLA_GATE_RANK = 16
GLA_GATE_TEMP = 16.0
CHUNK = 64
LANE = 128
VMEM_LIMIT = 56 * 1024 * 1024

NEG_BIG = -1e30
LOG2E = 1.4426950408889634


def _nt_dot(a, b):
    return lax.dot_general(a, b, (((1,), (1,)), ((), ())), preferred_element_type=F32)


def _tn_dot(a, b):
    return lax.dot_general(a, b, (((0,), (0,)), ((), ())), preferred_element_type=F32)


def _log_sigmoid(z):
    return jnp.minimum(z, 0.0) - jnp.log(1.0 + jnp.exp(-jnp.abs(z)))


def _sigmoid(z):
    return 1.0 / (1.0 + jnp.exp(-z))


def _tri_cumsum(tri, v):
    hi = v.astype(BF16)
    lo = (v - hi.astype(F32)).astype(BF16)
    return (jnp.dot(tri, hi, preferred_element_type=F32)
            + jnp.dot(tri, lo, preferred_element_type=F32))


def _lower_tri(n):
    r = lax.broadcasted_iota(jnp.int32, (n, n), 0)
    c = lax.broadcasted_iota(jnp.int32, (n, n), 1)
    return jnp.where(r >= c, 1.0, 0.0).astype(BF16)


def _ada_kernel(c_ref, w_ref, b_ref, o_ref):
    c = c_ref[...]
    act = (c * _sigmoid(c)).astype(BF16)
    o_ref[...] = jnp.dot(act, w_ref[...].astype(BF16), preferred_element_type=F32) + b_ref[...]


def _ada(c_pad, w_ada, b_ada, tn=1024):
    m, d = c_pad.shape
    n = w_ada.shape[1]
    return pl.pallas_call(
        _ada_kernel,
        out_shape=jax.ShapeDtypeStruct((m, n), F32),
        grid=(n // tn,),
        in_specs=[pl.BlockSpec((m, d), lambda j: (0, 0)),
                  pl.BlockSpec((d, tn), lambda j: (0, j)),
                  pl.BlockSpec((1, tn), lambda j: (0, j))],
        out_specs=pl.BlockSpec((m, tn), lambda j: (0, j)),
        compiler_params=pltpu.CompilerParams(dimension_semantics=("parallel",),
                                             vmem_limit_bytes=VMEM_LIMIT),
        name="ada",
    )(c_pad, w_ada, b_ada)


def _inproj_kernel(x_ref, mod_ref, g_ref, w_ref, ws_ref, o_ref, os_ref, h_scr):
    j = pl.program_id(1)

    @pl.when(j == 0)
    def _():
        x = x_ref[...]
        inv = lax.rsqrt(jnp.mean(x * x, axis=-1, keepdims=True) + EPS)
        gain = g_ref[...] * (1.0 + mod_ref[0, 1:2, :])
        hb = (x * inv * gain + mod_ref[0, 0:1, :]).astype(BF16)
        h_scr[...] = hb
        os_ref[...] = jnp.dot(hb, ws_ref[...], preferred_element_type=F32)
        acc = jnp.dot(hb, w_ref[...], preferred_element_type=F32)
        o_ref[...] = (acc * (FOX_HEAD_DIM ** -0.5 * LOG2E)).astype(BF16)

    @pl.when(j > 0)
    def _():
        o_ref[...] = jnp.dot(h_scr[...], w_ref[...], preferred_element_type=F32).astype(BF16)


def _in_proj(x2, mod, g, w_main, w_small, rows_per_batch, tm=1024, tn=1024):
    t, d = x2.shape
    n = w_main.shape[1]
    assert tn == FOX_HEADS * FOX_HEAD_DIM, "column block 0 must be exactly the FoX queries"
    tiles_per_batch = rows_per_batch // tm
    return pl.pallas_call(
        _inproj_kernel,
        out_shape=(jax.ShapeDtypeStruct((t, n), BF16),
                   jax.ShapeDtypeStruct((t, LANE), F32)),
        grid=(t // tm, n // tn),
        in_specs=[pl.BlockSpec((tm, d), lambda i, j: (i, 0)),
                  pl.BlockSpec((1, N_MOD, d), lambda i, j: (i // tiles_per_batch, 0, 0)),
                  pl.BlockSpec((1, d), lambda i, j: (0, 0)),
                  pl.BlockSpec((d, tn), lambda i, j: (0, j)),
                  pl.BlockSpec((d, LANE), lambda i, j: (0, 0))],
        out_specs=(pl.BlockSpec((tm, tn), lambda i, j: (i, j)),
                   pl.BlockSpec((tm, LANE), lambda i, j: (i, 0))),
        scratch_shapes=[pltpu.VMEM((tm, d), BF16)],
        compiler_params=pltpu.CompilerParams(dimension_semantics=("parallel", "arbitrary"),
                                             vmem_limit_bytes=VMEM_LIMIT),
        name="in_proj",
    )(x2, mod, g, w_main, w_small)


def _cum_kernel(s_ref, b_ref, o_ref, *, blk):
    s = s_ref.shape[1]
    tri = _lower_tri(blk)
    col = lax.broadcasted_iota(jnp.int32, (blk, LANE), 1)
    carry = jnp.zeros((1, LANE), F32)
    for r in range(s // blk):
        rows = slice(r * blk, (r + 1) * blk)
        z = s_ref[0, rows, :] + b_ref[...]
        cs = _tri_cumsum(tri, _log_sigmoid(z)) + carry
        carry = cs[blk - 1:blk, :]
        c2 = jnp.where(col < FOX_HEADS, cs * LOG2E, 0.0)
        hi = c2.astype(BF16).astype(F32)
        rem = c2 - hi
        mid = rem.astype(BF16).astype(F32)
        lo = (rem - mid).astype(BF16).astype(F32)
        pieces = hi + pltpu.roll(mid, FOX_HEADS, 1) + pltpu.roll(lo, 2 * FOX_HEADS, 1)
        o_ref[0, rows, :] = pieces.astype(BF16)


def _fox_cum(small3, bias_row, blk=256):
    b, s, _ = small3.shape
    return pl.pallas_call(
        functools.partial(_cum_kernel, blk=blk),
        out_shape=jax.ShapeDtypeStruct((b, s, LANE), BF16),
        grid=(b,),
        in_specs=[pl.BlockSpec((1, s, LANE), lambda i: (i, 0, 0)),
                  pl.BlockSpec((1, LANE), lambda i: (0, 0))],
        out_specs=pl.BlockSpec((1, s, LANE), lambda i: (i, 0, 0)),
        compiler_params=pltpu.CompilerParams(dimension_semantics=("parallel",),
                                             vmem_limit_bytes=VMEM_LIMIT),
        name="fox_cum",
    )(small3, bias_row)


def _fox_kernel(q_ref, k_ref, v_ref, a_ref, g_ref, o_ref, ka_scr, vt_scr, sa_scr, sb_scr, m_scr, l_scr,
                acc_scr, *, tq):
    h = pl.program_id(1)
    i = pl.program_id(2)

    @pl.when(i == 0)
    def _():
        ka_scr[:, 0:LANE] = k_ref[0]
        ka_scr[:, LANE:] = a_ref[0]
        vt_scr[...] = v_ref[0].T

    col = lax.broadcasted_iota(jnp.int32, (tq, LANE), 1)
    pick = (col == h) | (col == h + FOX_HEADS) | (col == h + 2 * FOX_HEADS)
    qa = jnp.concatenate([q_ref[0], jnp.where(pick, -1.0, 0.0).astype(BF16)], axis=1)

    def logits_to(s_ref, blk):
        s_ref[...] = _nt_dot(ka_scr[pl.ds(pl.multiple_of(blk * tq, tq), tq), :], qa)

    def absorb(s_ref, blk, masked):
        st = s_ref[...]
        if masked:
            kr = lax.broadcasted_iota(jnp.int32, (tq, tq), 0)
            qc = lax.broadcasted_iota(jnp.int32, (tq, tq), 1)
            st = jnp.where(kr <= qc, st, NEG_BIG)
        m = m_scr[...]
        m_new = jnp.maximum(m, jnp.max(st, axis=0, keepdims=True))
        alpha = jnp.exp2(m - m_new)
        p = jnp.exp2(st - m_new)
        m_scr[...] = m_new
        l_scr[...] = alpha * l_scr[...] + jnp.sum(p, axis=0, keepdims=True)
        pv = jnp.dot(vt_scr[:, pl.ds(pl.multiple_of(blk * tq, tq), tq)], p.astype(BF16),
                     preferred_element_type=F32)
        acc_scr[...] = alpha * acc_scr[...] + pv

    m_scr[...] = jnp.full_like(m_scr, NEG_BIG)
    l_scr[...] = jnp.zeros_like(l_scr)
    acc_scr[...] = jnp.zeros_like(acc_scr)

    logits_to(sa_scr, 0)

    def block_pair(t, carry):
        logits_to(sb_scr, 2 * t + 1)
        absorb(sa_scr, 2 * t, False)
        logits_to(sa_scr, 2 * t + 2)
        absorb(sb_scr, 2 * t + 1, False)
        return carry

    lax.fori_loop(0, i // 2, block_pair, 0)

    @pl.when(i % 2 == 0)
    def _():
        absorb(sa_scr, i, True)

    @pl.when(i % 2 == 1)
    def _():
        logits_to(sb_scr, i)
        absorb(sa_scr, i - 1, False)
        absorb(sb_scr, i, True)

    ot = acc_scr[...] / l_scr[...]
    ot = ot * lax.rsqrt(jnp.mean(ot * ot, axis=0, keepdims=True) + EPS)
    o_ref[0] = (ot.T * g_ref[0]).astype(BF16)


def _fox_attn(proj3, pieces3, g_fox, tq=512):
    b, s, _ = proj3.shape
    h = FOX_HEADS
    return pl.pallas_call(
        functools.partial(_fox_kernel, tq=tq),
        out_shape=jax.ShapeDtypeStruct((b, s, h * FOX_HEAD_DIM), BF16),
        grid=(b, h, s // tq),
        in_specs=[pl.BlockSpec((1, tq, LANE), lambda bi, hi, qi: (bi, qi, hi)),
                  pl.BlockSpec((1, s, LANE), lambda bi, hi, qi: (bi, 0, h + hi)),
                  pl.BlockSpec((1, s, LANE), lambda bi, hi, qi: (bi, 0, 2 * h + hi)),
                  pl.BlockSpec((1, s, LANE), lambda bi, hi, qi: (bi, 0, 0)),
                  pl.BlockSpec((1, 1, LANE), lambda bi, hi, qi: (hi, 0, 0))],
        out_specs=pl.BlockSpec((1, tq, LANE), lambda bi, hi, qi: (bi, qi, hi)),
        scratch_shapes=[pltpu.VMEM((s, 2 * LANE), BF16), pltpu.VMEM((FOX_HEAD_DIM, s), BF16),
                        pltpu.VMEM((tq, tq), F32), pltpu.VMEM((tq, tq), F32),
                        pltpu.VMEM((1, tq), F32), pltpu.VMEM((1, tq), F32),
                        pltpu.VMEM((FOX_HEAD_DIM, tq), F32)],
        compiler_params=pltpu.CompilerParams(
            dimension_semantics=("parallel", "parallel", "arbitrary"),
            vmem_limit_bytes=VMEM_LIMIT),
        name="fox_attn",
    )(proj3, proj3, proj3, pieces3, g_fox)


def _gla_kernel(q_ref, k_ref, v_ref, r_ref, s_ref, wa_ref, ba_ref, gg_ref, o_ref, st_ref, *, ts):
    @pl.when(pl.program_id(2) == 0)
    def _():
        st_ref[...] = jnp.zeros_like(st_ref)

    pre = jnp.dot(s_ref[0].astype(BF16), wa_ref[...], preferred_element_type=F32) + ba_ref[...]
    la = _log_sigmoid(pre) * (1.0 / GLA_GATE_TEMP)
    tri = _lower_tri(CHUNK)
    chunks = [slice(c * CHUNK, (c + 1) * CHUNK) for c in range(ts // CHUNK)]
    cums = [_tri_cumsum(tri, la[sl]) for sl in chunks]
    tots = [cum[CHUNK - 1:CHUNK, :] for cum in cums]
    ups = []
    for sl, cum, tot in zip(chunks, cums, tots):
        kd = (k_ref[0, sl, :].astype(F32) * jnp.exp(tot - cum)).astype(BF16)
        ups.append(_tn_dot(v_ref[0, sl, :], kd))
    st = st_ref[...]
    states = []
    for tot, up in zip(tots, ups):
        st = st * jnp.exp(tot) + up
        states.append(st.astype(BF16))
    st_ref[...] = st
    outs = [_nt_dot(q_ref[0, sl, :], sb) for sl, sb in zip(chunks, states)]
    o = jnp.concatenate(outs, axis=0) * (GLA_DK ** -0.5)
    o = o * lax.rsqrt(jnp.mean(o * o, axis=-1, keepdims=True) + EPS) * gg_ref[0]
    r = r_ref[0].astype(F32)
    o_ref[0] = (o * (r * _sigmoid(r))).astype(BF16)


def _gla(proj3, small3, wa_pad, ba_row, g_gla, ts=512):
    b, s, _ = proj3.shape
    q_blk = 3 * FOX_HEADS
    k_blk = q_blk + GLA_HEADS
    v_blk = (k_blk + GLA_HEADS) // 2
    r_blk = v_blk + GLA_HEADS
    return pl.pallas_call(
        functools.partial(_gla_kernel, ts=ts),
        out_shape=jax.ShapeDtypeStruct((b, s, GLA_HEADS * GLA_DV), BF16),
        grid=(b, GLA_HEADS, s // ts),
        in_specs=[pl.BlockSpec((1, ts, GLA_DK), lambda bi, gi, ti: (bi, ti, q_blk + gi)),
                  pl.BlockSpec((1, ts, GLA_DK), lambda bi, gi, ti: (bi, ti, k_blk + gi)),
                  pl.BlockSpec((1, ts, GLA_DV), lambda bi, gi, ti: (bi, ti, v_blk + gi)),
                  pl.BlockSpec((1, ts, GLA_DV), lambda bi, gi, ti: (bi, ti, r_blk + gi)),
                  pl.BlockSpec((1, ts, LANE), lambda bi, gi, ti: (bi, ti, 0)),
                  pl.BlockSpec((LANE, GLA_DK), lambda bi, gi, ti: (0, gi)),
                  pl.BlockSpec((1, GLA_DK), lambda bi, gi, ti: (0, gi)),
                  pl.BlockSpec((1, 1, GLA_DV), lambda bi, gi, ti: (gi, 0, 0))],
        out_specs=pl.BlockSpec((1, ts, GLA_DV), lambda bi, gi, ti: (bi, ti, gi)),
        scratch_shapes=[pltpu.VMEM((GLA_DV, GLA_DK), F32)],
        compiler_params=pltpu.CompilerParams(
            dimension_semantics=("parallel", "parallel", "arbitrary"),
            vmem_limit_bytes=VMEM_LIMIT),
        name="gla",
    )(proj3, proj3, proj3, proj3, small3, wa_pad, ba_row, g_gla)


def _outproj_kernel(fox_ref, gla_ref, w_ref, x_ref, mod_ref, g_ref, o_ref):
    half = fox_ref.shape[1]
    y = (jnp.dot(fox_ref[...], w_ref[0:half, :], preferred_element_type=F32)
         + jnp.dot(gla_ref[...], w_ref[half:, :], preferred_element_type=F32))
    inv = lax.rsqrt(jnp.mean(y * y, axis=-1, keepdims=True) + EPS)
    o_ref[...] = x_ref[...] + y * inv * (mod_ref[0, 2:3, :] * g_ref[...])


def _out_proj(fox2, gla2, w_out, x2, mod, g, rows_per_batch, tm=512):
    t, d = x2.shape
    half = fox2.shape[1]
    tiles_per_batch = rows_per_batch // tm
    return pl.pallas_call(
        _outproj_kernel,
        out_shape=jax.ShapeDtypeStruct((t, d), F32),
        grid=(t // tm,),
        in_specs=[pl.BlockSpec((tm, half), lambda i: (i, 0)),
                  pl.BlockSpec((tm, half), lambda i: (i, 0)),
                  pl.BlockSpec((2 * half, d), lambda i: (0, 0)),
                  pl.BlockSpec((tm, d), lambda i: (i, 0)),
                  pl.BlockSpec((1, N_MOD, d), lambda i: (i // tiles_per_batch, 0, 0)),
                  pl.BlockSpec((1, d), lambda i: (0, 0))],
        out_specs=pl.BlockSpec((tm, d), lambda i: (i, 0)),
        compiler_params=pltpu.CompilerParams(dimension_semantics=("parallel",),
                                             vmem_limit_bytes=VMEM_LIMIT),
        name="out_proj",
    )(fox2, gla2, w_out, x2, mod, g)


def _mlp_kernel(x_ref, mod_ref, gpre_ref, gpost_ref, w1_ref, w2_ref, o_ref, h_scr, acc_scr):
    f = pl.program_id(1)

    def hidden_block(hb):
        u = jnp.maximum(jnp.dot(hb, w1_ref[...], preferred_element_type=F32), 0.0)
        return jnp.dot((u * u).astype(BF16), w2_ref[...], preferred_element_type=F32)

    @pl.when(f == 0)
    def _():
        x = x_ref[...]
        inv = lax.rsqrt(jnp.mean(x * x, axis=-1, keepdims=True) + EPS)
        gain = gpre_ref[...] * (1.0 + mod_ref[0, 4:5, :])
        hb = (x * inv * gain + mod_ref[0, 3:4, :]).astype(BF16)
        h_scr[...] = hb
        acc_scr[...] = hidden_block(hb)

    last = pl.num_programs(1) - 1

    @pl.when((f > 0) & (f < last))
    def _():
        acc_scr[...] += hidden_block(h_scr[...])

    @pl.when(f == last)
    def _():
        gain = mod_ref[0, 5:6, :] * gpost_ref[...]
        half = x_ref.shape[0] // 2
        for rows in (slice(0, half), slice(half, 2 * half)):
            y = acc_scr[rows, :] + hidden_block(h_scr[rows, :])
            inv = lax.rsqrt(jnp.mean(y * y, axis=-1, keepdims=True) + EPS)
            o_ref[rows, :] = x_ref[rows, :] + y * inv * gain


def _mlp(x1, mod, g_pre, g_post, w1, w2, rows_per_batch, tm=512, tf=1024):
    t, d = x1.shape
    dff = w1.shape[1]
    tiles_per_batch = rows_per_batch // tm
    return pl.pallas_call(
        _mlp_kernel,
        out_shape=jax.ShapeDtypeStruct((t, d), F32),
        grid=(t // tm, dff // tf),
        in_specs=[pl.BlockSpec((tm, d), lambda i, f: (i, 0)),
                  pl.BlockSpec((1, N_MOD, d), lambda i, f: (i // tiles_per_batch, 0, 0)),
                  pl.BlockSpec((1, d), lambda i, f: (0, 0)),
                  pl.BlockSpec((1, d), lambda i, f: (0, 0)),
                  pl.BlockSpec((d, tf), lambda i, f: (0, f)),
                  pl.BlockSpec((tf, d), lambda i, f: (f, 0))],
        out_specs=pl.BlockSpec((tm, d), lambda i, f: (i, 0)),
        scratch_shapes=[pltpu.VMEM((tm, d), BF16), pltpu.VMEM((tm, d), F32)],
        compiler_params=pltpu.CompilerParams(dimension_semantics=("parallel", "arbitrary"),
                                             vmem_limit_bytes=VMEM_LIMIT),
        name="mlp",
    )(x1, mod, g_pre, g_post, w1, w2)


def _regroup_in_proj_weight(w_in):
    fw = FOX_HEADS * FOX_HEAD_DIM
    kw = GLA_HEADS * GLA_DK
    vw = GLA_HEADS * GLA_DV
    sizes = (fw, fw, fw, FOX_HEADS, kw, kw, vw, GLA_GATE_RANK, vw)
    offs = [0]
    for sz in sizes:
        offs.append(offs[-1] + sz)
    piece = lambda n: w_in[:, offs[n]:offs[n + 1]]
    main = jnp.concatenate([piece(0), piece(1), piece(2), piece(4), piece(5), piece(6), piece(8)],
                           axis=1).astype(BF16)
    pad = jnp.zeros((w_in.shape[0], LANE - FOX_HEADS - GLA_GATE_RANK), w_in.dtype)
    small = jnp.concatenate([piece(3), piece(7), pad], axis=1).astype(BF16)
    return main, small


def kernel(x, c, w_ada, b_ada, g_pre_mix, g_post_mix, w_in, b_fgate, w_gla_a2, b_gla_a2,
           g_fox_out, g_gla_out, w_out, g_pre_mlp, g_post_mlp, w_mlp_in, w_mlp_out):
    b, s, d = x.shape
    depth = w_ada.shape[0]
    row = lambda v: v.reshape(1, -1)
    for i in range(depth):
        c_pad = jnp.concatenate([c, jnp.zeros((8 - b, d), c.dtype)], axis=0)
        mod = _ada(c_pad, w_ada[i], row(b_ada[i]))[:b].reshape(b, N_MOD, d)

        w_main, w_small = _regroup_in_proj_weight(w_in[i])
        x2 = x.reshape(b * s, d)
        proj, small = _in_proj(x2, mod, row(g_pre_mix[i]), w_main, w_small, s)
        proj3 = proj.reshape(b, s, -1)
        small3 = small.reshape(b, s, LANE)

        fbias = jnp.concatenate([b_fgate[i], jnp.zeros((LANE - FOX_HEADS,), F32)]).reshape(1, LANE)
        fox = _fox_attn(proj3, _fox_cum(small3, fbias),
                        g_fox_out[i].reshape(FOX_HEADS, 1, FOX_HEAD_DIM))

        kw = GLA_HEADS * GLA_DK
        wa_pad = jnp.concatenate(
            [jnp.zeros((FOX_HEADS, kw), F32), w_gla_a2[i],
             jnp.zeros((LANE - FOX_HEADS - GLA_GATE_RANK, kw), F32)], axis=0).astype(BF16)
        gla = _gla(proj3, small3, wa_pad, row(b_gla_a2[i]),
                   g_gla_out[i].reshape(GLA_HEADS, 1, GLA_DV))

        x1 = _out_proj(fox.reshape(b * s, -1), gla.reshape(b * s, -1), w_out[i].astype(BF16),
                       x2, mod, row(g_post_mix[i]), s)
        x2 = _mlp(x1, mod, row(g_pre_mlp[i]), row(g_post_mlp[i]),
                  w_mlp_in[i].astype(BF16), w_mlp_out[i].astype(BF16), s)
        x = x2.reshape(b, s, d)
    return x
```

```python
import functools

import jax
import jax.numpy as jnp
from jax import lax
from jax.experimental import pallas as pl
from jax.experimental.pallas import tpu as pltpu

F32 = jnp.float32
BF16 = jnp.bfloat16

EPS = 1e-6
N_MOD = 6
FOX_HEADS = 8
FOX_HEAD_DIM = 128
GLA_HEADS = 4
GLA_DK = 128
GLA_DV = 256
GLA_GATE_RANK = 16
GLA_GATE_TEMP = 16.0
CHUNK = 64
LANE = 128
VMEM_LIMIT = 56 * 1024 * 1024

NEG_BIG = -1e30
LOG2E = 1.4426950408889634
FOX_HEADS_PER_STEP = 2


def _nt_dot(a, b):
    return lax.dot_general(a, b, (((1,), (1,)), ((), ())), preferred_element_type=F32)


def _tn_dot(a, b):
    return lax.dot_general(a, b, (((0,), (0,)), ((), ())), preferred_element_type=F32)


def _log_sigmoid(z):
    return jnp.minimum(z, 0.0) - jnp.log(1.0 + jnp.exp(-jnp.abs(z)))


def _sigmoid(z):
    return 1.0 / (1.0 + jnp.exp(-z))


def _tri_cumsum(tri, v):
    hi = v.astype(BF16)
    lo = (v - hi.astype(F32)).astype(BF16)
    return (jnp.dot(tri, hi, preferred_element_type=F32)
            + jnp.dot(tri, lo, preferred_element_type=F32))


def _lower_tri(n, value=1.0):
    r = lax.broadcasted_iota(jnp.int32, (n, n), 0)
    c = lax.broadcasted_iota(jnp.int32, (n, n), 1)
    return jnp.where(r >= c, value, 0.0).astype(BF16)


def _ada_kernel(c_ref, w_ref, b_ref, o_ref):
    c = c_ref[...]
    act = (c * _sigmoid(c)).astype(BF16)
    o_ref[...] = jnp.dot(act, w_ref[...].astype(BF16), preferred_element_type=F32) + b_ref[...]


def _ada(c_pad, w_ada, b_ada, tn=1024):
    m, d = c_pad.shape
    n = w_ada.shape[1]
    return pl.pallas_call(
        _ada_kernel,
        out_shape=jax.ShapeDtypeStruct((m, n), F32),
        grid=(n // tn,),
        in_specs=[pl.BlockSpec((m, d), lambda j: (0, 0)),
                  pl.BlockSpec((d, tn), lambda j: (0, j)),
                  pl.BlockSpec((1, tn), lambda j: (0, j))],
        out_specs=pl.BlockSpec((m, tn), lambda j: (0, j)),
        compiler_params=pltpu.CompilerParams(dimension_semantics=("parallel",),
                                             vmem_limit_bytes=VMEM_LIMIT),
        name="ada",
    )(c_pad, w_ada, b_ada)


def _inproj_kernel(x_ref, mod_ref, g_ref, w_ref, ws_ref, o_ref, os_ref, h_scr):
    j = pl.program_id(1)

    @pl.when(j == 0)
    def _():
        x = x_ref[...]
        inv = lax.rsqrt(jnp.mean(x * x, axis=-1, keepdims=True) + EPS)
        gain = g_ref[...] * (1.0 + mod_ref[0, 1:2, :])
        hb = (x * inv * gain + mod_ref[0, 0:1, :]).astype(BF16)
        h_scr[...] = hb
        os_ref[...] = jnp.dot(hb, ws_ref[...], preferred_element_type=F32)
        acc = jnp.dot(hb, w_ref[...], preferred_element_type=F32)
        o_ref[...] = (acc * (FOX_HEAD_DIM ** -0.5 * LOG2E)).astype(BF16)

    @pl.when(j > 0)
    def _():
        o_ref[...] = jnp.dot(h_scr[...], w_ref[...], preferred_element_type=F32).astype(BF16)


def _in_proj(x2, mod, g, w_main, w_small, rows_per_batch, tm=1024, tn=1024):
    t, d = x2.shape
    n = w_main.shape[1]
    assert tn == FOX_HEADS * FOX_HEAD_DIM, "column block 0 must be exactly the FoX queries"
    tiles_per_batch = rows_per_batch // tm
    return pl.pallas_call(
        _inproj_kernel,
        out_shape=(jax.ShapeDtypeStruct((t, n), BF16),
                   jax.ShapeDtypeStruct((t, LANE), F32)),
        grid=(t // tm, n // tn),
        in_specs=[pl.BlockSpec((tm, d), lambda i, j: (i, 0)),
                  pl.BlockSpec((1, N_MOD, d), lambda i, j: (i // tiles_per_batch, 0, 0)),
                  pl.BlockSpec((1, d), lambda i, j: (0, 0)),
                  pl.BlockSpec((d, tn), lambda i, j: (0, j)),
                  pl.BlockSpec((d, LANE), lambda i, j: (0, 0))],
        out_specs=(pl.BlockSpec((tm, tn), lambda i, j: (i, j)),
                   pl.BlockSpec((tm, LANE), lambda i, j: (i, 0))),
        scratch_shapes=[pltpu.VMEM((tm, d), BF16)],
        compiler_params=pltpu.CompilerParams(dimension_semantics=("parallel", "arbitrary"),
                                             vmem_limit_bytes=VMEM_LIMIT),
        name="in_proj",
    )(x2, mod, g, w_main, w_small)


def _cum_kernel(s_ref, b_ref, o_ref, *, blk):
    s = s_ref.shape[1]
    tri = _lower_tri(blk)
    col = lax.broadcasted_iota(jnp.int32, (blk, LANE), 1)
    carry = jnp.zeros((1, LANE), F32)
    for r in range(s // blk):
        rows = slice(r * blk, (r + 1) * blk)
        z = s_ref[0, rows, :] + b_ref[...]
        cs = _tri_cumsum(tri, _log_sigmoid(z)) + carry
        carry = cs[blk - 1:blk, :]
        c2 = jnp.where(col < FOX_HEADS, cs * LOG2E, 0.0)
        hi = c2.astype(BF16).astype(F32)
        rem = c2 - hi
        mid = rem.astype(BF16).astype(F32)
        lo = (rem - mid).astype(BF16).astype(F32)
        pieces = hi + pltpu.roll(mid, FOX_HEADS, 1) + pltpu.roll(lo, 2 * FOX_HEADS, 1)
        o_ref[0, rows, :] = pieces.astype(BF16)


def _fox_cum(small3, bias_row, blk=256):
    b, s, _ = small3.shape
    return pl.pallas_call(
        functools.partial(_cum_kernel, blk=blk),
        out_shape=jax.ShapeDtypeStruct((b, s, LANE), BF16),
        grid=(b,),
        in_specs=[pl.BlockSpec((1, s, LANE), lambda i: (i, 0, 0)),
                  pl.BlockSpec((1, LANE), lambda i: (0, 0))],
        out_specs=pl.BlockSpec((1, s, LANE), lambda i: (i, 0, 0)),
        compiler_params=pltpu.CompilerParams(dimension_semantics=("parallel",),
                                             vmem_limit_bytes=VMEM_LIMIT),
        name="fox_cum",
    )(small3, bias_row)


class _FoxHead:
    def __init__(self, ka, vt, sa, sb, m, l, acc):
        self.ka, self.vt, self.sa, self.sb, self.m, self.l, self.acc = ka, vt, sa, sb, m, l, acc


def _fox_kernel(q_ref, k_ref, v_ref, a_ref, g_ref, o_ref, *scratch, tq):
    heads = [_FoxHead(*scratch[n * 7:(n + 1) * 7]) for n in range(FOX_HEADS_PER_STEP)]
    i = pl.program_id(2)
    lanes = lambda n: slice(n * LANE, (n + 1) * LANE)

    @pl.when(i == 0)
    def _():
        for n, hd in enumerate(heads):
            hd.ka[:, 0:LANE] = k_ref[0, :, lanes(n)]
            hd.ka[:, LANE:] = a_ref[0]
            hd.vt[...] = v_ref[0, :, lanes(n)].T

    col = lax.broadcasted_iota(jnp.int32, (tq, LANE), 1)
    qas = []
    for n, hd in enumerate(heads):
        h = pl.program_id(1) * FOX_HEADS_PER_STEP + n
        pick = (col == h) | (col == h + FOX_HEADS) | (col == h + 2 * FOX_HEADS)
        qas.append(jnp.concatenate([q_ref[0, :, lanes(n)],
                                    jnp.where(pick, -1.0, 0.0).astype(BF16)], axis=1))
        hd.m[...] = jnp.full_like(hd.m, NEG_BIG)
        hd.l[...] = jnp.zeros_like(hd.l)
        hd.acc[...] = jnp.zeros_like(hd.acc)

    def logits_to(pick_buf, blk):
        keys = pl.ds(pl.multiple_of(blk * tq, tq), tq)
        for hd, qa in zip(heads, qas):
            pick_buf(hd)[...] = _nt_dot(hd.ka[keys, :], qa)

    def absorb(pick_buf, blk, masked):
        keys = pl.ds(pl.multiple_of(blk * tq, tq), tq)
        for hd in heads:
            st = pick_buf(hd)[...]
            if masked:
                kr = lax.broadcasted_iota(jnp.int32, (tq, tq), 0)
                qc = lax.broadcasted_iota(jnp.int32, (tq, tq), 1)
                st = jnp.where(kr <= qc, st, NEG_BIG)
            m = hd.m[...]
            m_new = jnp.maximum(m, jnp.max(st, axis=0, keepdims=True))
            alpha = jnp.exp2(m - m_new)
            p = jnp.exp2(st - m_new)
            hd.m[...] = m_new
            hd.l[...] = alpha * hd.l[...] + jnp.sum(p, axis=0, keepdims=True)
            pv = jnp.dot(hd.vt[:, keys], p.astype(BF16), preferred_element_type=F32)
            hd.acc[...] = alpha * hd.acc[...] + pv

    buf_a = lambda hd: hd.sa
    buf_b = lambda hd: hd.sb

    logits_to(buf_a, 0)

    def block_pair(t, carry):
        logits_to(buf_b, 2 * t + 1)
        absorb(buf_a, 2 * t, False)
        logits_to(buf_a, 2 * t + 2)
        absorb(buf_b, 2 * t + 1, False)
        return carry

    lax.fori_loop(0, i // 2, block_pair, 0)

    @pl.when(i % 2 == 0)
    def _():
        absorb(buf_a, i, True)

    @pl.when(i % 2 == 1)
    def _():
        logits_to(buf_b, i)
        absorb(buf_a, i - 1, False)
        absorb(buf_b, i, True)

    for n, hd in enumerate(heads):
        ot = hd.acc[...] / hd.l[...]
        ot = ot * lax.rsqrt(jnp.mean(ot * ot, axis=0, keepdims=True) + EPS)
        o_ref[0, :, lanes(n)] = (ot.T * g_ref[n]).astype(BF16)


def _fox_attn(proj3, pieces3, g_fox, tq=512):
    b, s, _ = proj3.shape
    nh = FOX_HEADS_PER_STEP
    groups = FOX_HEADS // nh
    width = nh * FOX_HEAD_DIM
    per_head = [pltpu.VMEM((s, 2 * LANE), BF16), pltpu.VMEM((FOX_HEAD_DIM, s), BF16),
                pltpu.VMEM((tq, tq), F32), pltpu.VMEM((tq, tq), F32),
                pltpu.VMEM((1, tq), F32), pltpu.VMEM((1, tq), F32),
                pltpu.VMEM((FOX_HEAD_DIM, tq), F32)]
    return pl.pallas_call(
        functools.partial(_fox_kernel, tq=tq),
        out_shape=jax.ShapeDtypeStruct((b, s, FOX_HEADS * FOX_HEAD_DIM), BF16),
        grid=(b, groups, s // tq),
        in_specs=[pl.BlockSpec((1, tq, width), lambda bi, gi, qi: (bi, qi, gi)),
                  pl.BlockSpec((1, s, width), lambda bi, gi, qi: (bi, 0, groups + gi)),
                  pl.BlockSpec((1, s, width), lambda bi, gi, qi: (bi, 0, 2 * groups + gi)),
                  pl.BlockSpec((1, s, LANE), lambda bi, gi, qi: (bi, 0, 0)),
                  pl.BlockSpec((nh, 1, LANE), lambda bi, gi, qi: (gi, 0, 0))],
        out_specs=pl.BlockSpec((1, tq, width), lambda bi, gi, qi: (bi, qi, gi)),
        scratch_shapes=per_head * nh,
        compiler_params=pltpu.CompilerParams(
            dimension_semantics=("parallel", "parallel", "arbitrary"),
            vmem_limit_bytes=VMEM_LIMIT),
        name="fox_attn",
    )(proj3, proj3, proj3, pieces3, g_fox)


def _gla_kernel(q_ref, k_ref, v_ref, r_ref, s_ref, wa_ref, ba_ref, gg_ref, o_ref, st_ref, *, ts):
    @pl.when(pl.program_id(2) == 0)
    def _():
        st_ref[...] = jnp.zeros_like(st_ref)

    pre = jnp.dot(s_ref[0].astype(BF16), wa_ref[...], preferred_element_type=F32) + ba_ref[...]
    la = _log_sigmoid(pre)
    tri = _lower_tri(CHUNK, 1.0 / GLA_GATE_TEMP)
    chunks = [slice(c * CHUNK, (c + 1) * CHUNK) for c in range(ts // CHUNK)]
    cums = [_tri_cumsum(tri, la[sl]) for sl in chunks]
    tots = [cum[CHUNK - 1:CHUNK, :] for cum in cums]
    ups = []
    for sl, cum, tot in zip(chunks, cums, tots):
        kd = (k_ref[0, sl, :].astype(F32) * jnp.exp(tot - cum)).astype(BF16)
        ups.append(_tn_dot(v_ref[0, sl, :], kd))
    st = st_ref[...]
    states = []
    for tot, up in zip(tots, ups):
        st = st * jnp.exp(tot) + up
        states.append(st.astype(BF16))
    st_ref[...] = st
    outs = [_nt_dot(q_ref[0, sl, :], sb) for sl, sb in zip(chunks, states)]
    o = jnp.concatenate(outs, axis=0)
    qs = GLA_DK ** -0.5
    inv = qs * lax.rsqrt(qs * qs * jnp.mean(o * o, axis=-1, keepdims=True) + EPS)
    o = o * inv * gg_ref[0]
    r = r_ref[0].astype(F32)
    o_ref[0] = (o * (r * _sigmoid(r))).astype(BF16)


def _gla(proj3, small3, wa_pad, ba_row, g_gla, ts=1024):
    b, s, _ = proj3.shape
    q_blk = 3 * FOX_HEADS
    k_blk = q_blk + GLA_HEADS
    v_blk = (k_blk + GLA_HEADS) // 2
    r_blk = v_blk + GLA_HEADS
    return pl.pallas_call(
        functools.partial(_gla_kernel, ts=ts),
        out_shape=jax.ShapeDtypeStruct((b, s, GLA_HEADS * GLA_DV), BF16),
        grid=(b, GLA_HEADS, s // ts),
        in_specs=[pl.BlockSpec((1, ts, GLA_DK), lambda bi, gi, ti: (bi, ti, q_blk + gi)),
                  pl.BlockSpec((1, ts, GLA_DK), lambda bi, gi, ti: (bi, ti, k_blk + gi)),
                  pl.BlockSpec((1, ts, GLA_DV), lambda bi, gi, ti: (bi, ti, v_blk + gi)),
                  pl.BlockSpec((1, ts, GLA_DV), lambda bi, gi, ti: (bi, ti, r_blk + gi)),
                  pl.BlockSpec((1, ts, LANE), lambda bi, gi, ti: (bi, ti, 0)),
                  pl.BlockSpec((LANE, GLA_DK), lambda bi, gi, ti: (0, gi)),
                  pl.BlockSpec((1, GLA_DK), lambda bi, gi, ti: (0, gi)),
                  pl.BlockSpec((1, 1, GLA_DV), lambda bi, gi, ti: (gi, 0, 0))],
        out_specs=pl.BlockSpec((1, ts, GLA_DV), lambda bi, gi, ti: (bi, ti, gi)),
        scratch_shapes=[pltpu.VMEM((GLA_DV, GLA_DK), F32)],
        compiler_params=pltpu.CompilerParams(
            dimension_semantics=("parallel", "parallel", "arbitrary"),
            vmem_limit_bytes=VMEM_LIMIT),
        name="gla",
    )(proj3, proj3, proj3, proj3, small3, wa_pad, ba_row, g_gla)


def _outproj_kernel(fox_ref, gla_ref, w_ref, x_ref, mod_ref, g_ref, o_ref):
    half = fox_ref.shape[1]
    y = (jnp.dot(fox_ref[...], w_ref[0:half, :], preferred_element_type=F32)
         + jnp.dot(gla_ref[...], w_ref[half:, :], preferred_element_type=F32))
    inv = lax.rsqrt(jnp.mean(y * y, axis=-1, keepdims=True) + EPS)
    o_ref[...] = x_ref[...] + y * inv * (mod_ref[0, 2:3, :] * g_ref[...])


def _out_proj(fox2, gla2, w_out, x2, mod, g, rows_per_batch, tm=512):
    t, d = x2.shape
    half = fox2.shape[1]
    tiles_per_batch = rows_per_batch // tm
    return pl.pallas_call(
        _outproj_kernel,
        out_shape=jax.ShapeDtypeStruct((t, d), F32),
        grid=(t // tm,),
        in_specs=[pl.BlockSpec((tm, half), lambda i: (i, 0)),
                  pl.BlockSpec((tm, half), lambda i: (i, 0)),
                  pl.BlockSpec((2 * half, d), lambda i: (0, 0)),
                  pl.BlockSpec((tm, d), lambda i: (i, 0)),
                  pl.BlockSpec((1, N_MOD, d), lambda i: (i // tiles_per_batch, 0, 0)),
                  pl.BlockSpec((1, d), lambda i: (0, 0))],
        out_specs=pl.BlockSpec((tm, d), lambda i: (i, 0)),
        compiler_params=pltpu.CompilerParams(dimension_semantics=("parallel",),
                                             vmem_limit_bytes=VMEM_LIMIT),
        name="out_proj",
    )(fox2, gla2, w_out, x2, mod, g)


def _mlp_kernel(x_ref, mod_ref, gpre_ref, gpost_ref, w1_ref, w2_ref, o_ref, h_scr, acc_scr):
    f = pl.program_id(1)

    def hidden_block(hb):
        u = jnp.maximum(jnp.dot(hb, w1_ref[...], preferred_element_type=F32), 0.0)
        return jnp.dot((u * u).astype(BF16), w2_ref[...], preferred_element_type=F32)

    @pl.when(f == 0)
    def _():
        x = x_ref[...]
        inv = lax.rsqrt(jnp.mean(x * x, axis=-1, keepdims=True) + EPS)
        gain = gpre_ref[...] * (1.0 + mod_ref[0, 4:5, :])
        hb = (x * inv * gain + mod_ref[0, 3:4, :]).astype(BF16)
        h_scr[...] = hb
        acc_scr[...] = hidden_block(hb)

    last = pl.num_programs(1) - 1

    @pl.when((f > 0) & (f < last))
    def _():
        acc_scr[...] += hidden_block(h_scr[...])

    @pl.when(f == last)
    def _():
        gain = mod_ref[0, 5:6, :] * gpost_ref[...]
        half = x_ref.shape[0] // 2
        for rows in (slice(0, half), slice(half, 2 * half)):
            y = acc_scr[rows, :] + hidden_block(h_scr[rows, :])
            inv = lax.rsqrt(jnp.mean(y * y, axis=-1, keepdims=True) + EPS)
            o_ref[rows, :] = x_ref[rows, :] + y * inv * gain


def _mlp(x1, mod, g_pre, g_post, w1, w2, rows_per_batch, tm=512, tf=1024):
    t, d = x1.shape
    dff = w1.shape[1]
    tiles_per_batch = rows_per_batch // tm
    return pl.pallas_call(
        _mlp_kernel,
        out_shape=jax.ShapeDtypeStruct((t, d), F32),
        grid=(t // tm, dff // tf),
        in_specs=[pl.BlockSpec((tm, d), lambda i, f: (i, 0)),
                  pl.BlockSpec((1, N_MOD, d), lambda i, f: (i // tiles_per_batch, 0, 0)),
                  pl.BlockSpec((1, d), lambda i, f: (0, 0)),
                  pl.BlockSpec((1, d), lambda i, f: (0, 0)),
                  pl.BlockSpec((d, tf), lambda i, f: (0, f)),
                  pl.BlockSpec((tf, d), lambda i, f: (f, 0))],
        out_specs=pl.BlockSpec((tm, d), lambda i, f: (i, 0)),
        scratch_shapes=[pltpu.VMEM((tm, d), BF16), pltpu.VMEM((tm, d), F32)],
        compiler_params=pltpu.CompilerParams(dimension_semantics=("parallel", "arbitrary"),
                                             vmem_limit_bytes=VMEM_LIMIT),
        name="mlp",
    )(x1, mod, g_pre, g_post, w1, w2)


def _regroup_in_proj_weight(w_in):
    fox_end = 3 * FOX_HEADS * FOX_HEAD_DIM
    gla_start = fox_end + FOX_HEADS
    gla_end = gla_start + 2 * GLA_HEADS * GLA_DK + GLA_HEADS * GLA_DV
    gate_start = gla_end + GLA_GATE_RANK
    wb = w_in.astype(BF16)
    main = jnp.concatenate([wb[:, :fox_end], wb[:, gla_start:gla_end], wb[:, gate_start:]], axis=1)
    pad = jnp.zeros((w_in.shape[0], LANE - FOX_HEADS - GLA_GATE_RANK), BF16)
    small = jnp.concatenate([wb[:, fox_end:gla_start], wb[:, gla_end:gate_start], pad], axis=1)
    return main, small


def kernel(x, c, w_ada, b_ada, g_pre_mix, g_post_mix, w_in, b_fgate, w_gla_a2, b_gla_a2,
           g_fox_out, g_gla_out, w_out, g_pre_mlp, g_post_mlp, w_mlp_in, w_mlp_out):
    b, s, d = x.shape
    depth = w_ada.shape[0]
    row = lambda v: v.reshape(1, -1)
    for i in range(depth):
        c_pad = jnp.concatenate([c, jnp.zeros((8 - b, d), c.dtype)], axis=0)
        mod = _ada(c_pad, w_ada[i], row(b_ada[i]))[:b].reshape(b, N_MOD, d)

        w_main, w_small = _regroup_in_proj_weight(w_in[i])
        x2 = x.reshape(b * s, d)
        proj, small = _in_proj(x2, mod, row(g_pre_mix[i]), w_main, w_small, s)
        proj3 = proj.reshape(b, s, -1)
        small3 = small.reshape(b, s, LANE)

        fbias = jnp.concatenate([b_fgate[i], jnp.zeros((LANE - FOX_HEADS,), F32)]).reshape(1, LANE)
        fox = _fox_attn(proj3, _fox_cum(small3, fbias),
                        g_fox_out[i].reshape(FOX_HEADS, 1, FOX_HEAD_DIM))

        kw = GLA_HEADS * GLA_DK
        wa_pad = jnp.concatenate(
            [jnp.zeros((FOX_HEADS, kw), F32), w_gla_a2[i],
             jnp.zeros((LANE - FOX_HEADS - GLA_GATE_RANK, kw), F32)], axis=0).astype(BF16)
        gla = _gla(proj3, small3, wa_pad, row(b_gla_a2[i]),
                   g_gla_out[i].reshape(GLA_HEADS, 1, GLA_DV))

        x1 = _out_proj(fox.reshape(b * s, -1), gla.reshape(b * s, -1), w_out[i].astype(BF16),
                       x2, mod, row(g_post_mix[i]), s)
        x2 = _mlp(x1, mod, row(g_pre_mlp[i]), row(g_post_mlp[i]),
                  w_mlp_in[i].astype(BF16), w_mlp_out[i].astype(BF16), s)
        x = x2.reshape(b, s, d)
    return x
```

```python
import functools

import jax
import jax.numpy as jnp
from jax import lax
from jax.experimental import pallas as pl
from jax.experimental.pallas import tpu as pltpu

F32 = jnp.float32
BF16 = jnp.bfloat16

EPS = 1e-6
N_MOD = 6
FOX_HEADS = 8
FOX_HEAD_DIM = 128
GLA_HEADS = 4
GLA_DK = 128
GLA_DV = 256
GLA_GATE_RANK = 16
GLA_GATE_TEMP = 16.0
CHUNK = 64
LANE = 128
VMEM_LIMIT = 56 * 1024 * 1024

NEG_BIG = -1e30
LOG2E = 1.4426950408889634
FOX_HEADS_PER_STEP = 2


def _nt_dot(a, b):
    return lax.dot_general(a, b, (((1,), (1,)), ((), ())), preferred_element_type=F32)


def _tn_dot(a, b):
    return lax.dot_general(a, b, (((0,), (0,)), ((), ())), preferred_element_type=F32)


def _log_sigmoid(z):
    return jnp.minimum(z, 0.0) - jnp.log(1.0 + jnp.exp(-jnp.abs(z)))


def _sigmoid(z):
    return 1.0 / (1.0 + jnp.exp(-z))


def _tri_cumsum(tri, v):
    hi = v.astype(BF16)
    lo = (v - hi.astype(F32)).astype(BF16)
    return (jnp.dot(tri, hi, preferred_element_type=F32)
            + jnp.dot(tri, lo, preferred_element_type=F32))


def _lower_tri(n, value=1.0):
    r = lax.broadcasted_iota(jnp.int32, (n, n), 0)
    c = lax.broadcasted_iota(jnp.int32, (n, n), 1)
    return jnp.where(r >= c, value, 0.0).astype(BF16)


def _ada_kernel(c_ref, w_ref, b_ref, o_ref):
    c = c_ref[...]
    act = (c * _sigmoid(c)).astype(BF16)
    o_ref[...] = jnp.dot(act, w_ref[...].astype(BF16), preferred_element_type=F32) + b_ref[...]


def _ada(c_pad, w_ada, b_ada, tn=1024):
    m, d = c_pad.shape
    n = w_ada.shape[1]
    return pl.pallas_call(
        _ada_kernel,
        out_shape=jax.ShapeDtypeStruct((m, n), F32),
        grid=(n // tn,),
        in_specs=[pl.BlockSpec((m, d), lambda j: (0, 0)),
                  pl.BlockSpec((d, tn), lambda j: (0, j)),
                  pl.BlockSpec((1, tn), lambda j: (0, j))],
        out_specs=pl.BlockSpec((m, tn), lambda j: (0, j)),
        compiler_params=pltpu.CompilerParams(dimension_semantics=("parallel",),
                                             vmem_limit_bytes=VMEM_LIMIT),
        name="ada",
    )(c_pad, w_ada, b_ada)


def _inproj_kernel(x_ref, mod_ref, g_ref, w_ref, ws_ref, o_ref, os_ref, h_scr):
    j = pl.program_id(1)

    @pl.when(j == 0)
    def _():
        x = x_ref[...]
        inv = lax.rsqrt(jnp.mean(x * x, axis=-1, keepdims=True) + EPS)
        gain = g_ref[...] * (1.0 + mod_ref[0, 1:2, :])
        hb = (x * inv * gain + mod_ref[0, 0:1, :]).astype(BF16)
        h_scr[...] = hb
        os_ref[...] = jnp.dot(hb, ws_ref[...], preferred_element_type=F32)
        acc = jnp.dot(hb, w_ref[...], preferred_element_type=F32)
        qw = FOX_HEADS * FOX_HEAD_DIM
        o_ref[:, :qw] = (acc[:, :qw] * (FOX_HEAD_DIM ** -0.5 * LOG2E)).astype(BF16)
        o_ref[:, qw:] = acc[:, qw:].astype(BF16)

    @pl.when(j > 0)
    def _():
        o_ref[...] = jnp.dot(h_scr[...], w_ref[...], preferred_element_type=F32).astype(BF16)


def _in_proj(x2, mod, g, w_main, w_small, rows_per_batch, tm=1024, tn=2048):
    t, d = x2.shape
    n = w_main.shape[1]
    assert tn > FOX_HEADS * FOX_HEAD_DIM, "column block 0 must contain all FoX queries"
    tiles_per_batch = rows_per_batch // tm
    return pl.pallas_call(
        _inproj_kernel,
        out_shape=(jax.ShapeDtypeStruct((t, n), BF16),
                   jax.ShapeDtypeStruct((t, LANE), F32)),
        grid=(t // tm, n // tn),
        in_specs=[pl.BlockSpec((tm, d), lambda i, j: (i, 0)),
                  pl.BlockSpec((1, N_MOD, d), lambda i, j: (i // tiles_per_batch, 0, 0)),
                  pl.BlockSpec((1, d), lambda i, j: (0, 0)),
                  pl.BlockSpec((d, tn), lambda i, j: (0, j)),
                  pl.BlockSpec((d, LANE), lambda i, j: (0, 0))],
        out_specs=(pl.BlockSpec((tm, tn), lambda i, j: (i, j)),
                   pl.BlockSpec((tm, LANE), lambda i, j: (i, 0))),
        scratch_shapes=[pltpu.VMEM((tm, d), BF16)],
        compiler_params=pltpu.CompilerParams(dimension_semantics=("parallel", "arbitrary"),
                                             vmem_limit_bytes=VMEM_LIMIT),
        name="in_proj",
    )(x2, mod, g, w_main, w_small)


def _cum_kernel(s_ref, b_ref, o_ref, *, blk):
    s = s_ref.shape[1]
    tri = _lower_tri(blk)
    col = lax.broadcasted_iota(jnp.int32, (blk, LANE), 1)
    carry = jnp.zeros((1, LANE), F32)
    for r in range(s // blk):
        rows = slice(r * blk, (r + 1) * blk)
        z = s_ref[0, rows, :] + b_ref[...]
        cs = _tri_cumsum(tri, _log_sigmoid(z)) + carry
        carry = cs[blk - 1:blk, :]
        c2 = jnp.where(col < FOX_HEADS, cs * LOG2E, 0.0)
        hi = c2.astype(BF16).astype(F32)
        rem = c2 - hi
        mid = rem.astype(BF16).astype(F32)
        lo = (rem - mid).astype(BF16).astype(F32)
        pieces = hi + pltpu.roll(mid, FOX_HEADS, 1) + pltpu.roll(lo, 2 * FOX_HEADS, 1)
        o_ref[0, rows, :] = pieces.astype(BF16)


def _fox_cum(small3, bias_row, blk=256):
    b, s, _ = small3.shape
    return pl.pallas_call(
        functools.partial(_cum_kernel, blk=blk),
        out_shape=jax.ShapeDtypeStruct((b, s, LANE), BF16),
        grid=(b,),
        in_specs=[pl.BlockSpec((1, s, LANE), lambda i: (i, 0, 0)),
                  pl.BlockSpec((1, LANE), lambda i: (0, 0))],
        out_specs=pl.BlockSpec((1, s, LANE), lambda i: (i, 0, 0)),
        compiler_params=pltpu.CompilerParams(dimension_semantics=("parallel",),
                                             vmem_limit_bytes=VMEM_LIMIT),
        name="fox_cum",
    )(small3, bias_row)


class _FoxHead:
    def __init__(self, ka, vt, sa, sb, m, l, acc):
        self.ka, self.vt, self.sa, self.sb, self.m, self.l, self.acc = ka, vt, sa, sb, m, l, acc


def _fox_kernel(q_ref, k_ref, v_ref, a_ref, g_ref, o_ref, *scratch, tq):
    heads = [_FoxHead(*scratch[n * 7:(n + 1) * 7]) for n in range(FOX_HEADS_PER_STEP)]
    i = pl.program_id(2)
    lanes = lambda n: slice(n * LANE, (n + 1) * LANE)

    @pl.when(i == 0)
    def _():
        for n, hd in enumerate(heads):
            hd.ka[:, 0:LANE] = k_ref[0, :, lanes(n)]
            hd.ka[:, LANE:] = a_ref[0]
            hd.vt[...] = v_ref[0, :, lanes(n)].T

    col = lax.broadcasted_iota(jnp.int32, (tq, LANE), 1)
    qas = []
    for n, hd in enumerate(heads):
        h = pl.program_id(1) * FOX_HEADS_PER_STEP + n
        pick = (col == h) | (col == h + FOX_HEADS) | (col == h + 2 * FOX_HEADS)
        qas.append(jnp.concatenate([q_ref[0, :, lanes(n)],
                                    jnp.where(pick, -1.0, 0.0).astype(BF16)], axis=1))
        hd.m[...] = jnp.full_like(hd.m, NEG_BIG)
        hd.l[...] = jnp.zeros_like(hd.l)
        hd.acc[...] = jnp.zeros_like(hd.acc)

    def logits_to(pick_buf, blk):
        keys = pl.ds(pl.multiple_of(blk * tq, tq), tq)
        for hd, qa in zip(heads, qas):
            pick_buf(hd)[...] = _nt_dot(hd.ka[keys, :], qa)

    def absorb(pick_buf, blk, masked):
        keys = pl.ds(pl.multiple_of(blk * tq, tq), tq)
        for hd in heads:
            st = pick_buf(hd)[...]
            if masked:
                kr = lax.broadcasted_iota(jnp.int32, (tq, tq), 0)
                qc = lax.broadcasted_iota(jnp.int32, (tq, tq), 1)
                st = jnp.where(kr <= qc, st, NEG_BIG)
            m = hd.m[...]
            m_new = jnp.maximum(m, jnp.max(st, axis=0, keepdims=True))
            alpha = jnp.exp2(m - m_new)
            p = jnp.exp2(st - m_new)
            hd.m[...] = m_new
            hd.l[...] = alpha * hd.l[...] + jnp.sum(p, axis=0, keepdims=True)
            pv = jnp.dot(hd.vt[:, keys], p.astype(BF16), preferred_element_type=F32)
            hd.acc[...] = alpha * hd.acc[...] + pv

    buf_a = lambda hd: hd.sa
    buf_b = lambda hd: hd.sb

    logits_to(buf_a, 0)

    def block_pair(t, carry):
        logits_to(buf_b, 2 * t + 1)
        absorb(buf_a, 2 * t, False)
        logits_to(buf_a, 2 * t + 2)
        absorb(buf_b, 2 * t + 1, False)
        return carry

    lax.fori_loop(0, i // 2, block_pair, 0)

    @pl.when(i % 2 == 0)
    def _():
        absorb(buf_a, i, True)

    @pl.when(i % 2 == 1)
    def _():
        logits_to(buf_b, i)
        absorb(buf_a, i - 1, False)
        absorb(buf_b, i, True)

    for n, hd in enumerate(heads):
        ot = hd.acc[...] / hd.l[...]
        ot = ot * lax.rsqrt(jnp.mean(ot * ot, axis=0, keepdims=True) + EPS)
        o_ref[0, :, lanes(n)] = (ot.T * g_ref[n]).astype(BF16)


def _fox_attn(proj3, pieces3, g_fox, tq=512):
    b, s, _ = proj3.shape
    nh = FOX_HEADS_PER_STEP
    groups = FOX_HEADS // nh
    width = nh * FOX_HEAD_DIM
    per_head = [pltpu.VMEM((s, 2 * LANE), BF16), pltpu.VMEM((FOX_HEAD_DIM, s), BF16),
                pltpu.VMEM((tq, tq), F32), pltpu.VMEM((tq, tq), F32),
                pltpu.VMEM((1, tq), F32), pltpu.VMEM((1, tq), F32),
                pltpu.VMEM((FOX_HEAD_DIM, tq), F32)]
    return pl.pallas_call(
        functools.partial(_fox_kernel, tq=tq),
        out_shape=jax.ShapeDtypeStruct((b, s, FOX_HEADS * FOX_HEAD_DIM), BF16),
        grid=(b, groups, s // tq),
        in_specs=[pl.BlockSpec((1, tq, width), lambda bi, gi, qi: (bi, qi, gi)),
                  pl.BlockSpec((1, s, width), lambda bi, gi, qi: (bi, 0, groups + gi)),
                  pl.BlockSpec((1, s, width), lambda bi, gi, qi: (bi, 0, 2 * groups + gi)),
                  pl.BlockSpec((1, s, LANE), lambda bi, gi, qi: (bi, 0, 0)),
                  pl.BlockSpec((nh, 1, LANE), lambda bi, gi, qi: (gi, 0, 0))],
        out_specs=pl.BlockSpec((1, tq, width), lambda bi, gi, qi: (bi, qi, gi)),
        scratch_shapes=per_head * nh,
        compiler_params=pltpu.CompilerParams(
            dimension_semantics=("parallel", "parallel", "arbitrary"),
            vmem_limit_bytes=VMEM_LIMIT),
        name="fox_attn",
    )(proj3, proj3, proj3, pieces3, g_fox)


def _gla_kernel(q_ref, k_ref, v_ref, r_ref, s_ref, wa_ref, ba_ref, gg_ref, o_ref, st_ref, *, ts):
    @pl.when(pl.program_id(2) == 0)
    def _():
        st_ref[...] = jnp.zeros_like(st_ref)

    pre = jnp.dot(s_ref[0].astype(BF16), wa_ref[...], preferred_element_type=F32) + ba_ref[...]
    la = _log_sigmoid(pre)
    tri = _lower_tri(CHUNK, 1.0 / GLA_GATE_TEMP)
    chunks = [slice(c * CHUNK, (c + 1) * CHUNK) for c in range(ts // CHUNK)]
    cums = [_tri_cumsum(tri, la[sl]) for sl in chunks]
    tots = [cum[CHUNK - 1:CHUNK, :] for cum in cums]
    ups = []
    for sl, cum, tot in zip(chunks, cums, tots):
        kd = (k_ref[0, sl, :].astype(F32) * jnp.exp(tot - cum)).astype(BF16)
        ups.append(_tn_dot(v_ref[0, sl, :], kd))
    st = st_ref[...]
    states = []
    for tot, up in zip(tots, ups):
        st = st * jnp.exp(tot) + up
        states.append(st.astype(BF16))
    st_ref[...] = st
    outs = [_nt_dot(q_ref[0, sl, :], sb) for sl, sb in zip(chunks, states)]
    o = jnp.concatenate(outs, axis=0)
    qs = GLA_DK ** -0.5
    inv = qs * lax.rsqrt(qs * qs * jnp.mean(o * o, axis=-1, keepdims=True) + EPS)
    o = o * inv * gg_ref[0]
    r = r_ref[0].astype(F32)
    o_ref[0] = (o * (r * _sigmoid(r))).astype(BF16)


def _gla(proj3, small3, wa_pad, ba_row, g_gla, ts=1024):
    b, s, _ = proj3.shape
    q_blk = 3 * FOX_HEADS
    k_blk = q_blk + GLA_HEADS
    v_blk = (k_blk + GLA_HEADS) // 2
    r_blk = v_blk + GLA_HEADS
    return pl.pallas_call(
        functools.partial(_gla_kernel, ts=ts),
        out_shape=jax.ShapeDtypeStruct((b, s, GLA_HEADS * GLA_DV), BF16),
        grid=(b, GLA_HEADS, s // ts),
        in_specs=[pl.BlockSpec((1, ts, GLA_DK), lambda bi, gi, ti: (bi, ti, q_blk + gi)),
                  pl.BlockSpec((1, ts, GLA_DK), lambda bi, gi, ti: (bi, ti, k_blk + gi)),
                  pl.BlockSpec((1, ts, GLA_DV), lambda bi, gi, ti: (bi, ti, v_blk + gi)),
                  pl.BlockSpec((1, ts, GLA_DV), lambda bi, gi, ti: (bi, ti, r_blk + gi)),
                  pl.BlockSpec((1, ts, LANE), lambda bi, gi, ti: (bi, ti, 0)),
                  pl.BlockSpec((LANE, GLA_DK), lambda bi, gi, ti: (0, gi)),
                  pl.BlockSpec((1, GLA_DK), lambda bi, gi, ti: (0, gi)),
                  pl.BlockSpec((1, 1, GLA_DV), lambda bi, gi, ti: (gi, 0, 0))],
        out_specs=pl.BlockSpec((1, ts, GLA_DV), lambda bi, gi, ti: (bi, ti, gi)),
        scratch_shapes=[pltpu.VMEM((GLA_DV, GLA_DK), F32)],
        compiler_params=pltpu.CompilerParams(
            dimension_semantics=("parallel", "parallel", "arbitrary"),
            vmem_limit_bytes=VMEM_LIMIT),
        name="gla",
    )(proj3, proj3, proj3, proj3, small3, wa_pad, ba_row, g_gla)


def _outproj_kernel(fox_ref, gla_ref, w_ref, x_ref, mod_ref, g_ref, o_ref):
    half = fox_ref.shape[1]
    y = (jnp.dot(fox_ref[...], w_ref[0:half, :], preferred_element_type=F32)
         + jnp.dot(gla_ref[...], w_ref[half:, :], preferred_element_type=F32))
    inv = lax.rsqrt(jnp.mean(y * y, axis=-1, keepdims=True) + EPS)
    o_ref[...] = x_ref[...] + y * inv * (mod_ref[0, 2:3, :] * g_ref[...])


def _out_proj(fox2, gla2, w_out, x2, mod, g, rows_per_batch, tm=512):
    t, d = x2.shape
    half = fox2.shape[1]
    tiles_per_batch = rows_per_batch // tm
    return pl.pallas_call(
        _outproj_kernel,
        out_shape=jax.ShapeDtypeStruct((t, d), F32),
        grid=(t // tm,),
        in_specs=[pl.BlockSpec((tm, half), lambda i: (i, 0)),
                  pl.BlockSpec((tm, half), lambda i: (i, 0)),
                  pl.BlockSpec((2 * half, d), lambda i: (0, 0)),
                  pl.BlockSpec((tm, d), lambda i: (i, 0)),
                  pl.BlockSpec((1, N_MOD, d), lambda i: (i // tiles_per_batch, 0, 0)),
                  pl.BlockSpec((1, d), lambda i: (0, 0))],
        out_specs=pl.BlockSpec((tm, d), lambda i: (i, 0)),
        compiler_params=pltpu.CompilerParams(dimension_semantics=("parallel",),
                                             vmem_limit_bytes=VMEM_LIMIT),
        name="out_proj",
    )(fox2, gla2, w_out, x2, mod, g)


def _mlp_kernel(x_ref, mod_ref, gpre_ref, gpost_ref, w1_ref, w2_ref, o_ref, h_scr, acc_scr):
    f = pl.program_id(1)

    def hidden_block(hb):
        u = jnp.maximum(jnp.dot(hb, w1_ref[...], preferred_element_type=F32), 0.0)
        return jnp.dot((u * u).astype(BF16), w2_ref[...], preferred_element_type=F32)

    @pl.when(f == 0)
    def _():
        x = x_ref[...]
        inv = lax.rsqrt(jnp.mean(x * x, axis=-1, keepdims=True) + EPS)
        gain = gpre_ref[...] * (1.0 + mod_ref[0, 4:5, :])
        hb = (x * inv * gain + mod_ref[0, 3:4, :]).astype(BF16)
        h_scr[...] = hb
        acc_scr[...] = hidden_block(hb)

    last = pl.num_programs(1) - 1

    @pl.when((f > 0) & (f < last))
    def _():
        acc_scr[...] += hidden_block(h_scr[...])

    @pl.when(f == last)
    def _():
        gain = mod_ref[0, 5:6, :] * gpost_ref[...]
        half = x_ref.shape[0] // 2
        for rows in (slice(0, half), slice(half, 2 * half)):
            y = acc_scr[rows, :] + hidden_block(h_scr[rows, :])
            inv = lax.rsqrt(jnp.mean(y * y, axis=-1, keepdims=True) + EPS)
            o_ref[rows, :] = x_ref[rows, :] + y * inv * gain


def _mlp(x1, mod, g_pre, g_post, w1, w2, rows_per_batch, tm=512, tf=1024):
    t, d = x1.shape
    dff = w1.shape[1]
    tiles_per_batch = rows_per_batch // tm
    return pl.pallas_call(
        _mlp_kernel,
        out_shape=jax.ShapeDtypeStruct((t, d), F32),
        grid=(t // tm, dff // tf),
        in_specs=[pl.BlockSpec((tm, d), lambda i, f: (i, 0)),
                  pl.BlockSpec((1, N_MOD, d), lambda i, f: (i // tiles_per_batch, 0, 0)),
                  pl.BlockSpec((1, d), lambda i, f: (0, 0)),
                  pl.BlockSpec((1, d), lambda i, f: (0, 0)),
                  pl.BlockSpec((d, tf), lambda i, f: (0, f)),
                  pl.BlockSpec((tf, d), lambda i, f: (f, 0))],
        out_specs=pl.BlockSpec((tm, d), lambda i, f: (i, 0)),
        scratch_shapes=[pltpu.VMEM((tm, d), BF16), pltpu.VMEM((tm, d), F32)],
        compiler_params=pltpu.CompilerParams(dimension_semantics=("parallel", "arbitrary"),
                                             vmem_limit_bytes=VMEM_LIMIT),
        name="mlp",
    )(x1, mod, g_pre, g_post, w1, w2)


def _regroup_kernel(w_ref, main_ref, small_ref):
    fox_end = 3 * FOX_HEADS * FOX_HEAD_DIM
    gla_start = fox_end + FOX_HEADS
    gla_end = gla_start + 2 * GLA_HEADS * GLA_DK + GLA_HEADS * GLA_DV
    gate_start = gla_end + GLA_GATE_RANK
    w = w_ref[...]
    main_ref[...] = jnp.concatenate(
        [w[:, :fox_end], w[:, gla_start:gla_end], w[:, gate_start:]], axis=1).astype(BF16)
    pad = jnp.zeros((w.shape[0], LANE - FOX_HEADS - GLA_GATE_RANK), w.dtype)
    small_ref[...] = jnp.concatenate(
        [w[:, fox_end:gla_start], w[:, gla_end:gate_start], pad], axis=1).astype(BF16)


def _regroup_in_proj_weight(w_in, rows=256):
    d, n_in = w_in.shape
    n_main = n_in - FOX_HEADS - GLA_GATE_RANK
    return pl.pallas_call(
        _regroup_kernel,
        out_shape=(jax.ShapeDtypeStruct((d, n_main), BF16), jax.ShapeDtypeStruct((d, LANE), BF16)),
        grid=(d // rows,),
        in_specs=[pl.BlockSpec((rows, n_in), lambda i: (i, 0))],
        out_specs=(pl.BlockSpec((rows, n_main), lambda i: (i, 0)),
                   pl.BlockSpec((rows, LANE), lambda i: (i, 0))),
        compiler_params=pltpu.CompilerParams(dimension_semantics=("parallel",),
                                             vmem_limit_bytes=VMEM_LIMIT),
        name="regroup_w_in",
    )(w_in)


def kernel(x, c, w_ada, b_ada, g_pre_mix, g_post_mix, w_in, b_fgate, w_gla_a2, b_gla_a2,
           g_fox_out, g_gla_out, w_out, g_pre_mlp, g_post_mlp, w_mlp_in, w_mlp_out):
    b, s, d = x.shape
    depth = w_ada.shape[0]
    row = lambda v: v.reshape(1, -1)
    for i in range(depth):
        c_pad = jnp.concatenate([c, jnp.zeros((8 - b, d), c.dtype)], axis=0)
        mod = _ada(c_pad, w_ada[i], row(b_ada[i]))[:b].reshape(b, N_MOD, d)

        w_main, w_small = _regroup_in_proj_weight(w_in[i])
        x2 = x.reshape(b * s, d)
        proj, small = _in_proj(x2, mod, row(g_pre_mix[i]), w_main, w_small, s)
        proj3 = proj.reshape(b, s, -1)
        small3 = small.reshape(b, s, LANE)

        fbias = jnp.concatenate([b_fgate[i], jnp.zeros((LANE - FOX_HEADS,), F32)]).reshape(1, LANE)
        fox = _fox_attn(proj3, _fox_cum(small3, fbias),
                        g_fox_out[i].reshape(FOX_HEADS, 1, FOX_HEAD_DIM))

        kw = GLA_HEADS * GLA_DK
        wa_pad = jnp.concatenate(
            [jnp.zeros((FOX_HEADS, kw), F32), w_gla_a2[i],
             jnp.zeros((LANE - FOX_HEADS - GLA_GATE_RANK, kw), F32)], axis=0).astype(BF16)
        gla = _gla(proj3, small3, wa_pad, row(b_gla_a2[i]),
                   g_gla_out[i].reshape(GLA_HEADS, 1, GLA_DV))

        x1 = _out_proj(fox.reshape(b * s, -1), gla.reshape(b * s, -1), w_out[i].astype(BF16),
                       x2, mod, row(g_post_mix[i]), s)
        x2 = _mlp(x1, mod, row(g_pre_mlp[i]), row(g_post_mlp[i]),
                  w_mlp_in[i].astype(BF16), w_mlp_out[i].astype(BF16), s)
        x = x2.reshape(b, s, d)
    return x
```

```python
import functools

import jax
import jax.numpy as jnp
from jax import lax
from jax.experimental import pallas as pl
from jax.experimental.pallas import tpu as pltpu

F32 = jnp.float32
BF16 = jnp.bfloat16

EPS = 1e-6
N_MOD = 6
FOX_HEADS = 8
FOX_HEAD_DIM = 128
GLA_HEADS = 4
GLA_DK = 128
GLA_DV = 256
GLA_GATE_RANK = 16
GLA_GATE_TEMP = 16.0
CHUNK = 64
LANE = 128
VMEM_LIMIT = 56 * 1024 * 1024

NEG_BIG = -1e30
LOG2E = 1.4426950408889634
FOX_HEADS_PER_STEP = 2


def _nt_dot(a, b):
    return lax.dot_general(a, b, (((1,), (1,)), ((), ())), preferred_element_type=F32)


def _tn_dot(a, b):
    return lax.dot_general(a, b, (((0,), (0,)), ((), ())), preferred_element_type=F32)


def _log_sigmoid(z):
    return jnp.minimum(z, 0.0) - jnp.log(1.0 + jnp.exp(-jnp.abs(z)))


def _sigmoid(z):
    return 1.0 / (1.0 + jnp.exp(-z))


def _tri_cumsum(tri, v):
    hi = v.astype(BF16)
    lo = (v - hi.astype(F32)).astype(BF16)
    return (jnp.dot(tri, hi, preferred_element_type=F32)
            + jnp.dot(tri, lo, preferred_element_type=F32))


def _lower_tri(n, value=1.0):
    r = lax.broadcasted_iota(jnp.int32, (n, n), 0)
    c = lax.broadcasted_iota(jnp.int32, (n, n), 1)
    return jnp.where(r >= c, value, 0.0).astype(BF16)


def _ada_kernel(c_ref, w_ref, b_ref, o_ref):
    c = c_ref[...]
    act = (c * _sigmoid(c)).astype(BF16)
    o_ref[...] = jnp.dot(act, w_ref[...].astype(BF16), preferred_element_type=F32) + b_ref[...]


def _ada(c_pad, w_ada, b_ada, tn=1024):
    m, d = c_pad.shape
    n = w_ada.shape[1]
    return pl.pallas_call(
        _ada_kernel,
        out_shape=jax.ShapeDtypeStruct((m, n), F32),
        grid=(n // tn,),
        in_specs=[pl.BlockSpec((m, d), lambda j: (0, 0)),
                  pl.BlockSpec((d, tn), lambda j: (0, j)),
                  pl.BlockSpec((1, tn), lambda j: (0, j))],
        out_specs=pl.BlockSpec((m, tn), lambda j: (0, j)),
        compiler_params=pltpu.CompilerParams(dimension_semantics=("parallel",),
                                             vmem_limit_bytes=VMEM_LIMIT),
        name="ada",
    )(c_pad, w_ada, b_ada)


def _inproj_kernel(x_ref, mod_ref, g_ref, w_ref, ws_ref, o_ref, os_ref, h_scr):
    j = pl.program_id(1)

    @pl.when(j == 0)
    def _():
        x = x_ref[...]
        inv = lax.rsqrt(jnp.mean(x * x, axis=-1, keepdims=True) + EPS)
        gain = g_ref[...] * (1.0 + mod_ref[0, 1:2, :])
        hb = (x * inv * gain + mod_ref[0, 0:1, :]).astype(BF16)
        h_scr[...] = hb
        os_ref[...] = _nt_dot(hb, ws_ref[...])
        acc = _nt_dot(hb, w_ref[...])
        qw = FOX_HEADS * FOX_HEAD_DIM
        o_ref[:, :qw] = (acc[:, :qw] * (FOX_HEAD_DIM ** -0.5 * LOG2E)).astype(BF16)
        o_ref[:, qw:] = acc[:, qw:].astype(BF16)

    @pl.when(j > 0)
    def _():
        o_ref[...] = _nt_dot(h_scr[...], w_ref[...]).astype(BF16)


def _in_proj(x2, mod, g, w_main, w_small, rows_per_batch, tm=1024, tn=2048):
    t, d = x2.shape
    n = w_main.shape[0]
    assert tn > FOX_HEADS * FOX_HEAD_DIM, "column block 0 must contain all FoX queries"
    tiles_per_batch = rows_per_batch // tm
    return pl.pallas_call(
        _inproj_kernel,
        out_shape=(jax.ShapeDtypeStruct((t, n), BF16),
                   jax.ShapeDtypeStruct((t, LANE), F32)),
        grid=(t // tm, n // tn),
        in_specs=[pl.BlockSpec((tm, d), lambda i, j: (i, 0)),
                  pl.BlockSpec((1, N_MOD, d), lambda i, j: (i // tiles_per_batch, 0, 0)),
                  pl.BlockSpec((1, d), lambda i, j: (0, 0)),
                  pl.BlockSpec((tn, d), lambda i, j: (j, 0)),
                  pl.BlockSpec((LANE, d), lambda i, j: (0, 0))],
        out_specs=(pl.BlockSpec((tm, tn), lambda i, j: (i, j)),
                   pl.BlockSpec((tm, LANE), lambda i, j: (i, 0))),
        scratch_shapes=[pltpu.VMEM((tm, d), BF16)],
        compiler_params=pltpu.CompilerParams(dimension_semantics=("parallel", "arbitrary"),
                                             vmem_limit_bytes=VMEM_LIMIT),
        name="in_proj",
    )(x2, mod, g, w_main, w_small)


def _cum_kernel(s_ref, b_ref, o_ref, *, blk):
    s = s_ref.shape[1]
    tri = _lower_tri(blk)
    col = lax.broadcasted_iota(jnp.int32, (blk, LANE), 1)
    carry = jnp.zeros((1, LANE), F32)
    for r in range(s // blk):
        rows = slice(r * blk, (r + 1) * blk)
        z = s_ref[0, rows, :] + b_ref[...]
        cs = _tri_cumsum(tri, _log_sigmoid(z)) + carry
        carry = cs[blk - 1:blk, :]
        c2 = jnp.where(col < FOX_HEADS, cs * LOG2E, 0.0)
        hi = c2.astype(BF16).astype(F32)
        rem = c2 - hi
        mid = rem.astype(BF16).astype(F32)
        lo = (rem - mid).astype(BF16).astype(F32)
        pieces = hi + pltpu.roll(mid, FOX_HEADS, 1) + pltpu.roll(lo, 2 * FOX_HEADS, 1)
        o_ref[0, rows, :] = pieces.astype(BF16)


def _fox_cum(small3, bias_row, blk=256):
    b, s, _ = small3.shape
    return pl.pallas_call(
        functools.partial(_cum_kernel, blk=blk),
        out_shape=jax.ShapeDtypeStruct((b, s, LANE), BF16),
        grid=(b,),
        in_specs=[pl.BlockSpec((1, s, LANE), lambda i: (i, 0, 0)),
                  pl.BlockSpec((1, LANE), lambda i: (0, 0))],
        out_specs=pl.BlockSpec((1, s, LANE), lambda i: (i, 0, 0)),
        compiler_params=pltpu.CompilerParams(dimension_semantics=("parallel",),
                                             vmem_limit_bytes=VMEM_LIMIT),
        name="fox_cum",
    )(small3, bias_row)


class _FoxHead:
    def __init__(self, ka, vt, sa, sb, m, l, acc):
        self.ka, self.vt, self.sa, self.sb, self.m, self.l, self.acc = ka, vt, sa, sb, m, l, acc


def _fox_kernel(q_ref, k_ref, v_ref, a_ref, g_ref, o_ref, *scratch, tq):
    heads = [_FoxHead(*scratch[n * 7:(n + 1) * 7]) for n in range(FOX_HEADS_PER_STEP)]
    i = pl.program_id(2)
    lanes = lambda n: slice(n * LANE, (n + 1) * LANE)

    @pl.when(i == 0)
    def _():
        for n, hd in enumerate(heads):
            hd.ka[:, 0:LANE] = k_ref[0, :, lanes(n)]
            hd.ka[:, LANE:] = a_ref[0]
            hd.vt[...] = v_ref[0, :, lanes(n)].T

    col = lax.broadcasted_iota(jnp.int32, (tq, LANE), 1)
    qas = []
    for n, hd in enumerate(heads):
        h = pl.program_id(1) * FOX_HEADS_PER_STEP + n
        pick = (col == h) | (col == h + FOX_HEADS) | (col == h + 2 * FOX_HEADS)
        qas.append(jnp.concatenate([q_ref[0, :, lanes(n)],
                                    jnp.where(pick, -1.0, 0.0).astype(BF16)], axis=1))
        hd.m[...] = jnp.full_like(hd.m, NEG_BIG)
        hd.l[...] = jnp.zeros_like(hd.l)
        hd.acc[...] = jnp.zeros_like(hd.acc)

    def logits_to(pick_buf, blk):
        keys = pl.ds(pl.multiple_of(blk * tq, tq), tq)
        for hd, qa in zip(heads, qas):
            pick_buf(hd)[...] = _nt_dot(hd.ka[keys, :], qa)

    def absorb(pick_buf, blk, masked):
        keys = pl.ds(pl.multiple_of(blk * tq, tq), tq)
        for hd in heads:
            st = pick_buf(hd)[...]
            if masked:
                kr = lax.broadcasted_iota(jnp.int32, (tq, tq), 0)
                qc = lax.broadcasted_iota(jnp.int32, (tq, tq), 1)
                st = jnp.where(kr <= qc, st, NEG_BIG)
            m = hd.m[...]
            m_new = jnp.maximum(m, jnp.max(st, axis=0, keepdims=True))
            alpha = jnp.exp2(m - m_new)
            p = jnp.exp2(st - m_new)
            hd.m[...] = m_new
            hd.l[...] = alpha * hd.l[...] + jnp.sum(p, axis=0, keepdims=True)
            pv = jnp.dot(hd.vt[:, keys], p.astype(BF16), preferred_element_type=F32)
            hd.acc[...] = alpha * hd.acc[...] + pv

    buf_a = lambda hd: hd.sa
    buf_b = lambda hd: hd.sb

    logits_to(buf_a, 0)

    def block_pair(t, carry):
        logits_to(buf_b, 2 * t + 1)
        absorb(buf_a, 2 * t, False)
        logits_to(buf_a, 2 * t + 2)
        absorb(buf_b, 2 * t + 1, False)
        return carry

    lax.fori_loop(0, i // 2, block_pair, 0)

    @pl.when(i % 2 == 0)
    def _():
        absorb(buf_a, i, True)

    @pl.when(i % 2 == 1)
    def _():
        logits_to(buf_b, i)
        absorb(buf_a, i - 1, False)
        absorb(buf_b, i, True)

    for n, hd in enumerate(heads):
        ot = hd.acc[...] / hd.l[...]
        ot = ot * lax.rsqrt(jnp.mean(ot * ot, axis=0, keepdims=True) + EPS)
        o_ref[0, :, lanes(n)] = (ot.T * g_ref[n]).astype(BF16)


def _fox_attn(proj3, pieces3, g_fox, tq=512):
    b, s, _ = proj3.shape
    nh = FOX_HEADS_PER_STEP
    groups = FOX_HEADS // nh
    width = nh * FOX_HEAD_DIM
    per_head = [pltpu.VMEM((s, 2 * LANE), BF16), pltpu.VMEM((FOX_HEAD_DIM, s), BF16),
                pltpu.VMEM((tq, tq), F32), pltpu.VMEM((tq, tq), F32),
                pltpu.VMEM((1, tq), F32), pltpu.VMEM((1, tq), F32),
                pltpu.VMEM((FOX_HEAD_DIM, tq), F32)]
    return pl.pallas_call(
        functools.partial(_fox_kernel, tq=tq),
        out_shape=jax.ShapeDtypeStruct((b, s, FOX_HEADS * FOX_HEAD_DIM), BF16),
        grid=(b, groups, s // tq),
        in_specs=[pl.BlockSpec((1, tq, width), lambda bi, gi, qi: (bi, qi, gi)),
                  pl.BlockSpec((1, s, width), lambda bi, gi, qi: (bi, 0, groups + gi)),
                  pl.BlockSpec((1, s, width), lambda bi, gi, qi: (bi, 0, 2 * groups + gi)),
                  pl.BlockSpec((1, s, LANE), lambda bi, gi, qi: (bi, 0, 0)),
                  pl.BlockSpec((nh, 1, LANE), lambda bi, gi, qi: (gi, 0, 0))],
        out_specs=pl.BlockSpec((1, tq, width), lambda bi, gi, qi: (bi, qi, gi)),
        scratch_shapes=per_head * nh,
        compiler_params=pltpu.CompilerParams(
            dimension_semantics=("parallel", "parallel", "arbitrary"),
            vmem_limit_bytes=VMEM_LIMIT),
        name="fox_attn",
    )(proj3, proj3, proj3, pieces3, g_fox)


def _gla_kernel(q_ref, k_ref, v_ref, r_ref, s_ref, wa_ref, ba_ref, gg_ref, o_ref, st_ref, *, ts):
    @pl.when(pl.program_id(2) == 0)
    def _():
        st_ref[...] = jnp.zeros_like(st_ref)

    pre = jnp.dot(s_ref[0].astype(BF16), wa_ref[...], preferred_element_type=F32) + ba_ref[...]
    la = _log_sigmoid(pre)
    tri = _lower_tri(CHUNK, 1.0 / GLA_GATE_TEMP)
    chunks = [slice(c * CHUNK, (c + 1) * CHUNK) for c in range(ts // CHUNK)]
    cums = [_tri_cumsum(tri, la[sl]) for sl in chunks]
    tots = [cum[CHUNK - 1:CHUNK, :] for cum in cums]
    ups = []
    for sl, cum, tot in zip(chunks, cums, tots):
        kd = (k_ref[0, sl, :].astype(F32) * jnp.exp(tot - cum)).astype(BF16)
        ups.append(_tn_dot(v_ref[0, sl, :], kd))
    st = st_ref[...]
    states = []
    for tot, up in zip(tots, ups):
        st = st * jnp.exp(tot) + up
        states.append(st.astype(BF16))
    st_ref[...] = st
    outs = [_nt_dot(q_ref[0, sl, :], sb) for sl, sb in zip(chunks, states)]
    o = jnp.concatenate(outs, axis=0)
    qs = GLA_DK ** -0.5
    inv = qs * lax.rsqrt(qs * qs * jnp.mean(o * o, axis=-1, keepdims=True) + EPS)
    o = o * inv * gg_ref[0]
    r = r_ref[0].astype(F32)
    o_ref[0] = (o * (r * _sigmoid(r))).astype(BF16)


def _gla(proj3, small3, wa_pad, ba_row, g_gla, ts=1024):
    b, s, _ = proj3.shape
    q_blk = 3 * FOX_HEADS
    k_blk = q_blk + GLA_HEADS
    v_blk = (k_blk + GLA_HEADS) // 2
    r_blk = v_blk + GLA_HEADS
    return pl.pallas_call(
        functools.partial(_gla_kernel, ts=ts),
        out_shape=jax.ShapeDtypeStruct((b, s, GLA_HEADS * GLA_DV), BF16),
        grid=(b, GLA_HEADS, s // ts),
        in_specs=[pl.BlockSpec((1, ts, GLA_DK), lambda bi, gi, ti: (bi, ti, q_blk + gi)),
                  pl.BlockSpec((1, ts, GLA_DK), lambda bi, gi, ti: (bi, ti, k_blk + gi)),
                  pl.BlockSpec((1, ts, GLA_DV), lambda bi, gi, ti: (bi, ti, v_blk + gi)),
                  pl.BlockSpec((1, ts, GLA_DV), lambda bi, gi, ti: (bi, ti, r_blk + gi)),
                  pl.BlockSpec((1, ts, LANE), lambda bi, gi, ti: (bi, ti, 0)),
                  pl.BlockSpec((LANE, GLA_DK), lambda bi, gi, ti: (0, gi)),
                  pl.BlockSpec((1, GLA_DK), lambda bi, gi, ti: (0, gi)),
                  pl.BlockSpec((1, 1, GLA_DV), lambda bi, gi, ti: (gi, 0, 0))],
        out_specs=pl.BlockSpec((1, ts, GLA_DV), lambda bi, gi, ti: (bi, ti, gi)),
        scratch_shapes=[pltpu.VMEM((GLA_DV, GLA_DK), F32)],
        compiler_params=pltpu.CompilerParams(
            dimension_semantics=("parallel", "parallel", "arbitrary"),
            vmem_limit_bytes=VMEM_LIMIT),
        name="gla",
    )(proj3, proj3, proj3, proj3, small3, wa_pad, ba_row, g_gla)


def _outproj_kernel(fox_ref, gla_ref, w_ref, x_ref, mod_ref, g_ref, o_ref):
    half = fox_ref.shape[1]
    y = (jnp.dot(fox_ref[...], w_ref[0:half, :], preferred_element_type=F32)
         + jnp.dot(gla_ref[...], w_ref[half:, :], preferred_element_type=F32))
    inv = lax.rsqrt(jnp.mean(y * y, axis=-1, keepdims=True) + EPS)
    o_ref[...] = x_ref[...] + y * inv * (mod_ref[0, 2:3, :] * g_ref[...])


def _out_proj(fox2, gla2, w_out, x2, mod, g, rows_per_batch, tm=512):
    t, d = x2.shape
    half = fox2.shape[1]
    tiles_per_batch = rows_per_batch // tm
    return pl.pallas_call(
        _outproj_kernel,
        out_shape=jax.ShapeDtypeStruct((t, d), F32),
        grid=(t // tm,),
        in_specs=[pl.BlockSpec((tm, half), lambda i: (i, 0)),
                  pl.BlockSpec((tm, half), lambda i: (i, 0)),
                  pl.BlockSpec((2 * half, d), lambda i: (0, 0)),
                  pl.BlockSpec((tm, d), lambda i: (i, 0)),
                  pl.BlockSpec((1, N_MOD, d), lambda i: (i // tiles_per_batch, 0, 0)),
                  pl.BlockSpec((1, d), lambda i: (0, 0))],
        out_specs=pl.BlockSpec((tm, d), lambda i: (i, 0)),
        compiler_params=pltpu.CompilerParams(dimension_semantics=("parallel",),
                                             vmem_limit_bytes=VMEM_LIMIT),
        name="out_proj",
    )(fox2, gla2, w_out, x2, mod, g)


def _mlp_kernel(x_ref, mod_ref, gpre_ref, gpost_ref, w1_ref, w2_ref, o_ref, h_scr, acc_scr):
    f = pl.program_id(1)

    def hidden_block(hb):
        u = jnp.maximum(jnp.dot(hb, w1_ref[...], preferred_element_type=F32), 0.0)
        return jnp.dot((u * u).astype(BF16), w2_ref[...], preferred_element_type=F32)

    @pl.when(f == 0)
    def _():
        x = x_ref[...]
        inv = lax.rsqrt(jnp.mean(x * x, axis=-1, keepdims=True) + EPS)
        gain = gpre_ref[...] * (1.0 + mod_ref[0, 4:5, :])
        hb = (x * inv * gain + mod_ref[0, 3:4, :]).astype(BF16)
        h_scr[...] = hb
        acc_scr[...] = hidden_block(hb)

    last = pl.num_programs(1) - 1

    @pl.when((f > 0) & (f < last))
    def _():
        acc_scr[...] += hidden_block(h_scr[...])

    @pl.when(f == last)
    def _():
        gain = mod_ref[0, 5:6, :] * gpost_ref[...]
        half = x_ref.shape[0] // 2
        for rows in (slice(0, half), slice(half, 2 * half)):
            y = acc_scr[rows, :] + hidden_block(h_scr[rows, :])
            inv = lax.rsqrt(jnp.mean(y * y, axis=-1, keepdims=True) + EPS)
            o_ref[rows, :] = x_ref[rows, :] + y * inv * gain


def _mlp(x1, mod, g_pre, g_post, w1, w2, rows_per_batch, tm=512, tf=1024):
    t, d = x1.shape
    dff = w1.shape[1]
    tiles_per_batch = rows_per_batch // tm
    return pl.pallas_call(
        _mlp_kernel,
        out_shape=jax.ShapeDtypeStruct((t, d), F32),
        grid=(t // tm, dff // tf),
        in_specs=[pl.BlockSpec((tm, d), lambda i, f: (i, 0)),
                  pl.BlockSpec((1, N_MOD, d), lambda i, f: (i // tiles_per_batch, 0, 0)),
                  pl.BlockSpec((1, d), lambda i, f: (0, 0)),
                  pl.BlockSpec((1, d), lambda i, f: (0, 0)),
                  pl.BlockSpec((d, tf), lambda i, f: (0, f)),
                  pl.BlockSpec((tf, d), lambda i, f: (f, 0))],
        out_specs=pl.BlockSpec((tm, d), lambda i, f: (i, 0)),
        scratch_shapes=[pltpu.VMEM((tm, d), BF16), pltpu.VMEM((tm, d), F32)],
        compiler_params=pltpu.CompilerParams(dimension_semantics=("parallel", "arbitrary"),
                                             vmem_limit_bytes=VMEM_LIMIT),
        name="mlp",
    )(x1, mod, g_pre, g_post, w1, w2)


def _regroup_in_proj_weight(w_in):
    fox_end = 3 * FOX_HEADS * FOX_HEAD_DIM
    gla_start = fox_end + FOX_HEADS
    gla_end = gla_start + 2 * GLA_HEADS * GLA_DK + GLA_HEADS * GLA_DV
    gate_start = gla_end + GLA_GATE_RANK
    wt = jnp.swapaxes(w_in, 0, 1)
    main = jnp.concatenate([wt[:fox_end], wt[gla_start:gla_end], wt[gate_start:]], axis=0)
    pad = jnp.zeros((LANE - FOX_HEADS - GLA_GATE_RANK, w_in.shape[0]), w_in.dtype)
    small = jnp.concatenate([wt[fox_end:gla_start], wt[gla_end:gate_start], pad], axis=0)
    return main.astype(BF16), small.astype(BF16)


def kernel(x, c, w_ada, b_ada, g_pre_mix, g_post_mix, w_in, b_fgate, w_gla_a2, b_gla_a2,
           g_fox_out, g_gla_out, w_out, g_pre_mlp, g_post_mlp, w_mlp_in, w_mlp_out):
    b, s, d = x.shape
    depth = w_ada.shape[0]
    row = lambda v: v.reshape(1, -1)
    for i in range(depth):
        c_pad = jnp.concatenate([c, jnp.zeros((8 - b, d), c.dtype)], axis=0)
        mod = _ada(c_pad, w_ada[i], row(b_ada[i]))[:b].reshape(b, N_MOD, d)

        w_main, w_small = _regroup_in_proj_weight(w_in[i])
        x2 = x.reshape(b * s, d)
        proj, small = _in_proj(x2, mod, row(g_pre_mix[i]), w_main, w_small, s)
        proj3 = proj.reshape(b, s, -1)
        small3 = small.reshape(b, s, LANE)

        fbias = jnp.concatenate([b_fgate[i], jnp.zeros((LANE - FOX_HEADS,), F32)]).reshape(1, LANE)
        fox = _fox_attn(proj3, _fox_cum(small3, fbias),
                        g_fox_out[i].reshape(FOX_HEADS, 1, FOX_HEAD_DIM))

        kw = GLA_HEADS * GLA_DK
        wa_pad = jnp.concatenate(
            [jnp.zeros((FOX_HEADS, kw), F32), w_gla_a2[i],
             jnp.zeros((LANE - FOX_HEADS - GLA_GATE_RANK, kw), F32)], axis=0).astype(BF16)
        gla = _gla(proj3, small3, wa_pad, row(b_gla_a2[i]),
                   g_gla_out[i].reshape(GLA_HEADS, 1, GLA_DV))

        x1 = _out_proj(fox.reshape(b * s, -1), gla.reshape(b * s, -1), w_out[i].astype(BF16),
                       x2, mod, row(g_post_mix[i]), s)
        x2 = _mlp(x1, mod, row(g_pre_mlp[i]), row(g_post_mlp[i]),
                  w_mlp_in[i].astype(BF16), w_mlp_out[i].astype(BF16), s)
        x = x2.reshape(b, s, d)
    return x
```

```python
import functools

import jax
import jax.numpy as jnp
from jax import lax
from jax.experimental import pallas as pl
from jax.experimental.pallas import tpu as pltpu

F32 = jnp.float32
BF16 = jnp.bfloat16

EPS = 1e-6
N_MOD = 6
FOX_HEADS = 8
FOX_HEAD_DIM = 128
GLA_HEADS = 4
GLA_DK = 128
GLA_DV = 256
GLA_GATE_RANK = 16
GLA_GATE_TEMP = 16.0
CHUNK = 64
LANE = 128
VMEM_LIMIT = 56 * 1024 * 1024

NEG_BIG = -1e30
LOG2E = 1.4426950408889634
FOX_HEADS_PER_STEP = 2


def _nt_dot(a, b):
    return lax.dot_general(a, b, (((1,), (1,)), ((), ())), preferred_element_type=F32)


def _tn_dot(a, b):
    return lax.dot_general(a, b, (((0,), (0,)), ((), ())), preferred_element_type=F32)


def _log_sigmoid(z):
    return jnp.minimum(z, 0.0) - jnp.log(1.0 + jnp.exp(-jnp.abs(z)))


def _sigmoid(z):
    return 1.0 / (1.0 + jnp.exp(-z))


def _tri_cumsum(tri, v):
    hi = v.astype(BF16)
    lo = (v - hi.astype(F32)).astype(BF16)
    return (jnp.dot(tri, hi, preferred_element_type=F32)
            + jnp.dot(tri, lo, preferred_element_type=F32))


def _lower_tri(n, value=1.0):
    r = lax.broadcasted_iota(jnp.int32, (n, n), 0)
    c = lax.broadcasted_iota(jnp.int32, (n, n), 1)
    return jnp.where(r >= c, value, 0.0).astype(BF16)


def _ada_kernel(c_ref, w_ref, b_ref, o_ref):
    c = c_ref[...]
    act = (c * _sigmoid(c)).astype(BF16)
    o_ref[...] = jnp.dot(act, w_ref[...].astype(BF16), preferred_element_type=F32) + b_ref[...]


def _ada(c_pad, w_ada, b_ada, tn=1024):
    m, d = c_pad.shape
    n = w_ada.shape[1]
    return pl.pallas_call(
        _ada_kernel,
        out_shape=jax.ShapeDtypeStruct((m, n), F32),
        grid=(n // tn,),
        in_specs=[pl.BlockSpec((m, d), lambda j: (0, 0)),
                  pl.BlockSpec((d, tn), lambda j: (0, j)),
                  pl.BlockSpec((1, tn), lambda j: (0, j))],
        out_specs=pl.BlockSpec((m, tn), lambda j: (0, j)),
        compiler_params=pltpu.CompilerParams(dimension_semantics=("parallel",),
                                             vmem_limit_bytes=VMEM_LIMIT),
        name="ada",
    )(c_pad, w_ada, b_ada)


def _inproj_kernel(x_ref, mod_ref, g_ref, wlo_ref, whi_ref, ws_ref, o_ref, os_ref, h_scr):
    j = pl.program_id(1)
    half = wlo_ref.shape[0]

    @pl.when(j == 0)
    def _():
        x = x_ref[...]
        inv = lax.rsqrt(jnp.mean(x * x, axis=-1, keepdims=True) + EPS)
        gain = g_ref[...] * (1.0 + mod_ref[0, 1:2, :])
        hb = (x * inv * gain + mod_ref[0, 0:1, :]).astype(BF16)
        h_scr[...] = hb
        os_ref[...] = _nt_dot(hb, ws_ref[...])
        o_ref[:, :half] = (_nt_dot(hb, wlo_ref[...]) * (FOX_HEAD_DIM ** -0.5 * LOG2E)).astype(BF16)
        o_ref[:, half:] = _nt_dot(hb, whi_ref[...]).astype(BF16)

    @pl.when(j > 0)
    def _():
        o_ref[:, :half] = _nt_dot(h_scr[...], wlo_ref[...]).astype(BF16)
        o_ref[:, half:] = _nt_dot(h_scr[...], whi_ref[...]).astype(BF16)


def _in_proj(x2, mod, g, wt, w_small, rows_per_batch, tm=1024):
    t, d = x2.shape
    half = FOX_HEADS * FOX_HEAD_DIM
    n = wt.shape[0] - FOX_HEADS - GLA_GATE_RANK
    tn = 2 * half
    assert n % tn == 0 and GLA_HEADS * (2 * GLA_DK + GLA_DV) == tn
    tiles_per_batch = rows_per_batch // tm
    lo_start = lambda j: pl.multiple_of(j * tn + jnp.where(j >= 2, FOX_HEADS, 0), FOX_HEADS)
    hi_start = lambda j: pl.multiple_of(j * tn + half + jnp.where(j >= 1, FOX_HEADS, 0)
                                        + jnp.where(j >= 2, GLA_GATE_RANK, 0), FOX_HEADS)
    return pl.pallas_call(
        _inproj_kernel,
        out_shape=(jax.ShapeDtypeStruct((t, n), BF16),
                   jax.ShapeDtypeStruct((t, LANE), F32)),
        grid=(t // tm, n // tn),
        in_specs=[pl.BlockSpec((tm, d), lambda i, j: (i, 0)),
                  pl.BlockSpec((1, N_MOD, d), lambda i, j: (i // tiles_per_batch, 0, 0)),
                  pl.BlockSpec((1, d), lambda i, j: (0, 0)),
                  pl.BlockSpec((pl.Element(half), pl.Element(d)), lambda i, j: (lo_start(j), 0)),
                  pl.BlockSpec((pl.Element(half), pl.Element(d)), lambda i, j: (hi_start(j), 0)),
                  pl.BlockSpec((LANE, d), lambda i, j: (0, 0))],
        out_specs=(pl.BlockSpec((tm, tn), lambda i, j: (i, j)),
                   pl.BlockSpec((tm, LANE), lambda i, j: (i, 0))),
        scratch_shapes=[pltpu.VMEM((tm, d), BF16)],
        compiler_params=pltpu.CompilerParams(dimension_semantics=("parallel", "arbitrary"),
                                             vmem_limit_bytes=VMEM_LIMIT),
        name="in_proj",
    )(x2, mod, g, wt, wt, w_small)


def _cum_kernel(s_ref, b_ref, o_ref, *, blk):
    s = s_ref.shape[1]
    tri = _lower_tri(blk)
    col = lax.broadcasted_iota(jnp.int32, (blk, LANE), 1)
    carry = jnp.zeros((1, LANE), F32)
    for r in range(s // blk):
        rows = slice(r * blk, (r + 1) * blk)
        z = s_ref[0, rows, :] + b_ref[...]
        cs = _tri_cumsum(tri, _log_sigmoid(z)) + carry
        carry = cs[blk - 1:blk, :]
        c2 = jnp.where(col < FOX_HEADS, cs * LOG2E, 0.0)
        hi = c2.astype(BF16).astype(F32)
        rem = c2 - hi
        mid = rem.astype(BF16).astype(F32)
        lo = (rem - mid).astype(BF16).astype(F32)
        pieces = hi + pltpu.roll(mid, FOX_HEADS, 1) + pltpu.roll(lo, 2 * FOX_HEADS, 1)
        o_ref[0, rows, :] = pieces.astype(BF16)


def _fox_cum(small3, bias_row, blk=256):
    b, s, _ = small3.shape
    return pl.pallas_call(
        functools.partial(_cum_kernel, blk=blk),
        out_shape=jax.ShapeDtypeStruct((b, s, LANE), BF16),
        grid=(b,),
        in_specs=[pl.BlockSpec((1, s, LANE), lambda i: (i, 0, 0)),
                  pl.BlockSpec((1, LANE), lambda i: (0, 0))],
        out_specs=pl.BlockSpec((1, s, LANE), lambda i: (i, 0, 0)),
        compiler_params=pltpu.CompilerParams(dimension_semantics=("parallel",),
                                             vmem_limit_bytes=VMEM_LIMIT),
        name="fox_cum",
    )(small3, bias_row)


class _FoxHead:
    def __init__(self, ka, vt, sa, sb, m, l, acc):
        self.ka, self.vt, self.sa, self.sb, self.m, self.l, self.acc = ka, vt, sa, sb, m, l, acc


def _fox_kernel(q_ref, k_ref, v_ref, a_ref, g_ref, o_ref, *scratch, tq):
    heads = [_FoxHead(*scratch[n * 7:(n + 1) * 7]) for n in range(FOX_HEADS_PER_STEP)]
    i = pl.program_id(2)
    lanes = lambda n: slice(n * LANE, (n + 1) * LANE)

    @pl.when(i == 0)
    def _():
        for n, hd in enumerate(heads):
            hd.ka[:, 0:LANE] = k_ref[0, :, lanes(n)]
            hd.ka[:, LANE:] = a_ref[0]
            hd.vt[...] = v_ref[0, :, lanes(n)].T

    col = lax.broadcasted_iota(jnp.int32, (tq, LANE), 1)
    qas = []
    for n, hd in enumerate(heads):
        h = pl.program_id(1) * FOX_HEADS_PER_STEP + n
        pick = (col == h) | (col == h + FOX_HEADS) | (col == h + 2 * FOX_HEADS)
        qas.append(jnp.concatenate([q_ref[0, :, lanes(n)],
                                    jnp.where(pick, -1.0, 0.0).astype(BF16)], axis=1))
        hd.m[...] = jnp.full_like(hd.m, NEG_BIG)
        hd.l[...] = jnp.zeros_like(hd.l)
        hd.acc[...] = jnp.zeros_like(hd.acc)

    def logits_to(pick_buf, blk):
        keys = pl.ds(pl.multiple_of(blk * tq, tq), tq)
        for hd, qa in zip(heads, qas):
            pick_buf(hd)[...] = _nt_dot(hd.ka[keys, :], qa)

    def absorb(pick_buf, blk, masked):
        keys = pl.ds(pl.multiple_of(blk * tq, tq), tq)
        for hd in heads:
            st = pick_buf(hd)[...]
            if masked:
                kr = lax.broadcasted_iota(jnp.int32, (tq, tq), 0)
                qc = lax.broadcasted_iota(jnp.int32, (tq, tq), 1)
                st = jnp.where(kr <= qc, st, NEG_BIG)
            m = hd.m[...]
            m_new = jnp.maximum(m, jnp.max(st, axis=0, keepdims=True))
            alpha = jnp.exp2(m - m_new)
            p = jnp.exp2(st - m_new)
            hd.m[...] = m_new
            hd.l[...] = alpha * hd.l[...] + jnp.sum(p, axis=0, keepdims=True)
            pv = jnp.dot(hd.vt[:, keys], p.astype(BF16), preferred_element_type=F32)
            hd.acc[...] = alpha * hd.acc[...] + pv

    buf_a = lambda hd: hd.sa
    buf_b = lambda hd: hd.sb

    logits_to(buf_a, 0)

    def block_pair(t, carry):
        logits_to(buf_b, 2 * t + 1)
        absorb(buf_a, 2 * t, False)
        logits_to(buf_a, 2 * t + 2)
        absorb(buf_b, 2 * t + 1, False)
        return carry

    lax.fori_loop(0, i // 2, block_pair, 0)

    @pl.when(i % 2 == 0)
    def _():
        absorb(buf_a, i, True)

    @pl.when(i % 2 == 1)
    def _():
        logits_to(buf_b, i)
        absorb(buf_a, i - 1, False)
        absorb(buf_b, i, True)

    for n, hd in enumerate(heads):
        ot = hd.acc[...] / hd.l[...]
        ot = ot * lax.rsqrt(jnp.mean(ot * ot, axis=0, keepdims=True) + EPS)
        o_ref[0, :, lanes(n)] = (ot.T * g_ref[n]).astype(BF16)


def _fox_attn(proj3, pieces3, g_fox, tq=512):
    b, s, _ = proj3.shape
    nh = FOX_HEADS_PER_STEP
    groups = FOX_HEADS // nh
    width = nh * FOX_HEAD_DIM
    per_head = [pltpu.VMEM((s, 2 * LANE), BF16), pltpu.VMEM((FOX_HEAD_DIM, s), BF16),
                pltpu.VMEM((tq, tq), F32), pltpu.VMEM((tq, tq), F32),
                pltpu.VMEM((1, tq), F32), pltpu.VMEM((1, tq), F32),
                pltpu.VMEM((FOX_HEAD_DIM, tq), F32)]
    return pl.pallas_call(
        functools.partial(_fox_kernel, tq=tq),
        out_shape=jax.ShapeDtypeStruct((b, s, FOX_HEADS * FOX_HEAD_DIM), BF16),
        grid=(b, groups, s // tq),
        in_specs=[pl.BlockSpec((1, tq, width), lambda bi, gi, qi: (bi, qi, gi)),
                  pl.BlockSpec((1, s, width), lambda bi, gi, qi: (bi, 0, groups + gi)),
                  pl.BlockSpec((1, s, width), lambda bi, gi, qi: (bi, 0, 2 * groups + gi)),
                  pl.BlockSpec((1, s, LANE), lambda bi, gi, qi: (bi, 0, 0)),
                  pl.BlockSpec((nh, 1, LANE), lambda bi, gi, qi: (gi, 0, 0))],
        out_specs=pl.BlockSpec((1, tq, width), lambda bi, gi, qi: (bi, qi, gi)),
        scratch_shapes=per_head * nh,
        compiler_params=pltpu.CompilerParams(
            dimension_semantics=("parallel", "parallel", "arbitrary"),
            vmem_limit_bytes=VMEM_LIMIT),
        name="fox_attn",
    )(proj3, proj3, proj3, pieces3, g_fox)


def _gla_kernel(q_ref, k_ref, v_ref, r_ref, s_ref, wa_ref, ba_ref, gg_ref, o_ref, st_ref, *, ts):
    @pl.when(pl.program_id(2) == 0)
    def _():
        st_ref[...] = jnp.zeros_like(st_ref)

    pre = jnp.dot(s_ref[0].astype(BF16), wa_ref[...], preferred_element_type=F32) + ba_ref[...]
    la = _log_sigmoid(pre)
    tri = _lower_tri(CHUNK, 1.0 / GLA_GATE_TEMP)
    chunks = [slice(c * CHUNK, (c + 1) * CHUNK) for c in range(ts // CHUNK)]
    cums = [_tri_cumsum(tri, la[sl]) for sl in chunks]
    tots = [cum[CHUNK - 1:CHUNK, :] for cum in cums]
    ups = []
    for sl, cum, tot in zip(chunks, cums, tots):
        kd = (k_ref[0, sl, :].astype(F32) * jnp.exp(tot - cum)).astype(BF16)
        ups.append(_tn_dot(v_ref[0, sl, :], kd))
    st = st_ref[...]
    states = []
    for tot, up in zip(tots, ups):
        st = st * jnp.exp(tot) + up
        states.append(st.astype(BF16))
    st_ref[...] = st
    outs = [_nt_dot(q_ref[0, sl, :], sb) for sl, sb in zip(chunks, states)]
    o = jnp.concatenate(outs, axis=0)
    qs = GLA_DK ** -0.5
    inv = qs * lax.rsqrt(qs * qs * jnp.mean(o * o, axis=-1, keepdims=True) + EPS)
    o = o * inv * gg_ref[0]
    r = r_ref[0].astype(F32)
    o_ref[0] = (o * (r * _sigmoid(r))).astype(BF16)


def _gla(proj3, small3, wa_pad, ba_row, g_gla, ts=1024):
    b, s, _ = proj3.shape
    q_blk = 3 * FOX_HEADS
    k_blk = q_blk + GLA_HEADS
    v_blk = (k_blk + GLA_HEADS) // 2
    r_blk = v_blk + GLA_HEADS
    return pl.pallas_call(
        functools.partial(_gla_kernel, ts=ts),
        out_shape=jax.ShapeDtypeStruct((b, s, GLA_HEADS * GLA_DV), BF16),
        grid=(b, GLA_HEADS, s // ts),
        in_specs=[pl.BlockSpec((1, ts, GLA_DK), lambda bi, gi, ti: (bi, ti, q_blk + gi)),
                  pl.BlockSpec((1, ts, GLA_DK), lambda bi, gi, ti: (bi, ti, k_blk + gi)),
                  pl.BlockSpec((1, ts, GLA_DV), lambda bi, gi, ti: (bi, ti, v_blk + gi)),
                  pl.BlockSpec((1, ts, GLA_DV), lambda bi, gi, ti: (bi, ti, r_blk + gi)),
                  pl.BlockSpec((1, ts, LANE), lambda bi, gi, ti: (bi, ti, 0)),
                  pl.BlockSpec((LANE, GLA_DK), lambda bi, gi, ti: (0, gi)),
                  pl.BlockSpec((1, GLA_DK), lambda bi, gi, ti: (0, gi)),
                  pl.BlockSpec((1, 1, GLA_DV), lambda bi, gi, ti: (gi, 0, 0))],
        out_specs=pl.BlockSpec((1, ts, GLA_DV), lambda bi, gi, ti: (bi, ti, gi)),
        scratch_shapes=[pltpu.VMEM((GLA_DV, GLA_DK), F32)],
        compiler_params=pltpu.CompilerParams(
            dimension_semantics=("parallel", "parallel", "arbitrary"),
            vmem_limit_bytes=VMEM_LIMIT),
        name="gla",
    )(proj3, proj3, proj3, proj3, small3, wa_pad, ba_row, g_gla)


def _outproj_kernel(fox_ref, gla_ref, w_ref, x_ref, mod_ref, g_ref, o_ref):
    half = fox_ref.shape[1]
    y = (jnp.dot(fox_ref[...], w_ref[0:half, :], preferred_element_type=F32)
         + jnp.dot(gla_ref[...], w_ref[half:, :], preferred_element_type=F32))
    inv = lax.rsqrt(jnp.mean(y * y, axis=-1, keepdims=True) + EPS)
    o_ref[...] = x_ref[...] + y * inv * (mod_ref[0, 2:3, :] * g_ref[...])


def _out_proj(fox2, gla2, w_out, x2, mod, g, rows_per_batch, tm=512):
    t, d = x2.shape
    half = fox2.shape[1]
    tiles_per_batch = rows_per_batch // tm
    return pl.pallas_call(
        _outproj_kernel,
        out_shape=jax.ShapeDtypeStruct((t, d), F32),
        grid=(t // tm,),
        in_specs=[pl.BlockSpec((tm, half), lambda i: (i, 0)),
                  pl.BlockSpec((tm, half), lambda i: (i, 0)),
                  pl.BlockSpec((2 * half, d), lambda i: (0, 0)),
                  pl.BlockSpec((tm, d), lambda i: (i, 0)),
                  pl.BlockSpec((1, N_MOD, d), lambda i: (i // tiles_per_batch, 0, 0)),
                  pl.BlockSpec((1, d), lambda i: (0, 0))],
        out_specs=pl.BlockSpec((tm, d), lambda i: (i, 0)),
        compiler_params=pltpu.CompilerParams(dimension_semantics=("parallel",),
                                             vmem_limit_bytes=VMEM_LIMIT),
        name="out_proj",
    )(fox2, gla2, w_out, x2, mod, g)


def _mlp_kernel(x_ref, mod_ref, gpre_ref, gpost_ref, w1_ref, w2_ref, o_ref, h_scr, acc_scr):
    f = pl.program_id(1)

    def hidden_block(hb):
        u = jnp.maximum(jnp.dot(hb, w1_ref[...], preferred_element_type=F32), 0.0)
        return jnp.dot((u * u).astype(BF16), w2_ref[...], preferred_element_type=F32)

    @pl.when(f == 0)
    def _():
        x = x_ref[...]
        inv = lax.rsqrt(jnp.mean(x * x, axis=-1, keepdims=True) + EPS)
        gain = gpre_ref[...] * (1.0 + mod_ref[0, 4:5, :])
        hb = (x * inv * gain + mod_ref[0, 3:4, :]).astype(BF16)
        h_scr[...] = hb
        acc_scr[...] = hidden_block(hb)

    last = pl.num_programs(1) - 1

    @pl.when((f > 0) & (f < last))
    def _():
        acc_scr[...] += hidden_block(h_scr[...])

    @pl.when(f == last)
    def _():
        gain = mod_ref[0, 5:6, :] * gpost_ref[...]
        half = x_ref.shape[0] // 2
        for rows in (slice(0, half), slice(half, 2 * half)):
            y = acc_scr[rows, :] + hidden_block(h_scr[rows, :])
            inv = lax.rsqrt(jnp.mean(y * y, axis=-1, keepdims=True) + EPS)
            o_ref[rows, :] = x_ref[rows, :] + y * inv * gain


def _mlp(x1, mod, g_pre, g_post, w1, w2, rows_per_batch, tm=512, tf=1024):
    t, d = x1.shape
    dff = w1.shape[1]
    tiles_per_batch = rows_per_batch // tm
    return pl.pallas_call(
        _mlp_kernel,
        out_shape=jax.ShapeDtypeStruct((t, d), F32),
        grid=(t // tm, dff // tf),
        in_specs=[pl.BlockSpec((tm, d), lambda i, f: (i, 0)),
                  pl.BlockSpec((1, N_MOD, d), lambda i, f: (i // tiles_per_batch, 0, 0)),
                  pl.BlockSpec((1, d), lambda i, f: (0, 0)),
                  pl.BlockSpec((1, d), lambda i, f: (0, 0)),
                  pl.BlockSpec((d, tf), lambda i, f: (0, f)),
                  pl.BlockSpec((tf, d), lambda i, f: (f, 0))],
        out_specs=pl.BlockSpec((tm, d), lambda i, f: (i, 0)),
        scratch_shapes=[pltpu.VMEM((tm, d), BF16), pltpu.VMEM((tm, d), F32)],
        compiler_params=pltpu.CompilerParams(dimension_semantics=("parallel", "arbitrary"),
                                             vmem_limit_bytes=VMEM_LIMIT),
        name="mlp",
    )(x1, mod, g_pre, g_post, w1, w2)


def _transposed_in_proj_weight(w_in):
    fox_end = 3 * FOX_HEADS * FOX_HEAD_DIM
    gla_start = fox_end + FOX_HEADS
    gla_end = gla_start + 2 * GLA_HEADS * GLA_DK + GLA_HEADS * GLA_DV
    gate_start = gla_end + GLA_GATE_RANK
    wt = jnp.swapaxes(w_in, 0, 1).astype(BF16)
    pad = jnp.zeros((LANE - FOX_HEADS - GLA_GATE_RANK, w_in.shape[0]), BF16)
    small = jnp.concatenate([wt[fox_end:gla_start], wt[gla_end:gate_start], pad], axis=0)
    return wt, small


def kernel(x, c, w_ada, b_ada, g_pre_mix, g_post_mix, w_in, b_fgate, w_gla_a2, b_gla_a2,
           g_fox_out, g_gla_out, w_out, g_pre_mlp, g_post_mlp, w_mlp_in, w_mlp_out):
    b, s, d = x.shape
    depth = w_ada.shape[0]
    row = lambda v: v.reshape(1, -1)
    for i in range(depth):
        c_pad = jnp.concatenate([c, jnp.zeros((8 - b, d), c.dtype)], axis=0)
        mod = _ada(c_pad, w_ada[i], row(b_ada[i]))[:b].reshape(b, N_MOD, d)

        wt, w_small = _transposed_in_proj_weight(w_in[i])
        x2 = x.reshape(b * s, d)
        proj, small = _in_proj(x2, mod, row(g_pre_mix[i]), wt, w_small, s)
        proj3 = proj.reshape(b, s, -1)
        small3 = small.reshape(b, s, LANE)

        fbias = jnp.concatenate([b_fgate[i], jnp.zeros((LANE - FOX_HEADS,), F32)]).reshape(1, LANE)
        fox = _fox_attn(proj3, _fox_cum(small3, fbias),
                        g_fox_out[i].reshape(FOX_HEADS, 1, FOX_HEAD_DIM))

        kw = GLA_HEADS * GLA_DK
        wa_pad = jnp.concatenate(
            [jnp.zeros((FOX_HEADS, kw), F32), w_gla_a2[i],
             jnp.zeros((LANE - FOX_HEADS - GLA_GATE_RANK, kw), F32)], axis=0).astype(BF16)
        gla = _gla(proj3, small3, wa_pad, row(b_gla_a2[i]),
                   g_gla_out[i].reshape(GLA_HEADS, 1, GLA_DV))

        x1 = _out_proj(fox.reshape(b * s, -1), gla.reshape(b * s, -1), w_out[i].astype(BF16),
                       x2, mod, row(g_post_mix[i]), s)
        x2 = _mlp(x1, mod, row(g_pre_mlp[i]), row(g_post_mlp[i]),
                  w_mlp_in[i].astype(BF16), w_mlp_out[i].astype(BF16), s)
        x = x2.reshape(b, s, d)
    return x
```

```python
import functools

import jax
import jax.numpy as jnp
from jax import lax
from jax.experimental import pallas as pl
from jax.experimental.pallas import tpu as pltpu

F32 = jnp.float32
BF16 = jnp.bfloat16

EPS = 1e-6
N_MOD = 6
FOX_HEADS = 8
FOX_HEAD_DIM = 128
GLA_HEADS = 4
GLA_DK = 128
GLA_DV = 256
GLA_GATE_RANK = 16
GLA_GATE_TEMP = 16.0
CHUNK = 64
LANE = 128
VMEM_LIMIT = 56 * 1024 * 1024

NEG_BIG = -1e30
LOG2E = 1.4426950408889634
FOX_HEADS_PER_STEP = 4


def _nt_dot(a, b):
    return lax.dot_general(a, b, (((1,), (1,)), ((), ())), preferred_element_type=F32)


def _tn_dot(a, b):
    return lax.dot_general(a, b, (((0,), (0,)), ((), ())), preferred_element_type=F32)


def _log_sigmoid(z):
    return jnp.minimum(z, 0.0) - jnp.log(1.0 + jnp.exp(-jnp.abs(z)))


def _sigmoid(z):
    return 1.0 / (1.0 + jnp.exp(-z))


def _tri_cumsum(tri, v):
    hi = v.astype(BF16)
    lo = (v - hi.astype(F32)).astype(BF16)
    return (jnp.dot(tri, hi, preferred_element_type=F32)
            + jnp.dot(tri, lo, preferred_element_type=F32))


def _lower_tri(n, value=1.0):
    r = lax.broadcasted_iota(jnp.int32, (n, n), 0)
    c = lax.broadcasted_iota(jnp.int32, (n, n), 1)
    return jnp.where(r >= c, value, 0.0).astype(BF16)


def _ada_kernel(c_ref, w_ref, b_ref, o_ref):
    c = c_ref[...]
    act = (c * _sigmoid(c)).astype(BF16)
    o_ref[...] = jnp.dot(act, w_ref[...].astype(BF16), preferred_element_type=F32) + b_ref[...]


def _ada(c_pad, w_ada, b_ada, tn=1024):
    m, d = c_pad.shape
    n = w_ada.shape[1]
    return pl.pallas_call(
        _ada_kernel,
        out_shape=jax.ShapeDtypeStruct((m, n), F32),
        grid=(n // tn,),
        in_specs=[pl.BlockSpec((m, d), lambda j: (0, 0)),
                  pl.BlockSpec((d, tn), lambda j: (0, j)),
                  pl.BlockSpec((1, tn), lambda j: (0, j))],
        out_specs=pl.BlockSpec((m, tn), lambda j: (0, j)),
        compiler_params=pltpu.CompilerParams(dimension_semantics=("parallel",),
                                             vmem_limit_bytes=VMEM_LIMIT),
        name="ada",
    )(c_pad, w_ada, b_ada)


def _inproj_kernel(x_ref, mod_ref, g_ref, wlo_ref, whi_ref, ws_ref, o_ref, os_ref, h_scr):
    j = pl.program_id(1)
    half = wlo_ref.shape[0]

    @pl.when(j == 0)
    def _():
        x = x_ref[...]
        inv = lax.rsqrt(jnp.mean(x * x, axis=-1, keepdims=True) + EPS)
        gain = g_ref[...] * (1.0 + mod_ref[0, 1:2, :])
        hb = (x * inv * gain + mod_ref[0, 0:1, :]).astype(BF16)
        h_scr[...] = hb
        os_ref[...] = _nt_dot(hb, ws_ref[...])
        o_ref[:, :half] = (_nt_dot(hb, wlo_ref[...]) * (FOX_HEAD_DIM ** -0.5 * LOG2E)).astype(BF16)
        o_ref[:, half:] = _nt_dot(hb, whi_ref[...]).astype(BF16)

    @pl.when(j > 0)
    def _():
        o_ref[:, :half] = _nt_dot(h_scr[...], wlo_ref[...]).astype(BF16)
        o_ref[:, half:] = _nt_dot(h_scr[...], whi_ref[...]).astype(BF16)


def _in_proj(x2, mod, g, wt, w_small, rows_per_batch, tm=1024):
    t, d = x2.shape
    half = FOX_HEADS * FOX_HEAD_DIM
    n = wt.shape[0] - FOX_HEADS - GLA_GATE_RANK
    tn = 2 * half
    assert n % tn == 0 and GLA_HEADS * (2 * GLA_DK + GLA_DV) == tn
    tiles_per_batch = rows_per_batch // tm
    lo_start = lambda j: pl.multiple_of(j * tn + jnp.where(j >= 2, FOX_HEADS, 0), FOX_HEADS)
    hi_start = lambda j: pl.multiple_of(j * tn + half + jnp.where(j >= 1, FOX_HEADS, 0)
                                        + jnp.where(j >= 2, GLA_GATE_RANK, 0), FOX_HEADS)
    return pl.pallas_call(
        _inproj_kernel,
        out_shape=(jax.ShapeDtypeStruct((t, n), BF16),
                   jax.ShapeDtypeStruct((t, LANE), F32)),
        grid=(t // tm, n // tn),
        in_specs=[pl.BlockSpec((tm, d), lambda i, j: (i, 0)),
                  pl.BlockSpec((1, N_MOD, d), lambda i, j: (i // tiles_per_batch, 0, 0)),
                  pl.BlockSpec((1, d), lambda i, j: (0, 0)),
                  pl.BlockSpec((pl.Element(half), pl.Element(d)), lambda i, j: (lo_start(j), 0)),
                  pl.BlockSpec((pl.Element(half), pl.Element(d)), lambda i, j: (hi_start(j), 0)),
                  pl.BlockSpec((LANE, d), lambda i, j: (0, 0))],
        out_specs=(pl.BlockSpec((tm, tn), lambda i, j: (i, j)),
                   pl.BlockSpec((tm, LANE), lambda i, j: (i, 0))),
        scratch_shapes=[pltpu.VMEM((tm, d), BF16)],
        compiler_params=pltpu.CompilerParams(dimension_semantics=("parallel", "arbitrary"),
                                             vmem_limit_bytes=VMEM_LIMIT),
        name="in_proj",
    )(x2, mod, g, wt, wt, w_small)


def _cum_kernel(s_ref, b_ref, o_ref, *, blk):
    s = s_ref.shape[1]
    tri = _lower_tri(blk)
    col = lax.broadcasted_iota(jnp.int32, (blk, LANE), 1)
    carry = jnp.zeros((1, LANE), F32)
    for r in range(s // blk):
        rows = slice(r * blk, (r + 1) * blk)
        z = s_ref[0, rows, :] + b_ref[...]
        cs = _tri_cumsum(tri, _log_sigmoid(z)) + carry
        carry = cs[blk - 1:blk, :]
        c2 = jnp.where(col < FOX_HEADS, cs * LOG2E, 0.0)
        hi = c2.astype(BF16).astype(F32)
        rem = c2 - hi
        mid = rem.astype(BF16).astype(F32)
        lo = (rem - mid).astype(BF16).astype(F32)
        pieces = hi + pltpu.roll(mid, FOX_HEADS, 1) + pltpu.roll(lo, 2 * FOX_HEADS, 1)
        o_ref[0, rows, :] = pieces.astype(BF16)


def _fox_cum(small3, bias_row, blk=256):
    b, s, _ = small3.shape
    return pl.pallas_call(
        functools.partial(_cum_kernel, blk=blk),
        out_shape=jax.ShapeDtypeStruct((b, s, LANE), BF16),
        grid=(b,),
        in_specs=[pl.BlockSpec((1, s, LANE), lambda i: (i, 0, 0)),
                  pl.BlockSpec((1, LANE), lambda i: (0, 0))],
        out_specs=pl.BlockSpec((1, s, LANE), lambda i: (i, 0, 0)),
        compiler_params=pltpu.CompilerParams(dimension_semantics=("parallel",),
                                             vmem_limit_bytes=VMEM_LIMIT),
        name="fox_cum",
    )(small3, bias_row)


class _FoxHead:
    def __init__(self, ka, vt, sa, sb, m, l, acc):
        self.ka, self.vt, self.sa, self.sb, self.m, self.l, self.acc = ka, vt, sa, sb, m, l, acc


def _fox_kernel(q_ref, k_ref, v_ref, a_ref, g_ref, o_ref, *scratch, tq):
    heads = [_FoxHead(*scratch[n * 7:(n + 1) * 7]) for n in range(FOX_HEADS_PER_STEP)]
    i = pl.program_id(2)
    lanes = lambda n: slice(n * LANE, (n + 1) * LANE)

    @pl.when(i == 0)
    def _():
        for n, hd in enumerate(heads):
            hd.ka[:, 0:LANE] = k_ref[0, :, lanes(n)]
            hd.ka[:, LANE:] = a_ref[0]
            hd.vt[...] = v_ref[0, :, lanes(n)].T

    col = lax.broadcasted_iota(jnp.int32, (tq, LANE), 1)
    qas = []
    for n, hd in enumerate(heads):
        h = pl.program_id(1) * FOX_HEADS_PER_STEP + n
        pick = (col == h) | (col == h + FOX_HEADS) | (col == h + 2 * FOX_HEADS)
        qas.append(jnp.concatenate([q_ref[0, :, lanes(n)],
                                    jnp.where(pick, -1.0, 0.0).astype(BF16)], axis=1))
        hd.m[...] = jnp.full_like(hd.m, NEG_BIG)
        hd.l[...] = jnp.zeros_like(hd.l)
        hd.acc[...] = jnp.zeros_like(hd.acc)

    def logits_to(pick_buf, blk):
        keys = pl.ds(pl.multiple_of(blk * tq, tq), tq)
        for hd, qa in zip(heads, qas):
            pick_buf(hd)[...] = _nt_dot(hd.ka[keys, :], qa)

    def absorb(pick_buf, blk, masked):
        keys = pl.ds(pl.multiple_of(blk * tq, tq), tq)
        for hd in heads:
            st = pick_buf(hd)[...]
            if masked:
                kr = lax.broadcasted_iota(jnp.int32, (tq, tq), 0)
                qc = lax.broadcasted_iota(jnp.int32, (tq, tq), 1)
                st = jnp.where(kr <= qc, st, NEG_BIG)
            m = hd.m[...]
            m_new = jnp.maximum(m, jnp.max(st, axis=0, keepdims=True))
            alpha = jnp.exp2(m - m_new)
            p = jnp.exp2(st - m_new)
            hd.m[...] = m_new
            hd.l[...] = alpha * hd.l[...] + jnp.sum(p, axis=0, keepdims=True)
            pv = jnp.dot(hd.vt[:, keys], p.astype(BF16), preferred_element_type=F32)
            hd.acc[...] = alpha * hd.acc[...] + pv

    buf_a = lambda hd: hd.sa
    buf_b = lambda hd: hd.sb

    logits_to(buf_a, 0)

    def block_pair(t, carry):
        logits_to(buf_b, 2 * t + 1)
        absorb(buf_a, 2 * t, False)
        logits_to(buf_a, 2 * t + 2)
        absorb(buf_b, 2 * t + 1, False)
        return carry

    lax.fori_loop(0, i // 2, block_pair, 0)

    @pl.when(i % 2 == 0)
    def _():
        absorb(buf_a, i, True)

    @pl.when(i % 2 == 1)
    def _():
        logits_to(buf_b, i)
        absorb(buf_a, i - 1, False)
        absorb(buf_b, i, True)

    for n, hd in enumerate(heads):
        ot = hd.acc[...] / hd.l[...]
        ot = ot * lax.rsqrt(jnp.mean(ot * ot, axis=0, keepdims=True) + EPS)
        o_ref[0, :, lanes(n)] = (ot.T * g_ref[n]).astype(BF16)


def _fox_attn(proj3, pieces3, g_fox, tq=512):
    b, s, _ = proj3.shape
    nh = FOX_HEADS_PER_STEP
    groups = FOX_HEADS // nh
    width = nh * FOX_HEAD_DIM
    per_head = [pltpu.VMEM((s, 2 * LANE), BF16), pltpu.VMEM((FOX_HEAD_DIM, s), BF16),
                pltpu.VMEM((tq, tq), F32), pltpu.VMEM((tq, tq), F32),
                pltpu.VMEM((1, tq), F32), pltpu.VMEM((1, tq), F32),
                pltpu.VMEM((FOX_HEAD_DIM, tq), F32)]
    return pl.pallas_call(
        functools.partial(_fox_kernel, tq=tq),
        out_shape=jax.ShapeDtypeStruct((b, s, FOX_HEADS * FOX_HEAD_DIM), BF16),
        grid=(b, groups, s // tq),
        in_specs=[pl.BlockSpec((1, tq, width), lambda bi, gi, qi: (bi, qi, gi)),
                  pl.BlockSpec((1, s, width), lambda bi, gi, qi: (bi, 0, groups + gi)),
                  pl.BlockSpec((1, s, width), lambda bi, gi, qi: (bi, 0, 2 * groups + gi)),
                  pl.BlockSpec((1, s, LANE), lambda bi, gi, qi: (bi, 0, 0)),
                  pl.BlockSpec((nh, 1, LANE), lambda bi, gi, qi: (gi, 0, 0))],
        out_specs=pl.BlockSpec((1, tq, width), lambda bi, gi, qi: (bi, qi, gi)),
        scratch_shapes=per_head * nh,
        compiler_params=pltpu.CompilerParams(
            dimension_semantics=("parallel", "parallel", "arbitrary"),
            vmem_limit_bytes=VMEM_LIMIT),
        name="fox_attn",
    )(proj3, proj3, proj3, pieces3, g_fox)


def _gla_kernel(q_ref, k_ref, v_ref, r_ref, s_ref, wa_ref, ba_ref, gg_ref, o_ref, st_ref, *, ts):
    @pl.when(pl.program_id(2) == 0)
    def _():
        st_ref[...] = jnp.zeros_like(st_ref)

    pre = jnp.dot(s_ref[0].astype(BF16), wa_ref[...], preferred_element_type=F32) + ba_ref[...]
    la = _log_sigmoid(pre)
    tri = _lower_tri(CHUNK, 1.0 / GLA_GATE_TEMP)
    chunks = [slice(c * CHUNK, (c + 1) * CHUNK) for c in range(ts // CHUNK)]
    cums = [_tri_cumsum(tri, la[sl]) for sl in chunks]
    tots = [cum[CHUNK - 1:CHUNK, :] for cum in cums]
    ups = []
    for sl, cum, tot in zip(chunks, cums, tots):
        kd = (k_ref[0, sl, :].astype(F32) * jnp.exp(tot - cum)).astype(BF16)
        ups.append(_tn_dot(v_ref[0, sl, :], kd))
    st = st_ref[...]
    states = []
    for tot, up in zip(tots, ups):
        st = st * jnp.exp(tot) + up
        states.append(st.astype(BF16))
    st_ref[...] = st
    outs = [_nt_dot(q_ref[0, sl, :], sb) for sl, sb in zip(chunks, states)]
    o = jnp.concatenate(outs, axis=0)
    qs = GLA_DK ** -0.5
    inv = qs * lax.rsqrt(qs * qs * jnp.mean(o * o, axis=-1, keepdims=True) + EPS)
    o = o * inv * gg_ref[0]
    r = r_ref[0].astype(F32)
    o_ref[0] = (o * (r * _sigmoid(r))).astype(BF16)


def _gla(proj3, small3, wa_pad, ba_row, g_gla, ts=1024):
    b, s, _ = proj3.shape
    q_blk = 3 * FOX_HEADS
    k_blk = q_blk + GLA_HEADS
    v_blk = (k_blk + GLA_HEADS) // 2
    r_blk = v_blk + GLA_HEADS
    return pl.pallas_call(
        functools.partial(_gla_kernel, ts=ts),
        out_shape=jax.ShapeDtypeStruct((b, s, GLA_HEADS * GLA_DV), BF16),
        grid=(b, GLA_HEADS, s // ts),
        in_specs=[pl.BlockSpec((1, ts, GLA_DK), lambda bi, gi, ti: (bi, ti, q_blk + gi)),
                  pl.BlockSpec((1, ts, GLA_DK), lambda bi, gi, ti: (bi, ti, k_blk + gi)),
                  pl.BlockSpec((1, ts, GLA_DV), lambda bi, gi, ti: (bi, ti, v_blk + gi)),
                  pl.BlockSpec((1, ts, GLA_DV), lambda bi, gi, ti: (bi, ti, r_blk + gi)),
                  pl.BlockSpec((1, ts, LANE), lambda bi, gi, ti: (bi, ti, 0)),
                  pl.BlockSpec((LANE, GLA_DK), lambda bi, gi, ti: (0, gi)),
                  pl.BlockSpec((1, GLA_DK), lambda bi, gi, ti: (0, gi)),
                  pl.BlockSpec((1, 1, GLA_DV), lambda bi, gi, ti: (gi, 0, 0))],
        out_specs=pl.BlockSpec((1, ts, GLA_DV), lambda bi, gi, ti: (bi, ti, gi)),
        scratch_shapes=[pltpu.VMEM((GLA_DV, GLA_DK), F32)],
        compiler_params=pltpu.CompilerParams(
            dimension_semantics=("parallel", "parallel", "arbitrary"),
            vmem_limit_bytes=VMEM_LIMIT),
        name="gla",
    )(proj3, proj3, proj3, proj3, small3, wa_pad, ba_row, g_gla)


def _outproj_kernel(fox_ref, gla_ref, w_ref, x_ref, mod_ref, g_ref, o_ref):
    half = fox_ref.shape[1]
    y = (jnp.dot(fox_ref[...], w_ref[0:half, :], preferred_element_type=F32)
         + jnp.dot(gla_ref[...], w_ref[half:, :], preferred_element_type=F32))
    inv = lax.rsqrt(jnp.mean(y * y, axis=-1, keepdims=True) + EPS)
    o_ref[...] = x_ref[...] + y * inv * (mod_ref[0, 2:3, :] * g_ref[...])


def _out_proj(fox2, gla2, w_out, x2, mod, g, rows_per_batch, tm=512):
    t, d = x2.shape
    half = fox2.shape[1]
    tiles_per_batch = rows_per_batch // tm
    return pl.pallas_call(
        _outproj_kernel,
        out_shape=jax.ShapeDtypeStruct((t, d), F32),
        grid=(t // tm,),
        in_specs=[pl.BlockSpec((tm, half), lambda i: (i, 0)),
                  pl.BlockSpec((tm, half), lambda i: (i, 0)),
                  pl.BlockSpec((2 * half, d), lambda i: (0, 0)),
                  pl.BlockSpec((tm, d), lambda i: (i, 0)),
                  pl.BlockSpec((1, N_MOD, d), lambda i: (i // tiles_per_batch, 0, 0)),
                  pl.BlockSpec((1, d), lambda i: (0, 0))],
        out_specs=pl.BlockSpec((tm, d), lambda i: (i, 0)),
        compiler_params=pltpu.CompilerParams(dimension_semantics=("parallel",),
                                             vmem_limit_bytes=VMEM_LIMIT),
        name="out_proj",
    )(fox2, gla2, w_out, x2, mod, g)


def _mlp_kernel(x_ref, mod_ref, gpre_ref, gpost_ref, w1_ref, w2_ref, o_ref, h_scr, acc_scr):
    f = pl.program_id(1)

    def hidden_block(hb):
        u = jnp.maximum(jnp.dot(hb, w1_ref[...], preferred_element_type=F32), 0.0)
        return jnp.dot((u * u).astype(BF16), w2_ref[...], preferred_element_type=F32)

    @pl.when(f == 0)
    def _():
        x = x_ref[...]
        inv = lax.rsqrt(jnp.mean(x * x, axis=-1, keepdims=True) + EPS)
        gain = gpre_ref[...] * (1.0 + mod_ref[0, 4:5, :])
        hb = (x * inv * gain + mod_ref[0, 3:4, :]).astype(BF16)
        h_scr[...] = hb
        acc_scr[...] = hidden_block(hb)

    last = pl.num_programs(1) - 1

    @pl.when((f > 0) & (f < last))
    def _():
        acc_scr[...] += hidden_block(h_scr[...])

    @pl.when(f == last)
    def _():
        gain = mod_ref[0, 5:6, :] * gpost_ref[...]
        half = x_ref.shape[0] // 2
        for rows in (slice(0, half), slice(half, 2 * half)):
            y = acc_scr[rows, :] + hidden_block(h_scr[rows, :])
            inv = lax.rsqrt(jnp.mean(y * y, axis=-1, keepdims=True) + EPS)
            o_ref[rows, :] = x_ref[rows, :] + y * inv * gain


def _mlp(x1, mod, g_pre, g_post, w1, w2, rows_per_batch, tm=512, tf=1024):
    t, d = x1.shape
    dff = w1.shape[1]
    tiles_per_batch = rows_per_batch // tm
    return pl.pallas_call(
        _mlp_kernel,
        out_shape=jax.ShapeDtypeStruct((t, d), F32),
        grid=(t // tm, dff // tf),
        in_specs=[pl.BlockSpec((tm, d), lambda i, f: (i, 0)),
                  pl.BlockSpec((1, N_MOD, d), lambda i, f: (i // tiles_per_batch, 0, 0)),
                  pl.BlockSpec((1, d), lambda i, f: (0, 0)),
                  pl.BlockSpec((1, d), lambda i, f: (0, 0)),
                  pl.BlockSpec((d, tf), lambda i, f: (0, f)),
                  pl.BlockSpec((tf, d), lambda i, f: (f, 0))],
        out_specs=pl.BlockSpec((tm, d), lambda i, f: (i, 0)),
        scratch_shapes=[pltpu.VMEM((tm, d), BF16), pltpu.VMEM((tm, d), F32)],
        compiler_params=pltpu.CompilerParams(dimension_semantics=("parallel", "arbitrary"),
                                             vmem_limit_bytes=VMEM_LIMIT),
        name="mlp",
    )(x1, mod, g_pre, g_post, w1, w2)


def _transposed_in_proj_weight(w_in):
    fox_end = 3 * FOX_HEADS * FOX_HEAD_DIM
    gla_start = fox_end + FOX_HEADS
    gla_end = gla_start + 2 * GLA_HEADS * GLA_DK + GLA_HEADS * GLA_DV
    gate_start = gla_end + GLA_GATE_RANK
    wt = jnp.swapaxes(w_in, 0, 1).astype(BF16)
    pad = jnp.zeros((LANE - FOX_HEADS - GLA_GATE_RANK, w_in.shape[0]), BF16)
    small = jnp.concatenate([wt[fox_end:gla_start], wt[gla_end:gate_start], pad], axis=0)
    return wt, small


def kernel(x, c, w_ada, b_ada, g_pre_mix, g_post_mix, w_in, b_fgate, w_gla_a2, b_gla_a2,
           g_fox_out, g_gla_out, w_out, g_pre_mlp, g_post_mlp, w_mlp_in, w_mlp_out):
    b, s, d = x.shape
    depth = w_ada.shape[0]
    row = lambda v: v.reshape(1, -1)
    for i in range(depth):
        c_pad = jnp.concatenate([c, jnp.zeros((8 - b, d), c.dtype)], axis=0)
        mod = _ada(c_pad, w_ada[i], row(b_ada[i]))[:b].reshape(b, N_MOD, d)

        wt, w_small = _transposed_in_proj_weight(w_in[i])
        x2 = x.reshape(b * s, d)
        proj, small = _in_proj(x2, mod, row(g_pre_mix[i]), wt, w_small, s)
        proj3 = proj.reshape(b, s, -1)
        small3 = small.reshape(b, s, LANE)

        fbias = jnp.concatenate([b_fgate[i], jnp.zeros((LANE - FOX_HEADS,), F32)]).reshape(1, LANE)
        fox = _fox_attn(proj3, _fox_cum(small3, fbias),
                        g_fox_out[i].reshape(FOX_HEADS, 1, FOX_HEAD_DIM))

        kw = GLA_HEADS * GLA_DK
        wa_pad = jnp.concatenate(
            [jnp.zeros((FOX_HEADS, kw), F32), w_gla_a2[i],
             jnp.zeros((LANE - FOX_HEADS - GLA_GATE_RANK, kw), F32)], axis=0).astype(BF16)
        gla = _gla(proj3, small3, wa_pad, row(b_gla_a2[i]),
                   g_gla_out[i].reshape(GLA_HEADS, 1, GLA_DV))

        x1 = _out_proj(fox.reshape(b * s, -1), gla.reshape(b * s, -1), w_out[i].astype(BF16),
                       x2, mod, row(g_post_mix[i]), s)
        x2 = _mlp(x1, mod, row(g_pre_mlp[i]), row(g_post_mlp[i]),
                  w_mlp_in[i].astype(BF16), w_mlp_out[i].astype(BF16), s)
        x = x2.reshape(b, s, d)
    return x
```

```python
import functools

import jax
import jax.numpy as jnp
from jax import lax
from jax.experimental import pallas as pl
from jax.experimental.pallas import tpu as pltpu

F32 = jnp.float32
BF16 = jnp.bfloat16

EPS = 1e-6
N_MOD = 6
FOX_HEADS = 8
FOX_HEAD_DIM = 128
GLA_HEADS = 4
GLA_DK = 128
GLA_DV = 256
GLA_GATE_RANK = 16
GLA_GATE_TEMP = 16.0
CHUNK = 64
LANE = 128
BF16_ROWS = 16
VMEM_LIMIT = 56 * 1024 * 1024

NEG_BIG = -1e30
LOG2E = 1.4426950408889634
FOX_HEADS_PER_STEP = 4


def _nt_dot(a, b):
    return lax.dot_general(a, b, (((1,), (1,)), ((), ())), preferred_element_type=F32)


def _tn_dot(a, b):
    return lax.dot_general(a, b, (((0,), (0,)), ((), ())), preferred_element_type=F32)


def _log_sigmoid(z):
    return jnp.minimum(z, 0.0) - jnp.log(1.0 + jnp.exp(-jnp.abs(z)))


def _sigmoid(z):
    return 1.0 / (1.0 + jnp.exp(-z))


def _tri_cumsum(tri, v):
    hi = v.astype(BF16)
    lo = (v - hi.astype(F32)).astype(BF16)
    return (jnp.dot(tri, hi, preferred_element_type=F32)
            + jnp.dot(tri, lo, preferred_element_type=F32))


def _lower_tri(n, value=1.0):
    r = lax.broadcasted_iota(jnp.int32, (n, n), 0)
    c = lax.broadcasted_iota(jnp.int32, (n, n), 1)
    return jnp.where(r >= c, value, 0.0).astype(BF16)


def _ada_kernel(c_ref, w_ref, b_ref, o_ref):
    c = c_ref[...]
    act = (c * _sigmoid(c)).astype(BF16)
    o_ref[...] = jnp.dot(act, w_ref[...].astype(BF16), preferred_element_type=F32) + b_ref[...]


def _ada(c_pad, w_ada, b_ada, tn=1024):
    m, d = c_pad.shape
    n = w_ada.shape[1]
    return pl.pallas_call(
        _ada_kernel,
        out_shape=jax.ShapeDtypeStruct((m, n), F32),
        grid=(n // tn,),
        in_specs=[pl.BlockSpec((m, d), lambda j: (0, 0)),
                  pl.BlockSpec((d, tn), lambda j: (0, j)),
                  pl.BlockSpec((1, tn), lambda j: (0, j))],
        out_specs=pl.BlockSpec((m, tn), lambda j: (0, j)),
        compiler_params=pltpu.CompilerParams(dimension_semantics=("parallel",),
                                             vmem_limit_bytes=VMEM_LIMIT),
        name="ada",
    )(c_pad, w_ada, b_ada)


def _inproj_kernel(x_ref, mod_ref, g_ref, wlo_ref, whi_ref, ws_ref, o_ref, os_ref, h_scr):
    j = pl.program_id(1)
    half = wlo_ref.shape[0]

    @pl.when(j == 0)
    def _():
        x = x_ref[...]
        inv = lax.rsqrt(jnp.mean(x * x, axis=-1, keepdims=True) + EPS)
        gain = g_ref[...] * (1.0 + mod_ref[0, 1:2, :])
        hb = (x * inv * gain + mod_ref[0, 0:1, :]).astype(BF16)
        h_scr[...] = hb
        os_ref[...] = _nt_dot(hb, ws_ref[...])
        o_ref[:, :half] = (_nt_dot(hb, wlo_ref[...]) * (FOX_HEAD_DIM ** -0.5 * LOG2E)).astype(BF16)
        o_ref[:, half:] = _nt_dot(hb, whi_ref[...]).astype(BF16)

    @pl.when(j > 0)
    def _():
        o_ref[:, :half] = _nt_dot(h_scr[...], wlo_ref[...]).astype(BF16)
        o_ref[:, half:] = _nt_dot(h_scr[...], whi_ref[...]).astype(BF16)


def _in_proj(x2, mod, g, wt, w_small, rows_per_batch, tm=1024):
    t, d = x2.shape
    half = FOX_HEADS * FOX_HEAD_DIM
    n = wt.shape[0] - FOX_HEADS - GLA_GATE_RANK
    tn = 2 * half
    assert n % tn == 0 and GLA_HEADS * (2 * GLA_DK + GLA_DV) == tn
    tiles_per_batch = rows_per_batch // tm
    lo_start = lambda j: pl.multiple_of(j * tn + jnp.where(j >= 2, FOX_HEADS, 0), FOX_HEADS)
    hi_start = lambda j: pl.multiple_of(j * tn + half + jnp.where(j >= 1, FOX_HEADS, 0)
                                        + jnp.where(j >= 2, GLA_GATE_RANK, 0), FOX_HEADS)
    return pl.pallas_call(
        _inproj_kernel,
        out_shape=(jax.ShapeDtypeStruct((t, n), BF16),
                   jax.ShapeDtypeStruct((t, LANE), F32)),
        grid=(t // tm, n // tn),
        in_specs=[pl.BlockSpec((tm, d), lambda i, j: (i, 0)),
                  pl.BlockSpec((1, N_MOD, d), lambda i, j: (i // tiles_per_batch, 0, 0)),
                  pl.BlockSpec((1, d), lambda i, j: (0, 0)),
                  pl.BlockSpec((pl.Element(half), pl.Element(d)), lambda i, j: (lo_start(j), 0)),
                  pl.BlockSpec((pl.Element(half), pl.Element(d)), lambda i, j: (hi_start(j), 0)),
                  pl.BlockSpec((LANE, d), lambda i, j: (0, 0))],
        out_specs=(pl.BlockSpec((tm, tn), lambda i, j: (i, j)),
                   pl.BlockSpec((tm, LANE), lambda i, j: (i, 0))),
        scratch_shapes=[pltpu.VMEM((tm, d), BF16)],
        compiler_params=pltpu.CompilerParams(dimension_semantics=("parallel", "arbitrary"),
                                             vmem_limit_bytes=VMEM_LIMIT),
        name="in_proj",
    )(x2, mod, g, wt, wt, w_small)


def _cum_kernel(s_ref, b_ref, o_ref, *, blk):
    s = s_ref.shape[1]
    tri = _lower_tri(blk)
    col = lax.broadcasted_iota(jnp.int32, (blk, LANE), 1)
    carry = jnp.zeros((1, LANE), F32)
    for r in range(s // blk):
        rows = slice(r * blk, (r + 1) * blk)
        z = s_ref[0, rows, :] + b_ref[...]
        cs = _tri_cumsum(tri, _log_sigmoid(z)) + carry
        carry = cs[blk - 1:blk, :]
        c2 = jnp.where(col < FOX_HEADS, cs * LOG2E, 0.0)
        hi = c2.astype(BF16).astype(F32)
        rem = c2 - hi
        mid = rem.astype(BF16).astype(F32)
        lo = (rem - mid).astype(BF16).astype(F32)
        pieces = hi + pltpu.roll(mid, FOX_HEADS, 1) + pltpu.roll(lo, 2 * FOX_HEADS, 1)
        o_ref[0, rows, :] = pieces.astype(BF16)


def _fox_cum(small3, bias_row, blk=256):
    b, s, _ = small3.shape
    return pl.pallas_call(
        functools.partial(_cum_kernel, blk=blk),
        out_shape=jax.ShapeDtypeStruct((b, s, LANE), BF16),
        grid=(b,),
        in_specs=[pl.BlockSpec((1, s, LANE), lambda i: (i, 0, 0)),
                  pl.BlockSpec((1, LANE), lambda i: (0, 0))],
        out_specs=pl.BlockSpec((1, s, LANE), lambda i: (i, 0, 0)),
        compiler_params=pltpu.CompilerParams(dimension_semantics=("parallel",),
                                             vmem_limit_bytes=VMEM_LIMIT),
        name="fox_cum",
    )(small3, bias_row)


class _FoxHead:
    def __init__(self, ka, vt, sa, sb, m, l, acc):
        self.ka, self.vt, self.sa, self.sb, self.m, self.l, self.acc = ka, vt, sa, sb, m, l, acc


def _fox_kernel(q_ref, k_ref, v_ref, a_ref, g_ref, *refs, tq, n_cast):
    o_ref = refs[n_cast]
    for src, dst in zip(refs[:n_cast], refs[n_cast + 1:2 * n_cast + 1]):
        dst[...] = src[...].astype(BF16)
    scratch = refs[2 * n_cast + 1:]
    heads = [_FoxHead(*scratch[n * 7:(n + 1) * 7]) for n in range(FOX_HEADS_PER_STEP)]
    i = pl.program_id(2)
    lanes = lambda n: slice(n * LANE, (n + 1) * LANE)

    @pl.when(i == 0)
    def _():
        for n, hd in enumerate(heads):
            hd.ka[:, 0:LANE] = k_ref[0, :, lanes(n)]
            hd.ka[:, LANE:] = a_ref[0]
            hd.vt[...] = v_ref[0, :, lanes(n)].T

    col = lax.broadcasted_iota(jnp.int32, (tq, LANE), 1)
    qas = []
    for n, hd in enumerate(heads):
        h = pl.program_id(1) * FOX_HEADS_PER_STEP + n
        pick = (col == h) | (col == h + FOX_HEADS) | (col == h + 2 * FOX_HEADS)
        qas.append(jnp.concatenate([q_ref[0, :, lanes(n)],
                                    jnp.where(pick, -1.0, 0.0).astype(BF16)], axis=1))
        hd.m[...] = jnp.full_like(hd.m, NEG_BIG)
        hd.l[...] = jnp.zeros_like(hd.l)
        hd.acc[...] = jnp.zeros_like(hd.acc)

    def logits_to(pick_buf, blk):
        keys = pl.ds(pl.multiple_of(blk * tq, tq), tq)
        for hd, qa in zip(heads, qas):
            pick_buf(hd)[...] = _nt_dot(hd.ka[keys, :], qa)

    def absorb(pick_buf, blk, masked):
        keys = pl.ds(pl.multiple_of(blk * tq, tq), tq)
        for hd in heads:
            st = pick_buf(hd)[...]
            if masked:
                kr = lax.broadcasted_iota(jnp.int32, (tq, tq), 0)
                qc = lax.broadcasted_iota(jnp.int32, (tq, tq), 1)
                st = jnp.where(kr <= qc, st, NEG_BIG)
            m = hd.m[...]
            m_new = jnp.maximum(m, jnp.max(st, axis=0, keepdims=True))
            alpha = jnp.exp2(m - m_new)
            p = jnp.exp2(st - m_new)
            hd.m[...] = m_new
            hd.l[...] = alpha * hd.l[...] + jnp.sum(p, axis=0, keepdims=True)
            pv = jnp.dot(hd.vt[:, keys], p.astype(BF16), preferred_element_type=F32)
            hd.acc[...] = alpha * hd.acc[...] + pv

    buf_a = lambda hd: hd.sa
    buf_b = lambda hd: hd.sb

    logits_to(buf_a, 0)

    def block_pair(t, carry):
        logits_to(buf_b, 2 * t + 1)
        absorb(buf_a, 2 * t, False)
        logits_to(buf_a, 2 * t + 2)
        absorb(buf_b, 2 * t + 1, False)
        return carry

    lax.fori_loop(0, i // 2, block_pair, 0)

    @pl.when(i % 2 == 0)
    def _():
        absorb(buf_a, i, True)

    @pl.when(i % 2 == 1)
    def _():
        logits_to(buf_b, i)
        absorb(buf_a, i - 1, False)
        absorb(buf_b, i, True)

    for n, hd in enumerate(heads):
        ot = hd.acc[...] / hd.l[...]
        ot = ot * lax.rsqrt(jnp.mean(ot * ot, axis=0, keepdims=True) + EPS)
        o_ref[0, :, lanes(n)] = (ot.T * g_ref[n]).astype(BF16)


def _fox_attn(proj3, pieces3, g_fox, f32_weights, tq=512):
    b, s, _ = proj3.shape
    nh = FOX_HEADS_PER_STEP
    groups = FOX_HEADS // nh
    width = nh * FOX_HEAD_DIM
    nq = s // tq
    n_steps = b * groups * nq
    step = lambda bi, gi, qi: ((bi * groups + gi) * nq + qi, 0)
    slabs = [pl.BlockSpec((w.shape[0] // n_steps, w.shape[1]), step) for w in f32_weights]
    assert all(w.shape[0] % (n_steps * BF16_ROWS) == 0 for w in f32_weights)
    per_head = [pltpu.VMEM((s, 2 * LANE), BF16), pltpu.VMEM((FOX_HEAD_DIM, s), BF16),
                pltpu.VMEM((tq, tq), F32), pltpu.VMEM((tq, tq), F32),
                pltpu.VMEM((1, tq), F32), pltpu.VMEM((1, tq), F32),
                pltpu.VMEM((FOX_HEAD_DIM, tq), F32)]
    outs = pl.pallas_call(
        functools.partial(_fox_kernel, tq=tq, n_cast=len(f32_weights)),
        out_shape=[jax.ShapeDtypeStruct((b, s, FOX_HEADS * FOX_HEAD_DIM), BF16)]
        + [jax.ShapeDtypeStruct(w.shape, BF16) for w in f32_weights],
        grid=(b, groups, nq),
        in_specs=[pl.BlockSpec((1, tq, width), lambda bi, gi, qi: (bi, qi, gi)),
                  pl.BlockSpec((1, s, width), lambda bi, gi, qi: (bi, 0, groups + gi)),
                  pl.BlockSpec((1, s, width), lambda bi, gi, qi: (bi, 0, 2 * groups + gi)),
                  pl.BlockSpec((1, s, LANE), lambda bi, gi, qi: (bi, 0, 0)),
                  pl.BlockSpec((nh, 1, LANE), lambda bi, gi, qi: (gi, 0, 0))] + slabs,
        out_specs=[pl.BlockSpec((1, tq, width), lambda bi, gi, qi: (bi, qi, gi))] + slabs,
        scratch_shapes=per_head * nh,
        compiler_params=pltpu.CompilerParams(
            dimension_semantics=("parallel", "parallel", "arbitrary"),
            vmem_limit_bytes=VMEM_LIMIT),
        name="fox_attn",
    )(proj3, proj3, proj3, pieces3, g_fox, *f32_weights)
    return outs[0], outs[1:]


def _gla_kernel(q_ref, k_ref, v_ref, r_ref, s_ref, wa_ref, ba_ref, gg_ref, o_ref, st_ref, *, ts):
    @pl.when(pl.program_id(2) == 0)
    def _():
        st_ref[...] = jnp.zeros_like(st_ref)

    pre = jnp.dot(s_ref[0].astype(BF16), wa_ref[...], preferred_element_type=F32) + ba_ref[...]
    la = _log_sigmoid(pre)
    tri = _lower_tri(CHUNK, 1.0 / GLA_GATE_TEMP)
    chunks = [slice(c * CHUNK, (c + 1) * CHUNK) for c in range(ts // CHUNK)]
    cums = [_tri_cumsum(tri, la[sl]) for sl in chunks]
    tots = [cum[CHUNK - 1:CHUNK, :] for cum in cums]
    ups = []
    for sl, cum, tot in zip(chunks, cums, tots):
        kd = (k_ref[0, sl, :].astype(F32) * jnp.exp(tot - cum)).astype(BF16)
        ups.append(_tn_dot(v_ref[0, sl, :], kd))
    st = st_ref[...]
    states = []
    for tot, up in zip(tots, ups):
        st = st * jnp.exp(tot) + up
        states.append(st.astype(BF16))
    st_ref[...] = st
    outs = [_nt_dot(q_ref[0, sl, :], sb) for sl, sb in zip(chunks, states)]
    o = jnp.concatenate(outs, axis=0)
    qs = GLA_DK ** -0.5
    inv = qs * lax.rsqrt(qs * qs * jnp.mean(o * o, axis=-1, keepdims=True) + EPS)
    o = o * inv * gg_ref[0]
    r = r_ref[0].astype(F32)
    o_ref[0] = (o * (r * _sigmoid(r))).astype(BF16)


def _gla(proj3, small3, wa_pad, ba_row, g_gla, ts=1024):
    b, s, _ = proj3.shape
    q_blk = 3 * FOX_HEADS
    k_blk = q_blk + GLA_HEADS
    v_blk = (k_blk + GLA_HEADS) // 2
    r_blk = v_blk + GLA_HEADS
    return pl.pallas_call(
        functools.partial(_gla_kernel, ts=ts),
        out_shape=jax.ShapeDtypeStruct((b, s, GLA_HEADS * GLA_DV), BF16),
        grid=(b, GLA_HEADS, s // ts),
        in_specs=[pl.BlockSpec((1, ts, GLA_DK), lambda bi, gi, ti: (bi, ti, q_blk + gi)),
                  pl.BlockSpec((1, ts, GLA_DK), lambda bi, gi, ti: (bi, ti, k_blk + gi)),
                  pl.BlockSpec((1, ts, GLA_DV), lambda bi, gi, ti: (bi, ti, v_blk + gi)),
                  pl.BlockSpec((1, ts, GLA_DV), lambda bi, gi, ti: (bi, ti, r_blk + gi)),
                  pl.BlockSpec((1, ts, LANE), lambda bi, gi, ti: (bi, ti, 0)),
                  pl.BlockSpec((LANE, GLA_DK), lambda bi, gi, ti: (0, gi)),
                  pl.BlockSpec((1, GLA_DK), lambda bi, gi, ti: (0, gi)),
                  pl.BlockSpec((1, 1, GLA_DV), lambda bi, gi, ti: (gi, 0, 0))],
        out_specs=pl.BlockSpec((1, ts, GLA_DV), lambda bi, gi, ti: (bi, ti, gi)),
        scratch_shapes=[pltpu.VMEM((GLA_DV, GLA_DK), F32)],
        compiler_params=pltpu.CompilerParams(
            dimension_semantics=("parallel", "parallel", "arbitrary"),
            vmem_limit_bytes=VMEM_LIMIT),
        name="gla",
    )(proj3, proj3, proj3, proj3, small3, wa_pad, ba_row, g_gla)


def _outproj_kernel(fox_ref, gla_ref, w_ref, x_ref, mod_ref, g_ref, o_ref):
    half = fox_ref.shape[1]
    y = (jnp.dot(fox_ref[...], w_ref[0:half, :], preferred_element_type=F32)
         + jnp.dot(gla_ref[...], w_ref[half:, :], preferred_element_type=F32))
    inv = lax.rsqrt(jnp.mean(y * y, axis=-1, keepdims=True) + EPS)
    o_ref[...] = x_ref[...] + y * inv * (mod_ref[0, 2:3, :] * g_ref[...])


def _out_proj(fox2, gla2, w_out, x2, mod, g, rows_per_batch, tm=512):
    t, d = x2.shape
    half = fox2.shape[1]
    tiles_per_batch = rows_per_batch // tm
    return pl.pallas_call(
        _outproj_kernel,
        out_shape=jax.ShapeDtypeStruct((t, d), F32),
        grid=(t // tm,),
        in_specs=[pl.BlockSpec((tm, half), lambda i: (i, 0)),
                  pl.BlockSpec((tm, half), lambda i: (i, 0)),
                  pl.BlockSpec((2 * half, d), lambda i: (0, 0)),
                  pl.BlockSpec((tm, d), lambda i: (i, 0)),
                  pl.BlockSpec((1, N_MOD, d), lambda i: (i // tiles_per_batch, 0, 0)),
                  pl.BlockSpec((1, d), lambda i: (0, 0))],
        out_specs=pl.BlockSpec((tm, d), lambda i: (i, 0)),
        compiler_params=pltpu.CompilerParams(dimension_semantics=("parallel",),
                                             vmem_limit_bytes=VMEM_LIMIT),
        name="out_proj",
    )(fox2, gla2, w_out, x2, mod, g)


def _mlp_kernel(x_ref, mod_ref, gpre_ref, gpost_ref, w1_ref, w2_ref, o_ref, h_scr, acc_scr):
    f = pl.program_id(1)

    def hidden_block(hb):
        u = jnp.maximum(jnp.dot(hb, w1_ref[...], preferred_element_type=F32), 0.0)
        return jnp.dot((u * u).astype(BF16), w2_ref[...], preferred_element_type=F32)

    @pl.when(f == 0)
    def _():
        x = x_ref[...]
        inv = lax.rsqrt(jnp.mean(x * x, axis=-1, keepdims=True) + EPS)
        gain = gpre_ref[...] * (1.0 + mod_ref[0, 4:5, :])
        hb = (x * inv * gain + mod_ref[0, 3:4, :]).astype(BF16)
        h_scr[...] = hb
        acc_scr[...] = hidden_block(hb)

    last = pl.num_programs(1) - 1

    @pl.when((f > 0) & (f < last))
    def _():
        acc_scr[...] += hidden_block(h_scr[...])

    @pl.when(f == last)
    def _():
        gain = mod_ref[0, 5:6, :] * gpost_ref[...]
        half = x_ref.shape[0] // 2
        for rows in (slice(0, half), slice(half, 2 * half)):
            y = acc_scr[rows, :] + hidden_block(h_scr[rows, :])
            inv = lax.rsqrt(jnp.mean(y * y, axis=-1, keepdims=True) + EPS)
            o_ref[rows, :] = x_ref[rows, :] + y * inv * gain


def _mlp(x1, mod, g_pre, g_post, w1, w2, rows_per_batch, tm=512, tf=1024):
    t, d = x1.shape
    dff = w1.shape[1]
    tiles_per_batch = rows_per_batch // tm
    return pl.pallas_call(
        _mlp_kernel,
        out_shape=jax.ShapeDtypeStruct((t, d), F32),
        grid=(t // tm, dff // tf),
        in_specs=[pl.BlockSpec((tm, d), lambda i, f: (i, 0)),
                  pl.BlockSpec((1, N_MOD, d), lambda i, f: (i // tiles_per_batch, 0, 0)),
                  pl.BlockSpec((1, d), lambda i, f: (0, 0)),
                  pl.BlockSpec((1, d), lambda i, f: (0, 0)),
                  pl.BlockSpec((d, tf), lambda i, f: (0, f)),
                  pl.BlockSpec((tf, d), lambda i, f: (f, 0))],
        out_specs=pl.BlockSpec((tm, d), lambda i, f: (i, 0)),
        scratch_shapes=[pltpu.VMEM((tm, d), BF16), pltpu.VMEM((tm, d), F32)],
        compiler_params=pltpu.CompilerParams(dimension_semantics=("parallel", "arbitrary"),
                                             vmem_limit_bytes=VMEM_LIMIT),
        name="mlp",
    )(x1, mod, g_pre, g_post, w1, w2)


def _transposed_in_proj_weight(w_in):
    fox_end = 3 * FOX_HEADS * FOX_HEAD_DIM
    gla_start = fox_end + FOX_HEADS
    gla_end = gla_start + 2 * GLA_HEADS * GLA_DK + GLA_HEADS * GLA_DV
    gate_start = gla_end + GLA_GATE_RANK
    wt = jnp.swapaxes(w_in, 0, 1).astype(BF16)
    pad = jnp.zeros((LANE - FOX_HEADS - GLA_GATE_RANK, w_in.shape[0]), BF16)
    small = jnp.concatenate([wt[fox_end:gla_start], wt[gla_end:gate_start], pad], axis=0)
    return wt, small


def kernel(x, c, w_ada, b_ada, g_pre_mix, g_post_mix, w_in, b_fgate, w_gla_a2, b_gla_a2,
           g_fox_out, g_gla_out, w_out, g_pre_mlp, g_post_mlp, w_mlp_in, w_mlp_out):
    b, s, d = x.shape
    depth = w_ada.shape[0]
    row = lambda v: v.reshape(1, -1)
    for i in range(depth):
        c_pad = jnp.concatenate([c, jnp.zeros((8 - b, d), c.dtype)], axis=0)
        mod = _ada(c_pad, w_ada[i], row(b_ada[i]))[:b].reshape(b, N_MOD, d)

        wt, w_small = _transposed_in_proj_weight(w_in[i])
        x2 = x.reshape(b * s, d)
        proj, small = _in_proj(x2, mod, row(g_pre_mix[i]), wt, w_small, s)
        proj3 = proj.reshape(b, s, -1)
        small3 = small.reshape(b, s, LANE)

        fbias = jnp.concatenate([b_fgate[i], jnp.zeros((LANE - FOX_HEADS,), F32)]).reshape(1, LANE)
        fox, (w_out_b, w_mlp_in_b, w_mlp_out_b) = _fox_attn(
            proj3, _fox_cum(small3, fbias), g_fox_out[i].reshape(FOX_HEADS, 1, FOX_HEAD_DIM),
            [w_out[i], w_mlp_in[i], w_mlp_out[i]])

        kw = GLA_HEADS * GLA_DK
        wa_pad = jnp.concatenate(
            [jnp.zeros((FOX_HEADS, kw), F32), w_gla_a2[i],
             jnp.zeros((LANE - FOX_HEADS - GLA_GATE_RANK, kw), F32)], axis=0).astype(BF16)
        gla = _gla(proj3, small3, wa_pad, row(b_gla_a2[i]),
                   g_gla_out[i].reshape(GLA_HEADS, 1, GLA_DV))

        x1 = _out_proj(fox.reshape(b * s, -1), gla.reshape(b * s, -1), w_out_b,
                       x2, mod, row(g_post_mix[i]), s)
        x2 = _mlp(x1, mod, row(g_pre_mlp[i]), row(g_post_mlp[i]), w_mlp_in_b, w_mlp_out_b, s)
        x = x2.reshape(b, s, d)
    return x
```

```python
import functools

import jax
import jax.numpy as jnp
from jax import lax
from jax.experimental import pallas as pl
from jax.experimental.pallas import tpu as pltpu

F32 = jnp.float32
BF16 = jnp.bfloat16

EPS = 1e-6
N_MOD = 6
FOX_HEADS = 8
FOX_HEAD_DIM = 128
GLA_HEADS = 4
GLA_DK = 128
GLA_DV = 256
GLA_GATE_RANK = 16
GLA_GATE_TEMP = 16.0
CHUNK = 64
LANE = 128
BF16_ROWS = 16
VMEM_LIMIT = 56 * 1024 * 1024

NEG_BIG = -1e30
LOG2E = 1.4426950408889634
FOX_HEADS_PER_STEP = 4


def _nt_dot(a, b):
    return lax.dot_general(a, b, (((1,), (1,)), ((), ())), preferred_element_type=F32)


def _tn_dot(a, b):
    return lax.dot_general(a, b, (((0,), (0,)), ((), ())), preferred_element_type=F32)


def _log_sigmoid(z):
    return jnp.minimum(z, 0.0) - jnp.log(1.0 + jnp.exp(-jnp.abs(z)))


def _sigmoid(z):
    return 1.0 / (1.0 + jnp.exp(-z))


def _tri_cumsum(tri, v):
    hi = v.astype(BF16)
    lo = (v - hi.astype(F32)).astype(BF16)
    return (jnp.dot(tri, hi, preferred_element_type=F32)
            + jnp.dot(tri, lo, preferred_element_type=F32))


def _lower_tri(n, value=1.0):
    r = lax.broadcasted_iota(jnp.int32, (n, n), 0)
    c = lax.broadcasted_iota(jnp.int32, (n, n), 1)
    return jnp.where(r >= c, value, 0.0).astype(BF16)


def _ada_kernel(c_ref, w_ref, b_ref, o_ref):
    c = c_ref[...]
    act = (c * _sigmoid(c)).astype(BF16)
    o_ref[...] = jnp.dot(act, w_ref[...].astype(BF16), preferred_element_type=F32) + b_ref[...]


def _ada(c_pad, w_ada, b_ada, tn=1024):
    m, d = c_pad.shape
    n = w_ada.shape[1]
    return pl.pallas_call(
        _ada_kernel,
        out_shape=jax.ShapeDtypeStruct((m, n), F32),
        grid=(n // tn,),
        in_specs=[pl.BlockSpec((m, d), lambda j: (0, 0)),
                  pl.BlockSpec((d, tn), lambda j: (0, j)),
                  pl.BlockSpec((1, tn), lambda j: (0, j))],
        out_specs=pl.BlockSpec((m, tn), lambda j: (0, j)),
        compiler_params=pltpu.CompilerParams(dimension_semantics=("parallel",),
                                             vmem_limit_bytes=VMEM_LIMIT),
        name="ada",
    )(c_pad, w_ada, b_ada)


def _inproj_kernel(x_ref, mod_ref, g_ref, wlo_ref, whi_ref, ws_ref, o_ref, os_ref, h_scr):
    j = pl.program_id(1)
    half = wlo_ref.shape[0]

    @pl.when(j == 0)
    def _():
        x = x_ref[...]
        inv = lax.rsqrt(jnp.mean(x * x, axis=-1, keepdims=True) + EPS)
        gain = g_ref[...] * (1.0 + mod_ref[0, 1:2, :])
        hb = (x * inv * gain + mod_ref[0, 0:1, :]).astype(BF16)
        h_scr[...] = hb
        os_ref[...] = _nt_dot(hb, ws_ref[...])
        o_ref[:, :half] = (_nt_dot(hb, wlo_ref[...]) * (FOX_HEAD_DIM ** -0.5 * LOG2E)).astype(BF16)
        o_ref[:, half:] = _nt_dot(hb, whi_ref[...]).astype(BF16)

    @pl.when(j > 0)
    def _():
        o_ref[:, :half] = _nt_dot(h_scr[...], wlo_ref[...]).astype(BF16)
        o_ref[:, half:] = _nt_dot(h_scr[...], whi_ref[...]).astype(BF16)


def _in_proj(x2, mod, g, wt, w_small, rows_per_batch, tm=1024):
    t, d = x2.shape
    half = FOX_HEADS * FOX_HEAD_DIM
    n = wt.shape[0] - FOX_HEADS - GLA_GATE_RANK
    tn = 2 * half
    assert n % tn == 0 and GLA_HEADS * (2 * GLA_DK + GLA_DV) == tn
    tiles_per_batch = rows_per_batch // tm
    lo_start = lambda j: pl.multiple_of(j * tn + jnp.where(j >= 2, FOX_HEADS, 0), FOX_HEADS)
    hi_start = lambda j: pl.multiple_of(j * tn + half + jnp.where(j >= 1, FOX_HEADS, 0)
                                        + jnp.where(j >= 2, GLA_GATE_RANK, 0), FOX_HEADS)
    return pl.pallas_call(
        _inproj_kernel,
        out_shape=(jax.ShapeDtypeStruct((t, n), BF16),
                   jax.ShapeDtypeStruct((t, LANE), F32)),
        grid=(t // tm, n // tn),
        in_specs=[pl.BlockSpec((tm, d), lambda i, j: (i, 0)),
                  pl.BlockSpec((1, N_MOD, d), lambda i, j: (i // tiles_per_batch, 0, 0)),
                  pl.BlockSpec((1, d), lambda i, j: (0, 0)),
                  pl.BlockSpec((pl.Element(half), pl.Element(d)), lambda i, j: (lo_start(j), 0)),
                  pl.BlockSpec((pl.Element(half), pl.Element(d)), lambda i, j: (hi_start(j), 0)),
                  pl.BlockSpec((LANE, d), lambda i, j: (0, 0))],
        out_specs=(pl.BlockSpec((tm, tn), lambda i, j: (i, j)),
                   pl.BlockSpec((tm, LANE), lambda i, j: (i, 0))),
        scratch_shapes=[pltpu.VMEM((tm, d), BF16)],
        compiler_params=pltpu.CompilerParams(dimension_semantics=("parallel", "arbitrary"),
                                             vmem_limit_bytes=VMEM_LIMIT),
        name="in_proj",
    )(x2, mod, g, wt, wt, w_small)


def _cum_kernel(s_ref, b_ref, o_ref, *, blk):
    s = s_ref.shape[1]
    tri = _lower_tri(blk)
    col = lax.broadcasted_iota(jnp.int32, (blk, LANE), 1)
    carry = jnp.zeros((1, LANE), F32)
    for r in range(s // blk):
        rows = slice(r * blk, (r + 1) * blk)
        z = s_ref[0, rows, :] + b_ref[...]
        cs = _tri_cumsum(tri, _log_sigmoid(z)) + carry
        carry = cs[blk - 1:blk, :]
        c2 = jnp.where(col < FOX_HEADS, cs * LOG2E, 0.0)
        hi = c2.astype(BF16).astype(F32)
        rem = c2 - hi
        mid = rem.astype(BF16).astype(F32)
        lo = (rem - mid).astype(BF16).astype(F32)
        pieces = hi + pltpu.roll(mid, FOX_HEADS, 1) + pltpu.roll(lo, 2 * FOX_HEADS, 1)
        o_ref[0, rows, :] = pieces.astype(BF16)


def _fox_cum(small3, bias_row, blk=256):
    b, s, _ = small3.shape
    return pl.pallas_call(
        functools.partial(_cum_kernel, blk=blk),
        out_shape=jax.ShapeDtypeStruct((b, s, LANE), BF16),
        grid=(b,),
        in_specs=[pl.BlockSpec((1, s, LANE), lambda i: (i, 0, 0)),
                  pl.BlockSpec((1, LANE), lambda i: (0, 0))],
        out_specs=pl.BlockSpec((1, s, LANE), lambda i: (i, 0, 0)),
        compiler_params=pltpu.CompilerParams(dimension_semantics=("parallel",),
                                             vmem_limit_bytes=VMEM_LIMIT),
        name="fox_cum",
    )(small3, bias_row)


class _FoxHead:
    def __init__(self, ka, vt, sa, sb, m, l, acc):
        self.ka, self.vt, self.sa, self.sb, self.m, self.l, self.acc = ka, vt, sa, sb, m, l, acc


def _fox_kernel(q_ref, k_ref, v_ref, a_ref, g_ref, o_ref, *scratch, tq):
    heads = [_FoxHead(*scratch[n * 7:(n + 1) * 7]) for n in range(FOX_HEADS_PER_STEP)]
    t = pl.program_id(2)
    lanes = lambda n: slice(n * LANE, (n + 1) * LANE)

    @pl.when(t == 0)
    def _():
        for n, hd in enumerate(heads):
            hd.ka[:, 0:LANE] = k_ref[0, :, lanes(n)]
            hd.ka[:, LANE:] = a_ref[0]
            hd.vt[...] = v_ref[0, :, lanes(n)].T

    col = lax.broadcasted_iota(jnp.int32, (tq, LANE), 1)

    def tile_queries(half):
        rows = slice(half * tq, (half + 1) * tq)
        qas = []
        for n in range(len(heads)):
            h = pl.program_id(1) * FOX_HEADS_PER_STEP + n
            pick = (col == h) | (col == h + FOX_HEADS) | (col == h + 2 * FOX_HEADS)
            qas.append(jnp.concatenate([q_ref[0, rows, lanes(n)],
                                        jnp.where(pick, -1.0, 0.0).astype(BF16)], axis=1))
        return qas

    def reset_state():
        for hd in heads:
            hd.m[...] = jnp.full_like(hd.m, NEG_BIG)
            hd.l[...] = jnp.zeros_like(hd.l)
            hd.acc[...] = jnp.zeros_like(hd.acc)

    def logits_to(pick_buf, qas, blk):
        keys = pl.ds(pl.multiple_of(blk * tq, tq), tq)
        for hd, qa in zip(heads, qas):
            pick_buf(hd)[...] = _nt_dot(hd.ka[keys, :], qa)

    def absorb(pick_buf, blk, masked):
        keys = pl.ds(pl.multiple_of(blk * tq, tq), tq)
        for hd in heads:
            st = pick_buf(hd)[...]
            if masked:
                kr = lax.broadcasted_iota(jnp.int32, (tq, tq), 0)
                qc = lax.broadcasted_iota(jnp.int32, (tq, tq), 1)
                st = jnp.where(kr <= qc, st, NEG_BIG)
            m = hd.m[...]
            m_new = jnp.maximum(m, jnp.max(st, axis=0, keepdims=True))
            alpha = jnp.exp2(m - m_new)
            p = jnp.exp2(st - m_new)
            hd.m[...] = m_new
            hd.l[...] = alpha * hd.l[...] + jnp.sum(p, axis=0, keepdims=True)
            pv = jnp.dot(hd.vt[:, keys], p.astype(BF16), preferred_element_type=F32)
            hd.acc[...] = alpha * hd.acc[...] + pv

    def full_block_pairs(first, second, qas):
        def body(u, carry):
            logits_to(second, qas, 2 * u + 1)
            absorb(first, 2 * u, False)
            logits_to(first, qas, 2 * u + 2)
            absorb(second, 2 * u + 1, False)
            return carry
        lax.fori_loop(0, t, body, 0)

    def finish(half):
        for n, hd in enumerate(heads):
            ot = hd.acc[...] / hd.l[...]
            ot = ot * lax.rsqrt(jnp.mean(ot * ot, axis=0, keepdims=True) + EPS)
            o_ref[0, half * tq:(half + 1) * tq, lanes(n)] = (ot.T * g_ref[n]).astype(BF16)

    buf_a = lambda hd: hd.sa
    buf_b = lambda hd: hd.sb
    q_even, q_odd = tile_queries(0), tile_queries(1)

    reset_state()
    logits_to(buf_a, q_even, 0)
    full_block_pairs(buf_a, buf_b, q_even)
    logits_to(buf_b, q_odd, 0)
    absorb(buf_a, 2 * t, True)
    finish(0)

    reset_state()
    full_block_pairs(buf_b, buf_a, q_odd)
    logits_to(buf_a, q_odd, 2 * t + 1)
    absorb(buf_b, 2 * t, False)
    absorb(buf_a, 2 * t + 1, True)
    finish(1)


def _fox_attn(proj3, pieces3, g_fox, tq=512):
    b, s, _ = proj3.shape
    nh = FOX_HEADS_PER_STEP
    groups = FOX_HEADS // nh
    width = nh * FOX_HEAD_DIM
    per_head = [pltpu.VMEM((s, 2 * LANE), BF16), pltpu.VMEM((FOX_HEAD_DIM, s), BF16),
                pltpu.VMEM((tq, tq), F32), pltpu.VMEM((tq, tq), F32),
                pltpu.VMEM((1, tq), F32), pltpu.VMEM((1, tq), F32),
                pltpu.VMEM((FOX_HEAD_DIM, tq), F32)]
    return pl.pallas_call(
        functools.partial(_fox_kernel, tq=tq),
        out_shape=jax.ShapeDtypeStruct((b, s, FOX_HEADS * FOX_HEAD_DIM), BF16),
        grid=(b, groups, s // (2 * tq)),
        in_specs=[pl.BlockSpec((1, 2 * tq, width), lambda bi, gi, ti: (bi, ti, gi)),
                  pl.BlockSpec((1, s, width), lambda bi, gi, ti: (bi, 0, groups + gi)),
                  pl.BlockSpec((1, s, width), lambda bi, gi, ti: (bi, 0, 2 * groups + gi)),
                  pl.BlockSpec((1, s, LANE), lambda bi, gi, ti: (bi, 0, 0)),
                  pl.BlockSpec((nh, 1, LANE), lambda bi, gi, ti: (gi, 0, 0))],
        out_specs=pl.BlockSpec((1, 2 * tq, width), lambda bi, gi, ti: (bi, ti, gi)),
        scratch_shapes=per_head * nh,
        compiler_params=pltpu.CompilerParams(
            dimension_semantics=("parallel", "parallel", "arbitrary"),
            vmem_limit_bytes=VMEM_LIMIT),
        name="fox_attn",
    )(proj3, proj3, proj3, pieces3, g_fox)


def _gla_kernel(q_ref, k_ref, v_ref, r_ref, s_ref, wa_ref, ba_ref, gg_ref, *refs, ts, n_cast):
    o_ref, st_ref = refs[n_cast], refs[-1]
    for src, dst in zip(refs[:n_cast], refs[n_cast + 1:2 * n_cast + 1]):
        dst[...] = src[...].astype(BF16)

    @pl.when(pl.program_id(2) == 0)
    def _():
        st_ref[...] = jnp.zeros_like(st_ref)

    pre = jnp.dot(s_ref[0].astype(BF16), wa_ref[...], preferred_element_type=F32) + ba_ref[...]
    la = _log_sigmoid(pre)
    tri = _lower_tri(CHUNK, 1.0 / GLA_GATE_TEMP)
    chunks = [slice(c * CHUNK, (c + 1) * CHUNK) for c in range(ts // CHUNK)]
    cums = [_tri_cumsum(tri, la[sl]) for sl in chunks]
    tots = [cum[CHUNK - 1:CHUNK, :] for cum in cums]
    ups = []
    for sl, cum, tot in zip(chunks, cums, tots):
        kd = (k_ref[0, sl, :].astype(F32) * jnp.exp(tot - cum)).astype(BF16)
        ups.append(_tn_dot(v_ref[0, sl, :], kd))
    st = st_ref[...]
    states = []
    for tot, up in zip(tots, ups):
        st = st * jnp.exp(tot) + up
        states.append(st.astype(BF16))
    st_ref[...] = st
    outs = [_nt_dot(q_ref[0, sl, :], sb) for sl, sb in zip(chunks, states)]
    o = jnp.concatenate(outs, axis=0)
    qs = GLA_DK ** -0.5
    inv = qs * lax.rsqrt(qs * qs * jnp.mean(o * o, axis=-1, keepdims=True) + EPS)
    o = o * inv * gg_ref[0]
    r = r_ref[0].astype(F32)
    o_ref[0] = (o * (r * _sigmoid(r))).astype(BF16)


def _gla(proj3, small3, wa_pad, ba_row, g_gla, f32_weights, ts=1024):
    b, s, _ = proj3.shape
    q_blk = 3 * FOX_HEADS
    k_blk = q_blk + GLA_HEADS
    v_blk = (k_blk + GLA_HEADS) // 2
    r_blk = v_blk + GLA_HEADS
    nt = s // ts
    n_steps = b * GLA_HEADS * nt
    step = lambda bi, gi, ti: ((bi * GLA_HEADS + gi) * nt + ti, 0)
    slabs = [pl.BlockSpec((w.shape[0] // n_steps, w.shape[1]), step) for w in f32_weights]
    assert all(w.shape[0] % (n_steps * BF16_ROWS) == 0 for w in f32_weights)
    outs = pl.pallas_call(
        functools.partial(_gla_kernel, ts=ts, n_cast=len(f32_weights)),
        out_shape=[jax.ShapeDtypeStruct((b, s, GLA_HEADS * GLA_DV), BF16)]
        + [jax.ShapeDtypeStruct(w.shape, BF16) for w in f32_weights],
        grid=(b, GLA_HEADS, nt),
        in_specs=[pl.BlockSpec((1, ts, GLA_DK), lambda bi, gi, ti: (bi, ti, q_blk + gi)),
                  pl.BlockSpec((1, ts, GLA_DK), lambda bi, gi, ti: (bi, ti, k_blk + gi)),
                  pl.BlockSpec((1, ts, GLA_DV), lambda bi, gi, ti: (bi, ti, v_blk + gi)),
                  pl.BlockSpec((1, ts, GLA_DV), lambda bi, gi, ti: (bi, ti, r_blk + gi)),
                  pl.BlockSpec((1, ts, LANE), lambda bi, gi, ti: (bi, ti, 0)),
                  pl.BlockSpec((LANE, GLA_DK), lambda bi, gi, ti: (0, gi)),
                  pl.BlockSpec((1, GLA_DK), lambda bi, gi, ti: (0, gi)),
                  pl.BlockSpec((1, 1, GLA_DV), lambda bi, gi, ti: (gi, 0, 0))] + slabs,
        out_specs=[pl.BlockSpec((1, ts, GLA_DV), lambda bi, gi, ti: (bi, ti, gi))] + slabs,
        scratch_shapes=[pltpu.VMEM((GLA_DV, GLA_DK), F32)],
        compiler_params=pltpu.CompilerParams(
            dimension_semantics=("parallel", "parallel", "arbitrary"),
            vmem_limit_bytes=VMEM_LIMIT),
        name="gla",
    )(proj3, proj3, proj3, proj3, small3, wa_pad, ba_row, g_gla, *f32_weights)
    return outs[0], outs[1:]


def _outproj_kernel(fox_ref, gla_ref, w_ref, x_ref, mod_ref, g_ref, o_ref):
    half = fox_ref.shape[1]
    y = (jnp.dot(fox_ref[...], w_ref[0:half, :], preferred_element_type=F32)
         + jnp.dot(gla_ref[...], w_ref[half:, :], preferred_element_type=F32))
    inv = lax.rsqrt(jnp.mean(y * y, axis=-1, keepdims=True) + EPS)
    o_ref[...] = x_ref[...] + y * inv * (mod_ref[0, 2:3, :] * g_ref[...])


def _out_proj(fox2, gla2, w_out, x2, mod, g, rows_per_batch, tm=512):
    t, d = x2.shape
    half = fox2.shape[1]
    tiles_per_batch = rows_per_batch // tm
    return pl.pallas_call(
        _outproj_kernel,
        out_shape=jax.ShapeDtypeStruct((t, d), F32),
        grid=(t // tm,),
        in_specs=[pl.BlockSpec((tm, half), lambda i: (i, 0)),
                  pl.BlockSpec((tm, half), lambda i: (i, 0)),
                  pl.BlockSpec((2 * half, d), lambda i: (0, 0)),
                  pl.BlockSpec((tm, d), lambda i: (i, 0)),
                  pl.BlockSpec((1, N_MOD, d), lambda i: (i // tiles_per_batch, 0, 0)),
                  pl.BlockSpec((1, d), lambda i: (0, 0))],
        out_specs=pl.BlockSpec((tm, d), lambda i: (i, 0)),
        compiler_params=pltpu.CompilerParams(dimension_semantics=("parallel",),
                                             vmem_limit_bytes=VMEM_LIMIT),
        name="out_proj",
    )(fox2, gla2, w_out, x2, mod, g)


def _mlp_kernel(x_ref, mod_ref, gpre_ref, gpost_ref, w1_ref, w2_ref, o_ref, h_scr, acc_scr):
    f = pl.program_id(1)

    def hidden_block(hb):
        u = jnp.maximum(jnp.dot(hb, w1_ref[...], preferred_element_type=F32), 0.0)
        return jnp.dot((u * u).astype(BF16), w2_ref[...], preferred_element_type=F32)

    @pl.when(f == 0)
    def _():
        x = x_ref[...]
        inv = lax.rsqrt(jnp.mean(x * x, axis=-1, keepdims=True) + EPS)
        gain = gpre_ref[...] * (1.0 + mod_ref[0, 4:5, :])
        hb = (x * inv * gain + mod_ref[0, 3:4, :]).astype(BF16)
        h_scr[...] = hb
        acc_scr[...] = hidden_block(hb)

    last = pl.num_programs(1) - 1

    @pl.when((f > 0) & (f < last))
    def _():
        acc_scr[...] += hidden_block(h_scr[...])

    @pl.when(f == last)
    def _():
        gain = mod_ref[0, 5:6, :] * gpost_ref[...]
        half = x_ref.shape[0] // 2
        for rows in (slice(0, half), slice(half, 2 * half)):
            y = acc_scr[rows, :] + hidden_block(h_scr[rows, :])
            inv = lax.rsqrt(jnp.mean(y * y, axis=-1, keepdims=True) + EPS)
            o_ref[rows, :] = x_ref[rows, :] + y * inv * gain


def _mlp(x1, mod, g_pre, g_post, w1, w2, rows_per_batch, tm=512, tf=1024):
    t, d = x1.shape
    dff = w1.shape[1]
    tiles_per_batch = rows_per_batch // tm
    return pl.pallas_call(
        _mlp_kernel,
        out_shape=jax.ShapeDtypeStruct((t, d), F32),
        grid=(t // tm, dff // tf),
        in_specs=[pl.BlockSpec((tm, d), lambda i, f: (i, 0)),
                  pl.BlockSpec((1, N_MOD, d), lambda i, f: (i // tiles_per_batch, 0, 0)),
                  pl.BlockSpec((1, d), lambda i, f: (0, 0)),
                  pl.BlockSpec((1, d), lambda i, f: (0, 0)),
                  pl.BlockSpec((d, tf), lambda i, f: (0, f)),
                  pl.BlockSpec((tf, d), lambda i, f: (f, 0))],
        out_specs=pl.BlockSpec((tm, d), lambda i, f: (i, 0)),
        scratch_shapes=[pltpu.VMEM((tm, d), BF16), pltpu.VMEM((tm, d), F32)],
        compiler_params=pltpu.CompilerParams(dimension_semantics=("parallel", "arbitrary"),
                                             vmem_limit_bytes=VMEM_LIMIT),
        name="mlp",
    )(x1, mod, g_pre, g_post, w1, w2)


def _transposed_in_proj_weight(w_in):
    fox_end = 3 * FOX_HEADS * FOX_HEAD_DIM
    gla_start = fox_end + FOX_HEADS
    gla_end = gla_start + 2 * GLA_HEADS * GLA_DK + GLA_HEADS * GLA_DV
    gate_start = gla_end + GLA_GATE_RANK
    wt = jnp.swapaxes(w_in, 0, 1).astype(BF16)
    pad = jnp.zeros((LANE - FOX_HEADS - GLA_GATE_RANK, w_in.shape[0]), BF16)
    small = jnp.concatenate([wt[fox_end:gla_start], wt[gla_end:gate_start], pad], axis=0)
    return wt, small


def kernel(x, c, w_ada, b_ada, g_pre_mix, g_post_mix, w_in, b_fgate, w_gla_a2, b_gla_a2,
           g_fox_out, g_gla_out, w_out, g_pre_mlp, g_post_mlp, w_mlp_in, w_mlp_out):
    b, s, d = x.shape
    depth = w_ada.shape[0]
    row = lambda v: v.reshape(1, -1)
    for i in range(depth):
        c_pad = jnp.concatenate([c, jnp.zeros((8 - b, d), c.dtype)], axis=0)
        mod = _ada(c_pad, w_ada[i], row(b_ada[i]))[:b].reshape(b, N_MOD, d)

        wt, w_small = _transposed_in_proj_weight(w_in[i])
        x2 = x.reshape(b * s, d)
        proj, small = _in_proj(x2, mod, row(g_pre_mix[i]), wt, w_small, s)
        proj3 = proj.reshape(b, s, -1)
        small3 = small.reshape(b, s, LANE)

        fbias = jnp.concatenate([b_fgate[i], jnp.zeros((LANE - FOX_HEADS,), F32)]).reshape(1, LANE)
        fox = _fox_attn(proj3, _fox_cum(small3, fbias),
                        g_fox_out[i].reshape(FOX_HEADS, 1, FOX_HEAD_DIM))

        kw = GLA_HEADS * GLA_DK
        wa_pad = jnp.concatenate(
            [jnp.zeros((FOX_HEADS, kw), F32), w_gla_a2[i],
             jnp.zeros((LANE - FOX_HEADS - GLA_GATE_RANK, kw), F32)], axis=0).astype(BF16)
        gla, (w_out_b, w_mlp_in_b, w_mlp_out_b) = _gla(
            proj3, small3, wa_pad, row(b_gla_a2[i]), g_gla_out[i].reshape(GLA_HEADS, 1, GLA_DV),
            [w_out[i], w_mlp_in[i], w_mlp_out[i]])

        x1 = _out_proj(fox.reshape(b * s, -1), gla.reshape(b * s, -1), w_out_b,
                       x2, mod, row(g_post_mix[i]), s)
        x2 = _mlp(x1, mod, row(g_pre_mlp[i]), row(g_post_mlp[i]), w_mlp_in_b, w_mlp_out_b, s)
        x = x2.reshape(b, s, d)
    return x
```

```python
import functools

import jax
import jax.numpy as jnp
from jax import lax
from jax.experimental import pallas as pl
from jax.experimental.pallas import tpu as pltpu

F32 = jnp.float32
BF16 = jnp.bfloat16

EPS = 1e-6
N_MOD = 6
FOX_HEADS = 8
FOX_HEAD_DIM = 128
GLA_HEADS = 4
GLA_DK = 128
GLA_DV = 256
GLA_GATE_RANK = 16
GLA_GATE_TEMP = 16.0
CHUNK = 64
LANE = 128
BF16_ROWS = 16
VMEM_LIMIT = 56 * 1024 * 1024
VMEM_LIMIT_MLP = 60 * 1024 * 1024

NEG_BIG = -1e30
LOG2E = 1.4426950408889634
FOX_HEADS_PER_STEP = 4


def _nt_dot(a, b):
    return lax.dot_general(a, b, (((1,), (1,)), ((), ())), preferred_element_type=F32)


def _tn_dot(a, b):
    return lax.dot_general(a, b, (((0,), (0,)), ((), ())), preferred_element_type=F32)


def _log_sigmoid(z):
    return jnp.minimum(z, 0.0) - jnp.log(1.0 + jnp.exp(-jnp.abs(z)))


def _sigmoid(z):
    return 1.0 / (1.0 + jnp.exp(-z))


def _tri_cumsum(tri, v):
    hi = v.astype(BF16)
    lo = (v - hi.astype(F32)).astype(BF16)
    return (jnp.dot(tri, hi, preferred_element_type=F32)
            + jnp.dot(tri, lo, preferred_element_type=F32))


def _lower_tri(n, value=1.0):
    r = lax.broadcasted_iota(jnp.int32, (n, n), 0)
    c = lax.broadcasted_iota(jnp.int32, (n, n), 1)
    return jnp.where(r >= c, value, 0.0).astype(BF16)


def _ada_kernel(c_ref, w_ref, b_ref, o_ref):
    c = c_ref[...]
    act = (c * _sigmoid(c)).astype(BF16)
    o_ref[...] = jnp.dot(act, w_ref[...].astype(BF16), preferred_element_type=F32) + b_ref[...]


def _ada(c_pad, w_ada, b_ada, tn=1024):
    m, d = c_pad.shape
    n = w_ada.shape[1]
    return pl.pallas_call(
        _ada_kernel,
        out_shape=jax.ShapeDtypeStruct((m, n), F32),
        grid=(n // tn,),
        in_specs=[pl.BlockSpec((m, d), lambda j: (0, 0)),
                  pl.BlockSpec((d, tn), lambda j: (0, j)),
                  pl.BlockSpec((1, tn), lambda j: (0, j))],
        out_specs=pl.BlockSpec((m, tn), lambda j: (0, j)),
        compiler_params=pltpu.CompilerParams(dimension_semantics=("parallel",),
                                             vmem_limit_bytes=VMEM_LIMIT),
        name="ada",
    )(c_pad, w_ada, b_ada)


def _inproj_kernel(x_ref, mod_ref, g_ref, wlo_ref, whi_ref, ws_ref, o_ref, os_ref, h_scr):
    j = pl.program_id(1)
    half = wlo_ref.shape[0]

    @pl.when(j == 0)
    def _():
        x = x_ref[...]
        inv = lax.rsqrt(jnp.mean(x * x, axis=-1, keepdims=True) + EPS)
        gain = g_ref[...] * (1.0 + mod_ref[0, 1:2, :])
        hb = (x * inv * gain + mod_ref[0, 0:1, :]).astype(BF16)
        h_scr[...] = hb
        os_ref[...] = _nt_dot(hb, ws_ref[...])
        o_ref[:, :half] = (_nt_dot(hb, wlo_ref[...]) * (FOX_HEAD_DIM ** -0.5 * LOG2E)).astype(BF16)
        o_ref[:, half:] = _nt_dot(hb, whi_ref[...]).astype(BF16)

    @pl.when(j > 0)
    def _():
        o_ref[:, :half] = _nt_dot(h_scr[...], wlo_ref[...]).astype(BF16)
        o_ref[:, half:] = _nt_dot(h_scr[...], whi_ref[...]).astype(BF16)


def _in_proj(x2, mod, g, wt, w_small, rows_per_batch, tm=1024):
    t, d = x2.shape
    half = FOX_HEADS * FOX_HEAD_DIM
    n = wt.shape[0] - FOX_HEADS - GLA_GATE_RANK
    tn = 2 * half
    assert n % tn == 0 and GLA_HEADS * (2 * GLA_DK + GLA_DV) == tn
    tiles_per_batch = rows_per_batch // tm
    lo_start = lambda j: pl.multiple_of(j * tn + jnp.where(j >= 2, FOX_HEADS, 0), FOX_HEADS)
    hi_start = lambda j: pl.multiple_of(j * tn + half + jnp.where(j >= 1, FOX_HEADS, 0)
                                        + jnp.where(j >= 2, GLA_GATE_RANK, 0), FOX_HEADS)
    return pl.pallas_call(
        _inproj_kernel,
        out_shape=(jax.ShapeDtypeStruct((t, n), BF16),
                   jax.ShapeDtypeStruct((t, LANE), F32)),
        grid=(t // tm, n // tn),
        in_specs=[pl.BlockSpec((tm, d), lambda i, j: (i, 0)),
                  pl.BlockSpec((1, N_MOD, d), lambda i, j: (i // tiles_per_batch, 0, 0)),
                  pl.BlockSpec((1, d), lambda i, j: (0, 0)),
                  pl.BlockSpec((pl.Element(half), pl.Element(d)), lambda i, j: (lo_start(j), 0)),
                  pl.BlockSpec((pl.Element(half), pl.Element(d)), lambda i, j: (hi_start(j), 0)),
                  pl.BlockSpec((LANE, d), lambda i, j: (0, 0))],
        out_specs=(pl.BlockSpec((tm, tn), lambda i, j: (i, j)),
                   pl.BlockSpec((tm, LANE), lambda i, j: (i, 0))),
        scratch_shapes=[pltpu.VMEM((tm, d), BF16)],
        compiler_params=pltpu.CompilerParams(dimension_semantics=("parallel", "arbitrary"),
                                             vmem_limit_bytes=VMEM_LIMIT),
        name="in_proj",
    )(x2, mod, g, wt, wt, w_small)


def _cum_kernel(s_ref, b_ref, o_ref, *, blk):
    s = s_ref.shape[1]
    tri = _lower_tri(blk)
    col = lax.broadcasted_iota(jnp.int32, (blk, LANE), 1)
    carry = jnp.zeros((1, LANE), F32)
    for r in range(s // blk):
        rows = slice(r * blk, (r + 1) * blk)
        z = s_ref[0, rows, :] + b_ref[...]
        cs = _tri_cumsum(tri, _log_sigmoid(z)) + carry
        carry = cs[blk - 1:blk, :]
        c2 = jnp.where(col < FOX_HEADS, cs * LOG2E, 0.0)
        hi = c2.astype(BF16).astype(F32)
        rem = c2 - hi
        mid = rem.astype(BF16).astype(F32)
        lo = (rem - mid).astype(BF16).astype(F32)
        pieces = hi + pltpu.roll(mid, FOX_HEADS, 1) + pltpu.roll(lo, 2 * FOX_HEADS, 1)
        o_ref[0, rows, :] = pieces.astype(BF16)


def _fox_cum(small3, bias_row, blk=256):
    b, s, _ = small3.shape
    return pl.pallas_call(
        functools.partial(_cum_kernel, blk=blk),
        out_shape=jax.ShapeDtypeStruct((b, s, LANE), BF16),
        grid=(b,),
        in_specs=[pl.BlockSpec((1, s, LANE), lambda i: (i, 0, 0)),
                  pl.BlockSpec((1, LANE), lambda i: (0, 0))],
        out_specs=pl.BlockSpec((1, s, LANE), lambda i: (i, 0, 0)),
        compiler_params=pltpu.CompilerParams(dimension_semantics=("parallel",),
                                             vmem_limit_bytes=VMEM_LIMIT),
        name="fox_cum",
    )(small3, bias_row)


class _FoxHead:
    def __init__(self, ka, vt, sa, sb, m, l, acc):
        self.ka, self.vt, self.sa, self.sb, self.m, self.l, self.acc = ka, vt, sa, sb, m, l, acc


def _fox_kernel(q_ref, k_ref, v_ref, a_ref, g_ref, o_ref, *scratch, tq):
    heads = [_FoxHead(*scratch[n * 7:(n + 1) * 7]) for n in range(FOX_HEADS_PER_STEP)]
    t = pl.program_id(2)
    lanes = lambda n: slice(n * LANE, (n + 1) * LANE)

    @pl.when(t == 0)
    def _():
        for n, hd in enumerate(heads):
            hd.ka[:, 0:LANE] = k_ref[0, :, lanes(n)]
            hd.ka[:, LANE:] = a_ref[0]
            hd.vt[...] = v_ref[0, :, lanes(n)].T

    col = lax.broadcasted_iota(jnp.int32, (tq, LANE), 1)

    def tile_queries(half):
        rows = slice(half * tq, (half + 1) * tq)
        qas = []
        for n in range(len(heads)):
            h = pl.program_id(1) * FOX_HEADS_PER_STEP + n
            pick = (col == h) | (col == h + FOX_HEADS) | (col == h + 2 * FOX_HEADS)
            qas.append(jnp.concatenate([q_ref[0, rows, lanes(n)],
                                        jnp.where(pick, -1.0, 0.0).astype(BF16)], axis=1))
        return qas

    def reset_state():
        for hd in heads:
            hd.m[...] = jnp.full_like(hd.m, NEG_BIG)
            hd.l[...] = jnp.zeros_like(hd.l)
            hd.acc[...] = jnp.zeros_like(hd.acc)

    def logits_to(pick_buf, qas, blk):
        keys = pl.ds(pl.multiple_of(blk * tq, tq), tq)
        for hd, qa in zip(heads, qas):
            pick_buf(hd)[...] = _nt_dot(hd.ka[keys, :], qa)

    def absorb(pick_buf, blk, masked):
        keys = pl.ds(pl.multiple_of(blk * tq, tq), tq)
        for hd in heads:
            st = pick_buf(hd)[...]
            if masked:
                kr = lax.broadcasted_iota(jnp.int32, (tq, tq), 0)
                qc = lax.broadcasted_iota(jnp.int32, (tq, tq), 1)
                st = jnp.where(kr <= qc, st, NEG_BIG)
            m = hd.m[...]
            m_new = jnp.maximum(m, jnp.max(st, axis=0, keepdims=True))
            alpha = jnp.exp2(m - m_new)
            p = jnp.exp2(st - m_new)
            hd.m[...] = m_new
            hd.l[...] = alpha * hd.l[...] + jnp.sum(p, axis=0, keepdims=True)
            pv = jnp.dot(hd.vt[:, keys], p.astype(BF16), preferred_element_type=F32)
            hd.acc[...] = alpha * hd.acc[...] + pv

    def full_block_pairs(first, second, qas):
        def body(u, carry):
            logits_to(second, qas, 2 * u + 1)
            absorb(first, 2 * u, False)
            logits_to(first, qas, 2 * u + 2)
            absorb(second, 2 * u + 1, False)
            return carry
        lax.fori_loop(0, t, body, 0)

    def finish(half):
        for n, hd in enumerate(heads):
            ot = hd.acc[...] / hd.l[...]
            ot = ot * lax.rsqrt(jnp.mean(ot * ot, axis=0, keepdims=True) + EPS)
            o_ref[0, half * tq:(half + 1) * tq, lanes(n)] = (ot.T * g_ref[n]).astype(BF16)

    buf_a = lambda hd: hd.sa
    buf_b = lambda hd: hd.sb
    q_even, q_odd = tile_queries(0), tile_queries(1)

    reset_state()
    logits_to(buf_a, q_even, 0)
    full_block_pairs(buf_a, buf_b, q_even)
    logits_to(buf_b, q_odd, 0)
    absorb(buf_a, 2 * t, True)
    finish(0)

    reset_state()
    full_block_pairs(buf_b, buf_a, q_odd)
    logits_to(buf_a, q_odd, 2 * t + 1)
    absorb(buf_b, 2 * t, False)
    absorb(buf_a, 2 * t + 1, True)
    finish(1)


def _fox_attn(proj3, pieces3, g_fox, tq=512):
    b, s, _ = proj3.shape
    nh = FOX_HEADS_PER_STEP
    groups = FOX_HEADS // nh
    width = nh * FOX_HEAD_DIM
    per_head = [pltpu.VMEM((s, 2 * LANE), BF16), pltpu.VMEM((FOX_HEAD_DIM, s), BF16),
                pltpu.VMEM((tq, tq), F32), pltpu.VMEM((tq, tq), F32),
                pltpu.VMEM((1, tq), F32), pltpu.VMEM((1, tq), F32),
                pltpu.VMEM((FOX_HEAD_DIM, tq), F32)]
    return pl.pallas_call(
        functools.partial(_fox_kernel, tq=tq),
        out_shape=jax.ShapeDtypeStruct((b, s, FOX_HEADS * FOX_HEAD_DIM), BF16),
        grid=(b, groups, s // (2 * tq)),
        in_specs=[pl.BlockSpec((1, 2 * tq, width), lambda bi, gi, ti: (bi, ti, gi)),
                  pl.BlockSpec((1, s, width), lambda bi, gi, ti: (bi, 0, groups + gi)),
                  pl.BlockSpec((1, s, width), lambda bi, gi, ti: (bi, 0, 2 * groups + gi)),
                  pl.BlockSpec((1, s, LANE), lambda bi, gi, ti: (bi, 0, 0)),
                  pl.BlockSpec((nh, 1, LANE), lambda bi, gi, ti: (gi, 0, 0))],
        out_specs=pl.BlockSpec((1, 2 * tq, width), lambda bi, gi, ti: (bi, ti, gi)),
        scratch_shapes=per_head * nh,
        compiler_params=pltpu.CompilerParams(
            dimension_semantics=("parallel", "parallel", "arbitrary"),
            vmem_limit_bytes=VMEM_LIMIT),
        name="fox_attn",
    )(proj3, proj3, proj3, pieces3, g_fox)


def _gla_kernel(q_ref, k_ref, v_ref, r_ref, s_ref, wa_ref, ba_ref, gg_ref, *refs, ts, n_cast):
    o_ref, st_ref = refs[n_cast], refs[-1]
    for src, dst in zip(refs[:n_cast], refs[n_cast + 1:2 * n_cast + 1]):
        dst[...] = src[...].astype(BF16)

    @pl.when(pl.program_id(2) == 0)
    def _():
        st_ref[...] = jnp.zeros_like(st_ref)

    pre = jnp.dot(s_ref[0].astype(BF16), wa_ref[...], preferred_element_type=F32) + ba_ref[...]
    la = _log_sigmoid(pre)
    tri = _lower_tri(CHUNK, 1.0 / GLA_GATE_TEMP)
    chunks = [slice(c * CHUNK, (c + 1) * CHUNK) for c in range(ts // CHUNK)]
    cums = [_tri_cumsum(tri, la[sl]) for sl in chunks]
    tots = [cum[CHUNK - 1:CHUNK, :] for cum in cums]
    ups = []
    for sl, cum, tot in zip(chunks, cums, tots):
        kd = (k_ref[0, sl, :].astype(F32) * jnp.exp(tot - cum)).astype(BF16)
        ups.append(_tn_dot(v_ref[0, sl, :], kd))
    st = st_ref[...]
    states = []
    for tot, up in zip(tots, ups):
        st = st * jnp.exp(tot) + up
        states.append(st.astype(BF16))
    st_ref[...] = st
    outs = [_nt_dot(q_ref[0, sl, :], sb) for sl, sb in zip(chunks, states)]
    o = jnp.concatenate(outs, axis=0)
    qs = GLA_DK ** -0.5
    inv = qs * lax.rsqrt(qs * qs * jnp.mean(o * o, axis=-1, keepdims=True) + EPS)
    o = o * inv * gg_ref[0]
    r = r_ref[0].astype(F32)
    o_ref[0] = (o * (r * _sigmoid(r))).astype(BF16)


def _gla(proj3, small3, wa_pad, ba_row, g_gla, f32_weights, ts=1024):
    b, s, _ = proj3.shape
    q_blk = 3 * FOX_HEADS
    k_blk = q_blk + GLA_HEADS
    v_blk = (k_blk + GLA_HEADS) // 2
    r_blk = v_blk + GLA_HEADS
    nt = s // ts
    n_steps = b * GLA_HEADS * nt
    step = lambda bi, gi, ti: ((bi * GLA_HEADS + gi) * nt + ti, 0)
    slabs = [pl.BlockSpec((w.shape[0] // n_steps, w.shape[1]), step) for w in f32_weights]
    assert all(w.shape[0] % (n_steps * BF16_ROWS) == 0 for w in f32_weights)
    outs = pl.pallas_call(
        functools.partial(_gla_kernel, ts=ts, n_cast=len(f32_weights)),
        out_shape=[jax.ShapeDtypeStruct((b, s, GLA_HEADS * GLA_DV), BF16)]
        + [jax.ShapeDtypeStruct(w.shape, BF16) for w in f32_weights],
        grid=(b, GLA_HEADS, nt),
        in_specs=[pl.BlockSpec((1, ts, GLA_DK), lambda bi, gi, ti: (bi, ti, q_blk + gi)),
                  pl.BlockSpec((1, ts, GLA_DK), lambda bi, gi, ti: (bi, ti, k_blk + gi)),
                  pl.BlockSpec((1, ts, GLA_DV), lambda bi, gi, ti: (bi, ti, v_blk + gi)),
                  pl.BlockSpec((1, ts, GLA_DV), lambda bi, gi, ti: (bi, ti, r_blk + gi)),
                  pl.BlockSpec((1, ts, LANE), lambda bi, gi, ti: (bi, ti, 0)),
                  pl.BlockSpec((LANE, GLA_DK), lambda bi, gi, ti: (0, gi)),
                  pl.BlockSpec((1, GLA_DK), lambda bi, gi, ti: (0, gi)),
                  pl.BlockSpec((1, 1, GLA_DV), lambda bi, gi, ti: (gi, 0, 0))] + slabs,
        out_specs=[pl.BlockSpec((1, ts, GLA_DV), lambda bi, gi, ti: (bi, ti, gi))] + slabs,
        scratch_shapes=[pltpu.VMEM((GLA_DV, GLA_DK), F32)],
        compiler_params=pltpu.CompilerParams(
            dimension_semantics=("parallel", "parallel", "arbitrary"),
            vmem_limit_bytes=VMEM_LIMIT),
        name="gla",
    )(proj3, proj3, proj3, proj3, small3, wa_pad, ba_row, g_gla, *f32_weights)
    return outs[0], outs[1:]


def _outproj_kernel(fox_ref, gla_ref, w_ref, x_ref, mod_ref, g_ref, o_ref):
    half = fox_ref.shape[1]
    y = (jnp.dot(fox_ref[...], w_ref[0:half, :], preferred_element_type=F32)
         + jnp.dot(gla_ref[...], w_ref[half:, :], preferred_element_type=F32))
    inv = lax.rsqrt(jnp.mean(y * y, axis=-1, keepdims=True) + EPS)
    o_ref[...] = x_ref[...] + y * inv * (mod_ref[0, 2:3, :] * g_ref[...])


def _out_proj(fox2, gla2, w_out, x2, mod, g, rows_per_batch, tm=512):
    t, d = x2.shape
    half = fox2.shape[1]
    tiles_per_batch = rows_per_batch // tm
    return pl.pallas_call(
        _outproj_kernel,
        out_shape=jax.ShapeDtypeStruct((t, d), F32),
        grid=(t // tm,),
        in_specs=[pl.BlockSpec((tm, half), lambda i: (i, 0)),
                  pl.BlockSpec((tm, half), lambda i: (i, 0)),
                  pl.BlockSpec((2 * half, d), lambda i: (0, 0)),
                  pl.BlockSpec((tm, d), lambda i: (i, 0)),
                  pl.BlockSpec((1, N_MOD, d), lambda i: (i // tiles_per_batch, 0, 0)),
                  pl.BlockSpec((1, d), lambda i: (0, 0))],
        out_specs=pl.BlockSpec((tm, d), lambda i: (i, 0)),
        compiler_params=pltpu.CompilerParams(dimension_semantics=("parallel",),
                                             vmem_limit_bytes=VMEM_LIMIT),
        name="out_proj",
    )(fox2, gla2, w_out, x2, mod, g)


def _mlp_kernel(x_ref, mod_ref, gpre_ref, gpost_ref, w1_ref, w2_ref, o_ref, h_scr):
    f = pl.program_id(1)
    last = pl.num_programs(1) - 1
    half = x_ref.shape[0] // 2
    halves = (slice(0, half), slice(half, 2 * half))

    def hidden_block(hb):
        u = jnp.maximum(jnp.dot(hb, w1_ref[...], preferred_element_type=F32), 0.0)
        return jnp.dot((u * u).astype(BF16), w2_ref[...], preferred_element_type=F32)

    @pl.when(f == 0)
    def _():
        gain = gpre_ref[...] * (1.0 + mod_ref[0, 4:5, :])
        for rows in halves:
            x = x_ref[rows, :]
            inv = lax.rsqrt(jnp.mean(x * x, axis=-1, keepdims=True) + EPS)
            hb = (x * inv * gain + mod_ref[0, 3:4, :]).astype(BF16)
            h_scr[rows, :] = hb
            o_ref[rows, :] = hidden_block(hb)

    @pl.when((f > 0) & (f < last))
    def _():
        for rows in halves:
            o_ref[rows, :] += hidden_block(h_scr[rows, :])

    @pl.when(f == last)
    def _():
        gain = mod_ref[0, 5:6, :] * gpost_ref[...]
        for rows in halves:
            y = o_ref[rows, :] + hidden_block(h_scr[rows, :])
            inv = lax.rsqrt(jnp.mean(y * y, axis=-1, keepdims=True) + EPS)
            o_ref[rows, :] = x_ref[rows, :] + y * inv * gain


def _mlp(x1, mod, g_pre, g_post, w1, w2, rows_per_batch, tm=1024, tf=1024):
    t, d = x1.shape
    dff = w1.shape[1]
    tiles_per_batch = rows_per_batch // tm
    return pl.pallas_call(
        _mlp_kernel,
        out_shape=jax.ShapeDtypeStruct((t, d), F32),
        grid=(t // tm, dff // tf),
        in_specs=[pl.BlockSpec((tm, d), lambda i, f: (i, 0)),
                  pl.BlockSpec((1, N_MOD, d), lambda i, f: (i // tiles_per_batch, 0, 0)),
                  pl.BlockSpec((1, d), lambda i, f: (0, 0)),
                  pl.BlockSpec((1, d), lambda i, f: (0, 0)),
                  pl.BlockSpec((d, tf), lambda i, f: (0, f)),
                  pl.BlockSpec((tf, d), lambda i, f: (f, 0))],
        out_specs=pl.BlockSpec((tm, d), lambda i, f: (i, 0)),
        scratch_shapes=[pltpu.VMEM((tm, d), BF16)],
        compiler_params=pltpu.CompilerParams(dimension_semantics=("parallel", "arbitrary"),
                                             vmem_limit_bytes=VMEM_LIMIT_MLP),
        name="mlp",
    )(x1, mod, g_pre, g_post, w1, w2)


def _transposed_in_proj_weight(w_in):
    fox_end = 3 * FOX_HEADS * FOX_HEAD_DIM
    gla_start = fox_end + FOX_HEADS
    gla_end = gla_start + 2 * GLA_HEADS * GLA_DK + GLA_HEADS * GLA_DV
    gate_start = gla_end + GLA_GATE_RANK
    wt = jnp.swapaxes(w_in, 0, 1).astype(BF16)
    pad = jnp.zeros((LANE - FOX_HEADS - GLA_GATE_RANK, w_in.shape[0]), BF16)
    small = jnp.concatenate([wt[fox_end:gla_start], wt[gla_end:gate_start], pad], axis=0)
    return wt, small


def kernel(x, c, w_ada, b_ada, g_pre_mix, g_post_mix, w_in, b_fgate, w_gla_a2, b_gla_a2,
           g_fox_out, g_gla_out, w_out, g_pre_mlp, g_post_mlp, w_mlp_in, w_mlp_out):
    b, s, d = x.shape
    depth = w_ada.shape[0]
    row = lambda v: v.reshape(1, -1)
    for i in range(depth):
        c_pad = jnp.concatenate([c, jnp.zeros((8 - b, d), c.dtype)], axis=0)
        mod = _ada(c_pad, w_ada[i], row(b_ada[i]))[:b].reshape(b, N_MOD, d)

        wt, w_small = _transposed_in_proj_weight(w_in[i])
        x2 = x.reshape(b * s, d)
        proj, small = _in_proj(x2, mod, row(g_pre_mix[i]), wt, w_small, s)
        proj3 = proj.reshape(b, s, -1)
        small3 = small.reshape(b, s, LANE)

        fbias = jnp.concatenate([b_fgate[i], jnp.zeros((LANE - FOX_HEADS,), F32)]).reshape(1, LANE)
        fox = _fox_attn(proj3, _fox_cum(small3, fbias),
                        g_fox_out[i].reshape(FOX_HEADS, 1, FOX_HEAD_DIM))

        kw = GLA_HEADS * GLA_DK
        wa_pad = jnp.concatenate(
            [jnp.zeros((FOX_HEADS, kw), F32), w_gla_a2[i],
             jnp.zeros((LANE - FOX_HEADS - GLA_GATE_RANK, kw), F32)], axis=0).astype(BF16)
        gla, (w_out_b, w_mlp_in_b, w_mlp_out_b) = _gla(
            proj3, small3, wa_pad, row(b_gla_a2[i]), g_gla_out[i].reshape(GLA_HEADS, 1, GLA_DV),
            [w_out[i], w_mlp_in[i], w_mlp_out[i]])

        x1 = _out_proj(fox.reshape(b * s, -1), gla.reshape(b * s, -1), w_out_b,
                       x2, mod, row(g_post_mix[i]), s)
        x2 = _mlp(x1, mod, row(g_pre_mlp[i]), row(g_post_mlp[i]), w_mlp_in_b, w_mlp_out_b, s)
        x = x2.reshape(b, s, d)
    return x
```

```python
import functools

import jax
import jax.numpy as jnp
from jax import lax
from jax.experimental import pallas as pl
from jax.experimental.pallas import tpu as pltpu

F32 = jnp.float32
BF16 = jnp.bfloat16

EPS = 1e-6
N_MOD = 6
FOX_HEADS = 8
FOX_HEAD_DIM = 128
GLA_HEADS = 4
GLA_DK = 128
GLA_DV = 256
GLA_GATE_RANK = 16
GLA_GATE_TEMP = 16.0
CHUNK = 64
LANE = 128
BF16_ROWS = 16
VMEM_LIMIT = 56 * 1024 * 1024
VMEM_LIMIT_LARGE = 60 * 1024 * 1024

NEG_BIG = -1e30
LOG2E = 1.4426950408889634
FOX_HEADS_PER_STEP = 4


def _nt_dot(a, b):
    return lax.dot_general(a, b, (((1,), (1,)), ((), ())), preferred_element_type=F32)


def _tn_dot(a, b):
    return lax.dot_general(a, b, (((0,), (0,)), ((), ())), preferred_element_type=F32)


def _log_sigmoid(z):
    return jnp.minimum(z, 0.0) - jnp.log(1.0 + jnp.exp(-jnp.abs(z)))


def _sigmoid(z):
    return 1.0 / (1.0 + jnp.exp(-z))


def _tri_cumsum(tri, v):
    hi = v.astype(BF16)
    lo = (v - hi.astype(F32)).astype(BF16)
    return (jnp.dot(tri, hi, preferred_element_type=F32)
            + jnp.dot(tri, lo, preferred_element_type=F32))


def _lower_tri(n, value=1.0):
    r = lax.broadcasted_iota(jnp.int32, (n, n), 0)
    c = lax.broadcasted_iota(jnp.int32, (n, n), 1)
    return jnp.where(r >= c, value, 0.0).astype(BF16)


def _ada_kernel(c_ref, w_ref, b_ref, o_ref):
    c = c_ref[...]
    act = (c * _sigmoid(c)).astype(BF16)
    o_ref[...] = jnp.dot(act, w_ref[...].astype(BF16), preferred_element_type=F32) + b_ref[...]


def _ada(c_pad, w_ada, b_ada, tn=1024):
    m, d = c_pad.shape
    n = w_ada.shape[1]
    return pl.pallas_call(
        _ada_kernel,
        out_shape=jax.ShapeDtypeStruct((m, n), F32),
        grid=(n // tn,),
        in_specs=[pl.BlockSpec((m, d), lambda j: (0, 0)),
                  pl.BlockSpec((d, tn), lambda j: (0, j)),
                  pl.BlockSpec((1, tn), lambda j: (0, j))],
        out_specs=pl.BlockSpec((m, tn), lambda j: (0, j)),
        compiler_params=pltpu.CompilerParams(dimension_semantics=("parallel",),
                                             vmem_limit_bytes=VMEM_LIMIT),
        name="ada",
    )(c_pad, w_ada, b_ada)


def _inproj_kernel(x_ref, mod_ref, g_ref, wlo_ref, whi_ref, ws_ref, o_ref, os_ref, h_scr):
    j = pl.program_id(1)
    half = wlo_ref.shape[0]

    @pl.when(j == 0)
    def _():
        x = x_ref[...]
        inv = lax.rsqrt(jnp.mean(x * x, axis=-1, keepdims=True) + EPS)
        gain = g_ref[...] * (1.0 + mod_ref[0, 1:2, :])
        hb = (x * inv * gain + mod_ref[0, 0:1, :]).astype(BF16)
        h_scr[...] = hb
        os_ref[...] = _nt_dot(hb, ws_ref[...])
        o_ref[:, :half] = (_nt_dot(hb, wlo_ref[...]) * (FOX_HEAD_DIM ** -0.5 * LOG2E)).astype(BF16)
        o_ref[:, half:] = _nt_dot(hb, whi_ref[...]).astype(BF16)

    @pl.when(j > 0)
    def _():
        o_ref[:, :half] = _nt_dot(h_scr[...], wlo_ref[...]).astype(BF16)
        o_ref[:, half:] = _nt_dot(h_scr[...], whi_ref[...]).astype(BF16)


def _in_proj(x2, mod, g, wt, w_small, rows_per_batch, tm=1024):
    t, d = x2.shape
    half = FOX_HEADS * FOX_HEAD_DIM
    n = wt.shape[0] - FOX_HEADS - GLA_GATE_RANK
    tn = 2 * half
    assert n % tn == 0 and GLA_HEADS * (2 * GLA_DK + GLA_DV) == tn
    tiles_per_batch = rows_per_batch // tm
    lo_start = lambda j: pl.multiple_of(j * tn + jnp.where(j >= 2, FOX_HEADS, 0), FOX_HEADS)
    hi_start = lambda j: pl.multiple_of(j * tn + half + jnp.where(j >= 1, FOX_HEADS, 0)
                                        + jnp.where(j >= 2, GLA_GATE_RANK, 0), FOX_HEADS)
    return pl.pallas_call(
        _inproj_kernel,
        out_shape=(jax.ShapeDtypeStruct((t, n), BF16),
                   jax.ShapeDtypeStruct((t, LANE), F32)),
        grid=(t // tm, n // tn),
        in_specs=[pl.BlockSpec((tm, d), lambda i, j: (i, 0)),
                  pl.BlockSpec((1, N_MOD, d), lambda i, j: (i // tiles_per_batch, 0, 0)),
                  pl.BlockSpec((1, d), lambda i, j: (0, 0)),
                  pl.BlockSpec((pl.Element(half), pl.Element(d)), lambda i, j: (lo_start(j), 0)),
                  pl.BlockSpec((pl.Element(half), pl.Element(d)), lambda i, j: (hi_start(j), 0)),
                  pl.BlockSpec((LANE, d), lambda i, j: (0, 0))],
        out_specs=(pl.BlockSpec((tm, tn), lambda i, j: (i, j)),
                   pl.BlockSpec((tm, LANE), lambda i, j: (i, 0))),
        scratch_shapes=[pltpu.VMEM((tm, d), BF16)],
        compiler_params=pltpu.CompilerParams(dimension_semantics=("parallel", "arbitrary"),
                                             vmem_limit_bytes=VMEM_LIMIT),
        name="in_proj",
    )(x2, mod, g, wt, wt, w_small)


def _cum_kernel(s_ref, b_ref, o_ref, *, blk):
    s = s_ref.shape[1]
    tri = _lower_tri(blk)
    col = lax.broadcasted_iota(jnp.int32, (blk, LANE), 1)
    carry = jnp.zeros((1, LANE), F32)
    for r in range(s // blk):
        rows = slice(r * blk, (r + 1) * blk)
        z = s_ref[0, rows, :] + b_ref[...]
        cs = _tri_cumsum(tri, _log_sigmoid(z)) + carry
        carry = cs[blk - 1:blk, :]
        c2 = jnp.where(col < FOX_HEADS, cs * LOG2E, 0.0)
        hi = c2.astype(BF16).astype(F32)
        rem = c2 - hi
        mid = rem.astype(BF16).astype(F32)
        lo = (rem - mid).astype(BF16).astype(F32)
        pieces = hi + pltpu.roll(mid, FOX_HEADS, 1) + pltpu.roll(lo, 2 * FOX_HEADS, 1)
        o_ref[0, rows, :] = pieces.astype(BF16)


def _fox_cum(small3, bias_row, blk=256):
    b, s, _ = small3.shape
    return pl.pallas_call(
        functools.partial(_cum_kernel, blk=blk),
        out_shape=jax.ShapeDtypeStruct((b, s, LANE), BF16),
        grid=(b,),
        in_specs=[pl.BlockSpec((1, s, LANE), lambda i: (i, 0, 0)),
                  pl.BlockSpec((1, LANE), lambda i: (0, 0))],
        out_specs=pl.BlockSpec((1, s, LANE), lambda i: (i, 0, 0)),
        compiler_params=pltpu.CompilerParams(dimension_semantics=("parallel",),
                                             vmem_limit_bytes=VMEM_LIMIT),
        name="fox_cum",
    )(small3, bias_row)


class _FoxHead:
    def __init__(self, ka, vt, sa, sb, m, l, acc):
        self.ka, self.vt, self.sa, self.sb, self.m, self.l, self.acc = ka, vt, sa, sb, m, l, acc


def _fox_kernel(q_ref, k_ref, v_ref, a_ref, g_ref, *refs, tq, n_cast):
    o_ref = refs[n_cast]
    scratch = refs[2 * n_cast + 1:]
    heads = [_FoxHead(*scratch[n * 7:(n + 1) * 7]) for n in range(FOX_HEADS_PER_STEP)]
    t = pl.program_id(2)
    lanes = lambda n: slice(n * LANE, (n + 1) * LANE)

    @pl.when(t == 0)
    def _():
        for n, hd in enumerate(heads):
            hd.ka[:, 0:LANE] = k_ref[0, :, lanes(n)]
            hd.ka[:, LANE:] = a_ref[0]
            hd.vt[...] = v_ref[0, :, lanes(n)].T

    for src, dst in zip(refs[:n_cast], refs[n_cast + 1:2 * n_cast + 1]):
        dst[...] = src[...].astype(BF16)

    col = lax.broadcasted_iota(jnp.int32, (tq, LANE), 1)

    def tile_queries(half):
        rows = slice(half * tq, (half + 1) * tq)
        qas = []
        for n in range(len(heads)):
            h = pl.program_id(1) * FOX_HEADS_PER_STEP + n
            pick = (col == h) | (col == h + FOX_HEADS) | (col == h + 2 * FOX_HEADS)
            qas.append(jnp.concatenate([q_ref[0, rows, lanes(n)],
                                        jnp.where(pick, -1.0, 0.0).astype(BF16)], axis=1))
        return qas

    def reset_state():
        for hd in heads:
            hd.m[...] = jnp.full_like(hd.m, NEG_BIG)
            hd.l[...] = jnp.zeros_like(hd.l)
            hd.acc[...] = jnp.zeros_like(hd.acc)

    def logits_to(pick_buf, qas, blk):
        keys = pl.ds(pl.multiple_of(blk * tq, tq), tq)
        for hd, qa in zip(heads, qas):
            pick_buf(hd)[...] = _nt_dot(hd.ka[keys, :], qa)

    def absorb(pick_buf, blk, masked):
        keys = pl.ds(pl.multiple_of(blk * tq, tq), tq)
        for hd in heads:
            st = pick_buf(hd)[...]
            if masked:
                kr = lax.broadcasted_iota(jnp.int32, (tq, tq), 0)
                qc = lax.broadcasted_iota(jnp.int32, (tq, tq), 1)
                st = jnp.where(kr <= qc, st, NEG_BIG)
            m = hd.m[...]
            m_new = jnp.maximum(m, jnp.max(st, axis=0, keepdims=True))
            alpha = jnp.exp2(m - m_new)
            p = jnp.exp2(st - m_new)
            hd.m[...] = m_new
            hd.l[...] = alpha * hd.l[...] + jnp.sum(p, axis=0, keepdims=True)
            pv = jnp.dot(hd.vt[:, keys], p.astype(BF16), preferred_element_type=F32)
            hd.acc[...] = alpha * hd.acc[...] + pv

    def full_block_pairs(first, second, qas):
        def body(u, carry):
            logits_to(second, qas, 2 * u + 1)
            absorb(first, 2 * u, False)
            logits_to(first, qas, 2 * u + 2)
            absorb(second, 2 * u + 1, False)
            return carry
        lax.fori_loop(0, t, body, 0)

    def finish(half):
        for n, hd in enumerate(heads):
            ot = hd.acc[...] / hd.l[...]
            ot = ot * lax.rsqrt(jnp.mean(ot * ot, axis=0, keepdims=True) + EPS)
            o_ref[0, half * tq:(half + 1) * tq, lanes(n)] = (ot.T * g_ref[n]).astype(BF16)

    buf_a = lambda hd: hd.sa
    buf_b = lambda hd: hd.sb
    q_even, q_odd = tile_queries(0), tile_queries(1)

    reset_state()
    logits_to(buf_a, q_even, 0)
    full_block_pairs(buf_a, buf_b, q_even)
    logits_to(buf_b, q_odd, 0)
    absorb(buf_a, 2 * t, True)
    finish(0)

    reset_state()
    full_block_pairs(buf_b, buf_a, q_odd)
    logits_to(buf_a, q_odd, 2 * t + 1)
    absorb(buf_b, 2 * t, False)
    absorb(buf_a, 2 * t + 1, True)
    finish(1)


def _cast_slabs(f32_weights, n_steps, step_index):
    assert all(w.shape[0] % (n_steps * BF16_ROWS) == 0 for w in f32_weights)
    return [pl.BlockSpec((w.shape[0] // n_steps, w.shape[1]), step_index) for w in f32_weights]


def _fox_attn(proj3, pieces3, g_fox, f32_weights, tq=512):
    b, s, _ = proj3.shape
    nh = FOX_HEADS_PER_STEP
    groups = FOX_HEADS // nh
    width = nh * FOX_HEAD_DIM
    nt = s // (2 * tq)
    slabs = _cast_slabs(f32_weights, b * groups * nt,
                        lambda bi, gi, ti: ((bi * groups + gi) * nt + ti, 0))
    per_head = [pltpu.VMEM((s, 2 * LANE), BF16), pltpu.VMEM((FOX_HEAD_DIM, s), BF16),
                pltpu.VMEM((tq, tq), F32), pltpu.VMEM((tq, tq), F32),
                pltpu.VMEM((1, tq), F32), pltpu.VMEM((1, tq), F32),
                pltpu.VMEM((FOX_HEAD_DIM, tq), F32)]
    outs = pl.pallas_call(
        functools.partial(_fox_kernel, tq=tq, n_cast=len(f32_weights)),
        out_shape=[jax.ShapeDtypeStruct((b, s, FOX_HEADS * FOX_HEAD_DIM), BF16)]
        + [jax.ShapeDtypeStruct(w.shape, BF16) for w in f32_weights],
        grid=(b, groups, nt),
        in_specs=[pl.BlockSpec((1, 2 * tq, width), lambda bi, gi, ti: (bi, ti, gi)),
                  pl.BlockSpec((1, s, width), lambda bi, gi, ti: (bi, 0, groups + gi)),
                  pl.BlockSpec((1, s, width), lambda bi, gi, ti: (bi, 0, 2 * groups + gi)),
                  pl.BlockSpec((1, s, LANE), lambda bi, gi, ti: (bi, 0, 0)),
                  pl.BlockSpec((nh, 1, LANE), lambda bi, gi, ti: (gi, 0, 0))] + slabs,
        out_specs=[pl.BlockSpec((1, 2 * tq, width), lambda bi, gi, ti: (bi, ti, gi))] + slabs,
        scratch_shapes=per_head * nh,
        compiler_params=pltpu.CompilerParams(
            dimension_semantics=("parallel", "parallel", "arbitrary"),
            vmem_limit_bytes=VMEM_LIMIT_LARGE),
        name="fox_attn",
    )(proj3, proj3, proj3, pieces3, g_fox, *f32_weights)
    return outs[0], outs[1:]


def _gla_kernel(q_ref, k_ref, v_ref, r_ref, s_ref, wa_ref, ba_ref, gg_ref, *refs, ts, n_cast):
    o_ref, st_ref = refs[n_cast], refs[-1]
    for src, dst in zip(refs[:n_cast], refs[n_cast + 1:2 * n_cast + 1]):
        dst[...] = src[...].astype(BF16)

    @pl.when(pl.program_id(2) == 0)
    def _():
        st_ref[...] = jnp.zeros_like(st_ref)

    pre = jnp.dot(s_ref[0].astype(BF16), wa_ref[...], preferred_element_type=F32) + ba_ref[...]
    la = _log_sigmoid(pre)
    tri = _lower_tri(CHUNK, 1.0 / GLA_GATE_TEMP)
    chunks = [slice(c * CHUNK, (c + 1) * CHUNK) for c in range(ts // CHUNK)]
    cums = [_tri_cumsum(tri, la[sl]) for sl in chunks]
    tots = [cum[CHUNK - 1:CHUNK, :] for cum in cums]
    ups = []
    for sl, cum, tot in zip(chunks, cums, tots):
        kd = (k_ref[0, sl, :].astype(F32) * jnp.exp(tot - cum)).astype(BF16)
        ups.append(_tn_dot(v_ref[0, sl, :], kd))
    st = st_ref[...]
    states = []
    for tot, up in zip(tots, ups):
        st = st * jnp.exp(tot) + up
        states.append(st.astype(BF16))
    st_ref[...] = st
    outs = [_nt_dot(q_ref[0, sl, :], sb) for sl, sb in zip(chunks, states)]
    o = jnp.concatenate(outs, axis=0)
    qs = GLA_DK ** -0.5
    inv = qs * lax.rsqrt(qs * qs * jnp.mean(o * o, axis=-1, keepdims=True) + EPS)
    o = o * inv * gg_ref[0]
    r = r_ref[0].astype(F32)
    o_ref[0] = (o * (r * _sigmoid(r))).astype(BF16)


def _gla(proj3, small3, wa_pad, ba_row, g_gla, f32_weights, ts=1024):
    b, s, _ = proj3.shape
    q_blk = 3 * FOX_HEADS
    k_blk = q_blk + GLA_HEADS
    v_blk = (k_blk + GLA_HEADS) // 2
    r_blk = v_blk + GLA_HEADS
    nt = s // ts
    slabs = _cast_slabs(f32_weights, b * GLA_HEADS * nt,
                        lambda bi, gi, ti: ((bi * GLA_HEADS + gi) * nt + ti, 0))
    outs = pl.pallas_call(
        functools.partial(_gla_kernel, ts=ts, n_cast=len(f32_weights)),
        out_shape=[jax.ShapeDtypeStruct((b, s, GLA_HEADS * GLA_DV), BF16)]
        + [jax.ShapeDtypeStruct(w.shape, BF16) for w in f32_weights],
        grid=(b, GLA_HEADS, nt),
        in_specs=[pl.BlockSpec((1, ts, GLA_DK), lambda bi, gi, ti: (bi, ti, q_blk + gi)),
                  pl.BlockSpec((1, ts, GLA_DK), lambda bi, gi, ti: (bi, ti, k_blk + gi)),
                  pl.BlockSpec((1, ts, GLA_DV), lambda bi, gi, ti: (bi, ti, v_blk + gi)),
                  pl.BlockSpec((1, ts, GLA_DV), lambda bi, gi, ti: (bi, ti, r_blk + gi)),
                  pl.BlockSpec((1, ts, LANE), lambda bi, gi, ti: (bi, ti, 0)),
                  pl.BlockSpec((LANE, GLA_DK), lambda bi, gi, ti: (0, gi)),
                  pl.BlockSpec((1, GLA_DK), lambda bi, gi, ti: (0, gi)),
                  pl.BlockSpec((1, 1, GLA_DV), lambda bi, gi, ti: (gi, 0, 0))] + slabs,
        out_specs=[pl.BlockSpec((1, ts, GLA_DV), lambda bi, gi, ti: (bi, ti, gi))] + slabs,
        scratch_shapes=[pltpu.VMEM((GLA_DV, GLA_DK), F32)],
        compiler_params=pltpu.CompilerParams(
            dimension_semantics=("parallel", "parallel", "arbitrary"),
            vmem_limit_bytes=VMEM_LIMIT),
        name="gla",
    )(proj3, proj3, proj3, proj3, small3, wa_pad, ba_row, g_gla, *f32_weights)
    return outs[0], outs[1:]


def _outproj_kernel(fox_ref, gla_ref, w_ref, x_ref, mod_ref, g_ref, o_ref):
    wide = fox_ref.shape[1]
    gain = mod_ref[0, 2:3, :] * g_ref[...]
    half = x_ref.shape[0] // 2
    for rows in (slice(0, half), slice(half, 2 * half)):
        y = (jnp.dot(fox_ref[rows, :], w_ref[0:wide, :], preferred_element_type=F32)
             + jnp.dot(gla_ref[rows, :], w_ref[wide:, :], preferred_element_type=F32))
        inv = lax.rsqrt(jnp.mean(y * y, axis=-1, keepdims=True) + EPS)
        o_ref[rows, :] = x_ref[rows, :] + y * inv * gain


def _out_proj(fox2, gla2, w_out, x2, mod, g, rows_per_batch, tm=1024):
    t, d = x2.shape
    wide = fox2.shape[1]
    tiles_per_batch = rows_per_batch // tm
    return pl.pallas_call(
        _outproj_kernel,
        out_shape=jax.ShapeDtypeStruct((t, d), F32),
        grid=(t // tm,),
        in_specs=[pl.BlockSpec((tm, wide), lambda i: (i, 0)),
                  pl.BlockSpec((tm, wide), lambda i: (i, 0)),
                  pl.BlockSpec((2 * wide, d), lambda i: (0, 0), pipeline_mode=pl.Buffered(1)),
                  pl.BlockSpec((tm, d), lambda i: (i, 0)),
                  pl.BlockSpec((1, N_MOD, d), lambda i: (i // tiles_per_batch, 0, 0)),
                  pl.BlockSpec((1, d), lambda i: (0, 0))],
        out_specs=pl.BlockSpec((tm, d), lambda i: (i, 0)),
        compiler_params=pltpu.CompilerParams(dimension_semantics=("parallel",),
                                             vmem_limit_bytes=VMEM_LIMIT),
        name="out_proj",
    )(fox2, gla2, w_out, x2, mod, g)


def _mlp_kernel(x_ref, mod_ref, gpre_ref, gpost_ref, w1_ref, w2_ref, o_ref, h_scr):
    f = pl.program_id(1)
    last = pl.num_programs(1) - 1
    half = x_ref.shape[0] // 2
    halves = (slice(0, half), slice(half, 2 * half))

    def hidden_block(hb):
        u = jnp.maximum(jnp.dot(hb, w1_ref[...], preferred_element_type=F32), 0.0)
        return jnp.dot((u * u).astype(BF16), w2_ref[...], preferred_element_type=F32)

    @pl.when(f == 0)
    def _():
        gain = gpre_ref[...] * (1.0 + mod_ref[0, 4:5, :])
        for rows in halves:
            x = x_ref[rows, :]
            inv = lax.rsqrt(jnp.mean(x * x, axis=-1, keepdims=True) + EPS)
            hb = (x * inv * gain + mod_ref[0, 3:4, :]).astype(BF16)
            h_scr[rows, :] = hb
            o_ref[rows, :] = hidden_block(hb)

    @pl.when((f > 0) & (f < last))
    def _():
        for rows in halves:
            o_ref[rows, :] += hidden_block(h_scr[rows, :])

    @pl.when(f == last)
    def _():
        gain = mod_ref[0, 5:6, :] * gpost_ref[...]
        for rows in halves:
            y = o_ref[rows, :] + hidden_block(h_scr[rows, :])
            inv = lax.rsqrt(jnp.mean(y * y, axis=-1, keepdims=True) + EPS)
            o_ref[rows, :] = x_ref[rows, :] + y * inv * gain


def _mlp(x1, mod, g_pre, g_post, w1, w2, rows_per_batch, tm=1024, tf=1024):
    t, d = x1.shape
    dff = w1.shape[1]
    tiles_per_batch = rows_per_batch // tm
    return pl.pallas_call(
        _mlp_kernel,
        out_shape=jax.ShapeDtypeStruct((t, d), F32),
        grid=(t // tm, dff // tf),
        in_specs=[pl.BlockSpec((tm, d), lambda i, f: (i, 0)),
                  pl.BlockSpec((1, N_MOD, d), lambda i, f: (i // tiles_per_batch, 0, 0)),
                  pl.BlockSpec((1, d), lambda i, f: (0, 0)),
                  pl.BlockSpec((1, d), lambda i, f: (0, 0)),
                  pl.BlockSpec((d, tf), lambda i, f: (0, f)),
                  pl.BlockSpec((tf, d), lambda i, f: (f, 0))],
        out_specs=pl.BlockSpec((tm, d), lambda i, f: (i, 0)),
        scratch_shapes=[pltpu.VMEM((tm, d), BF16)],
        compiler_params=pltpu.CompilerParams(dimension_semantics=("parallel", "arbitrary"),
                                             vmem_limit_bytes=VMEM_LIMIT_LARGE),
        name="mlp",
    )(x1, mod, g_pre, g_post, w1, w2)


def _transposed_in_proj_weight(w_in):
    fox_end = 3 * FOX_HEADS * FOX_HEAD_DIM
    gla_start = fox_end + FOX_HEADS
    gla_end = gla_start + 2 * GLA_HEADS * GLA_DK + GLA_HEADS * GLA_DV
    gate_start = gla_end + GLA_GATE_RANK
    wt = jnp.swapaxes(w_in, 0, 1).astype(BF16)
    pad = jnp.zeros((LANE - FOX_HEADS - GLA_GATE_RANK, w_in.shape[0]), BF16)
    small = jnp.concatenate([wt[fox_end:gla_start], wt[gla_end:gate_start], pad], axis=0)
    return wt, small


def kernel(x, c, w_ada, b_ada, g_pre_mix, g_post_mix, w_in, b_fgate, w_gla_a2, b_gla_a2,
           g_fox_out, g_gla_out, w_out, g_pre_mlp, g_post_mlp, w_mlp_in, w_mlp_out):
    b, s, d = x.shape
    depth = w_ada.shape[0]
    row = lambda v: v.reshape(1, -1)
    for i in range(depth):
        c_pad = jnp.concatenate([c, jnp.zeros((8 - b, d), c.dtype)], axis=0)
        mod = _ada(c_pad, w_ada[i], row(b_ada[i]))[:b].reshape(b, N_MOD, d)

        wt, w_small = _transposed_in_proj_weight(w_in[i])
        x2 = x.reshape(b * s, d)
        proj, small = _in_proj(x2, mod, row(g_pre_mix[i]), wt, w_small, s)
        proj3 = proj.reshape(b, s, -1)
        small3 = small.reshape(b, s, LANE)

        fbias = jnp.concatenate([b_fgate[i], jnp.zeros((LANE - FOX_HEADS,), F32)]).reshape(1, LANE)
        fox, (w_out_b, w_mlp_out_b) = _fox_attn(
            proj3, _fox_cum(small3, fbias), g_fox_out[i].reshape(FOX_HEADS, 1, FOX_HEAD_DIM),
            [w_out[i], w_mlp_out[i]])

        kw = GLA_HEADS * GLA_DK
        wa_pad = jnp.concatenate(
            [jnp.zeros((FOX_HEADS, kw), F32), w_gla_a2[i],
             jnp.zeros((LANE - FOX_HEADS - GLA_GATE_RANK, kw), F32)], axis=0).astype(BF16)
        gla, (w_mlp_in_b,) = _gla(
            proj3, small3, wa_pad, row(b_gla_a2[i]), g_gla_out[i].reshape(GLA_HEADS, 1, GLA_DV),
            [w_mlp_in[i]])

        x1 = _out_proj(fox.reshape(b * s, -1), gla.reshape(b * s, -1), w_out_b,
                       x2, mod, row(g_post_mix[i]), s)
        x2 = _mlp(x1, mod, row(g_pre_mlp[i]), row(g_post_mlp[i]), w_mlp_in_b, w_mlp_out_b, s)
        x = x2.reshape(b, s, d)
    return x
```

```python
import functools

import jax
import jax.numpy as jnp
from jax import lax
from jax.experimental import pallas as pl
from jax.experimental.pallas import tpu as pltpu

F32 = jnp.float32
BF16 = jnp.bfloat16

EPS = 1e-6
N_MOD = 6
FOX_HEADS = 8
FOX_HEAD_DIM = 128
GLA_HEADS = 4
GLA_DK = 128
GLA_DV = 256
GLA_GATE_RANK = 16
GLA_GATE_TEMP = 16.0
CHUNK = 64
LANE = 128
BF16_ROWS = 16
VMEM_LIMIT = 56 * 1024 * 1024
VMEM_LIMIT_LARGE = 60 * 1024 * 1024

NEG_BIG = -1e30
LOG2E = 1.4426950408889634
FOX_HEADS_PER_STEP = 4


def _nt_dot(a, b):
    return lax.dot_general(a, b, (((1,), (1,)), ((), ())), preferred_element_type=F32)


def _tn_dot(a, b):
    return lax.dot_general(a, b, (((0,), (0,)), ((), ())), preferred_element_type=F32)


def _log_sigmoid(z):
    return jnp.minimum(z, 0.0) - jnp.log(1.0 + jnp.exp(-jnp.abs(z)))


def _sigmoid(z):
    return 1.0 / (1.0 + jnp.exp(-z))


def _tri_cumsum(tri, v):
    hi = v.astype(BF16)
    lo = (v - hi.astype(F32)).astype(BF16)
    return (jnp.dot(tri, hi, preferred_element_type=F32)
            + jnp.dot(tri, lo, preferred_element_type=F32))


def _lower_tri(n, value=1.0):
    r = lax.broadcasted_iota(jnp.int32, (n, n), 0)
    c = lax.broadcasted_iota(jnp.int32, (n, n), 1)
    return jnp.where(r >= c, value, 0.0).astype(BF16)


def _ada_kernel(c_ref, w_ref, b_ref, o_ref):
    c = c_ref[...]
    act = (c * _sigmoid(c)).astype(BF16)
    o_ref[...] = jnp.dot(act, w_ref[...].astype(BF16), preferred_element_type=F32) + b_ref[...]


def _ada(c_pad, w_ada, b_ada, tn=1024):
    m, d = c_pad.shape
    n = w_ada.shape[1]
    return pl.pallas_call(
        _ada_kernel,
        out_shape=jax.ShapeDtypeStruct((m, n), F32),
        grid=(n // tn,),
        in_specs=[pl.BlockSpec((m, d), lambda j: (0, 0)),
                  pl.BlockSpec((d, tn), lambda j: (0, j)),
                  pl.BlockSpec((1, tn), lambda j: (0, j))],
        out_specs=pl.BlockSpec((m, tn), lambda j: (0, j)),
        compiler_params=pltpu.CompilerParams(dimension_semantics=("parallel",),
                                             vmem_limit_bytes=VMEM_LIMIT),
        name="ada",
    )(c_pad, w_ada, b_ada)


def _inproj_kernel(x_ref, mod_ref, g_ref, wlo_ref, whi_ref, ws_ref, o_ref, os_ref, h_scr):
    j = pl.program_id(1)
    half = wlo_ref.shape[0]

    @pl.when(j == 0)
    def _():
        x = x_ref[...]
        inv = lax.rsqrt(jnp.mean(x * x, axis=-1, keepdims=True) + EPS)
        gain = g_ref[...] * (1.0 + mod_ref[0, 1:2, :])
        hb = (x * inv * gain + mod_ref[0, 0:1, :]).astype(BF16)
        h_scr[...] = hb
        os_ref[...] = _nt_dot(hb, ws_ref[...])
        o_ref[:, :half] = (_nt_dot(hb, wlo_ref[...]) * (FOX_HEAD_DIM ** -0.5 * LOG2E)).astype(BF16)
        o_ref[:, half:] = _nt_dot(hb, whi_ref[...]).astype(BF16)

    @pl.when(j > 0)
    def _():
        o_ref[:, :half] = _nt_dot(h_scr[...], wlo_ref[...]).astype(BF16)
        o_ref[:, half:] = _nt_dot(h_scr[...], whi_ref[...]).astype(BF16)


def _in_proj(x2, mod, g, wt, w_small, rows_per_batch, tm=1024):
    t, d = x2.shape
    half = FOX_HEADS * FOX_HEAD_DIM
    n = wt.shape[0] - FOX_HEADS - GLA_GATE_RANK
    tn = 2 * half
    assert n % tn == 0 and GLA_HEADS * (2 * GLA_DK + GLA_DV) == tn
    tiles_per_batch = rows_per_batch // tm
    lo_start = lambda j: pl.multiple_of(j * tn + jnp.where(j >= 2, FOX_HEADS, 0), FOX_HEADS)
    hi_start = lambda j: pl.multiple_of(j * tn + half + jnp.where(j >= 1, FOX_HEADS, 0)
                                        + jnp.where(j >= 2, GLA_GATE_RANK, 0), FOX_HEADS)
    return pl.pallas_call(
        _inproj_kernel,
        out_shape=(jax.ShapeDtypeStruct((t, n), BF16),
                   jax.ShapeDtypeStruct((t, LANE), F32)),
        grid=(t // tm, n // tn),
        in_specs=[pl.BlockSpec((tm, d), lambda i, j: (i, 0)),
                  pl.BlockSpec((1, N_MOD, d), lambda i, j: (i // tiles_per_batch, 0, 0)),
                  pl.BlockSpec((1, d), lambda i, j: (0, 0)),
                  pl.BlockSpec((pl.Element(half), pl.Element(d)), lambda i, j: (lo_start(j), 0)),
                  pl.BlockSpec((pl.Element(half), pl.Element(d)), lambda i, j: (hi_start(j), 0)),
                  pl.BlockSpec((LANE, d), lambda i, j: (0, 0))],
        out_specs=(pl.BlockSpec((tm, tn), lambda i, j: (i, j)),
                   pl.BlockSpec((tm, LANE), lambda i, j: (i, 0))),
        scratch_shapes=[pltpu.VMEM((tm, d), BF16)],
        compiler_params=pltpu.CompilerParams(dimension_semantics=("parallel", "arbitrary"),
                                             vmem_limit_bytes=VMEM_LIMIT),
        name="in_proj",
    )(x2, mod, g, wt, wt, w_small)


def _cum_kernel(s_ref, b_ref, o_ref, *, blk):
    s = s_ref.shape[1]
    tri = _lower_tri(blk)
    col = lax.broadcasted_iota(jnp.int32, (blk, LANE), 1)
    carry = jnp.zeros((1, LANE), F32)
    for r in range(s // blk):
        rows = slice(r * blk, (r + 1) * blk)
        z = s_ref[0, rows, :] + b_ref[...]
        cs = _tri_cumsum(tri, _log_sigmoid(z)) + carry
        carry = cs[blk - 1:blk, :]
        c2 = jnp.where(col < FOX_HEADS, cs * LOG2E, 0.0)
        hi = c2.astype(BF16).astype(F32)
        rem = c2 - hi
        mid = rem.astype(BF16).astype(F32)
        lo = (rem - mid).astype(BF16).astype(F32)
        pieces = hi + pltpu.roll(mid, FOX_HEADS, 1) + pltpu.roll(lo, 2 * FOX_HEADS, 1)
        o_ref[0, rows, :] = pieces.astype(BF16)


def _fox_cum(small3, bias_row, blk=256):
    b, s, _ = small3.shape
    return pl.pallas_call(
        functools.partial(_cum_kernel, blk=blk),
        out_shape=jax.ShapeDtypeStruct((b, s, LANE), BF16),
        grid=(b,),
        in_specs=[pl.BlockSpec((1, s, LANE), lambda i: (i, 0, 0)),
                  pl.BlockSpec((1, LANE), lambda i: (0, 0))],
        out_specs=pl.BlockSpec((1, s, LANE), lambda i: (i, 0, 0)),
        compiler_params=pltpu.CompilerParams(dimension_semantics=("parallel",),
                                             vmem_limit_bytes=VMEM_LIMIT),
        name="fox_cum",
    )(small3, bias_row)


class _FoxHead:
    def __init__(self, ka, vt, sa, sb, xa, xb, m, l, acc):
        self.ka, self.vt, self.m, self.l, self.acc = ka, vt, m, l, acc
        self.a = (sa, xa)
        self.b = (sb, xb)


def _fox_kernel(q_ref, k_ref, v_ref, a_ref, g_ref, *refs, tq, n_cast):
    o_ref = refs[n_cast]
    scratch = refs[2 * n_cast + 1:]
    heads = [_FoxHead(*scratch[n * 9:(n + 1) * 9]) for n in range(FOX_HEADS_PER_STEP)]
    t = pl.program_id(2)
    lanes = lambda n: slice(n * LANE, (n + 1) * LANE)

    @pl.when(t == 0)
    def _():
        for n, hd in enumerate(heads):
            hd.ka[:, 0:LANE] = k_ref[0, :, lanes(n)]
            hd.ka[:, LANE:] = a_ref[0]
            hd.vt[...] = v_ref[0, :, lanes(n)].T

    for src, dst in zip(refs[:n_cast], refs[n_cast + 1:2 * n_cast + 1]):
        dst[...] = src[...].astype(BF16)

    col = lax.broadcasted_iota(jnp.int32, (tq, LANE), 1)

    def tile_queries(half):
        rows = slice(half * tq, (half + 1) * tq)
        qas = []
        for n in range(len(heads)):
            h = pl.program_id(1) * FOX_HEADS_PER_STEP + n
            pick = (col == h) | (col == h + FOX_HEADS) | (col == h + 2 * FOX_HEADS)
            qas.append(jnp.concatenate([q_ref[0, rows, lanes(n)],
                                        jnp.where(pick, -1.0, 0.0).astype(BF16)], axis=1))
        return qas

    def reset_state():
        for hd in heads:
            hd.m[...] = jnp.full_like(hd.m, NEG_BIG)
            hd.l[...] = jnp.zeros_like(hd.l)
            hd.acc[...] = jnp.zeros_like(hd.acc)

    def logits_to(pick_buf, qas, blk):
        keys = pl.ds(pl.multiple_of(blk * tq, tq), tq)
        for hd, qa in zip(heads, qas):
            s_ref, max_ref = pick_buf(hd)
            st = _nt_dot(hd.ka[keys, :], qa)
            s_ref[...] = st
            max_ref[...] = jnp.max(st, axis=0, keepdims=True)

    def absorb(pick_buf, blk, masked):
        keys = pl.ds(pl.multiple_of(blk * tq, tq), tq)
        for hd in heads:
            s_ref, max_ref = pick_buf(hd)
            st = s_ref[...]
            if masked:
                kr = lax.broadcasted_iota(jnp.int32, (tq, tq), 0)
                qc = lax.broadcasted_iota(jnp.int32, (tq, tq), 1)
                st = jnp.where(kr <= qc, st, NEG_BIG)
                block_max = jnp.max(st, axis=0, keepdims=True)
            else:
                block_max = max_ref[...]
            m = hd.m[...]
            m_new = jnp.maximum(m, block_max)
            alpha = jnp.exp2(m - m_new)
            p = jnp.exp2(st - m_new)
            hd.m[...] = m_new
            hd.l[...] = alpha * hd.l[...] + jnp.sum(p, axis=0, keepdims=True)
            pv = jnp.dot(hd.vt[:, keys], p.astype(BF16), preferred_element_type=F32)
            hd.acc[...] = alpha * hd.acc[...] + pv

    def full_block_pairs(first, second, qas):
        def body(u, carry):
            logits_to(second, qas, 2 * u + 1)
            absorb(first, 2 * u, False)
            logits_to(first, qas, 2 * u + 2)
            absorb(second, 2 * u + 1, False)
            return carry
        lax.fori_loop(0, t, body, 0)

    def finish(half):
        for n, hd in enumerate(heads):
            ot = hd.acc[...] / hd.l[...]
            ot = ot * lax.rsqrt(jnp.mean(ot * ot, axis=0, keepdims=True) + EPS)
            o_ref[0, half * tq:(half + 1) * tq, lanes(n)] = (ot.T * g_ref[n]).astype(BF16)

    buf_a = lambda hd: hd.a
    buf_b = lambda hd: hd.b
    q_even, q_odd = tile_queries(0), tile_queries(1)

    reset_state()
    logits_to(buf_a, q_even, 0)
    full_block_pairs(buf_a, buf_b, q_even)
    logits_to(buf_b, q_odd, 0)
    absorb(buf_a, 2 * t, True)
    finish(0)

    reset_state()
    full_block_pairs(buf_b, buf_a, q_odd)
    logits_to(buf_a, q_odd, 2 * t + 1)
    absorb(buf_b, 2 * t, False)
    absorb(buf_a, 2 * t + 1, True)
    finish(1)


def _cast_slabs(f32_weights, n_steps, step_index):
    assert all(w.shape[0] % (n_steps * BF16_ROWS) == 0 for w in f32_weights)
    return [pl.BlockSpec((w.shape[0] // n_steps, w.shape[1]), step_index) for w in f32_weights]


def _fox_attn(proj3, pieces3, g_fox, f32_weights, tq=512):
    b, s, _ = proj3.shape
    nh = FOX_HEADS_PER_STEP
    groups = FOX_HEADS // nh
    width = nh * FOX_HEAD_DIM
    nt = s // (2 * tq)
    slabs = _cast_slabs(f32_weights, b * groups * nt,
                        lambda bi, gi, ti: ((bi * groups + gi) * nt + ti, 0))
    per_head = [pltpu.VMEM((s, 2 * LANE), BF16), pltpu.VMEM((FOX_HEAD_DIM, s), BF16),
                pltpu.VMEM((tq, tq), F32), pltpu.VMEM((tq, tq), F32),
                pltpu.VMEM((1, tq), F32), pltpu.VMEM((1, tq), F32),
                pltpu.VMEM((1, tq), F32), pltpu.VMEM((1, tq), F32),
                pltpu.VMEM((FOX_HEAD_DIM, tq), F32)]
    outs = pl.pallas_call(
        functools.partial(_fox_kernel, tq=tq, n_cast=len(f32_weights)),
        out_shape=[jax.ShapeDtypeStruct((b, s, FOX_HEADS * FOX_HEAD_DIM), BF16)]
        + [jax.ShapeDtypeStruct(w.shape, BF16) for w in f32_weights],
        grid=(b, groups, nt),
        in_specs=[pl.BlockSpec((1, 2 * tq, width), lambda bi, gi, ti: (bi, ti, gi)),
                  pl.BlockSpec((1, s, width), lambda bi, gi, ti: (bi, 0, groups + gi)),
                  pl.BlockSpec((1, s, width), lambda bi, gi, ti: (bi, 0, 2 * groups + gi)),
                  pl.BlockSpec((1, s, LANE), lambda bi, gi, ti: (bi, 0, 0)),
                  pl.BlockSpec((nh, 1, LANE), lambda bi, gi, ti: (gi, 0, 0))] + slabs,
        out_specs=[pl.BlockSpec((1, 2 * tq, width), lambda bi, gi, ti: (bi, ti, gi))] + slabs,
        scratch_shapes=per_head * nh,
        compiler_params=pltpu.CompilerParams(
            dimension_semantics=("parallel", "parallel", "arbitrary"),
            vmem_limit_bytes=VMEM_LIMIT_LARGE),
        name="fox_attn",
    )(proj3, proj3, proj3, pieces3, g_fox, *f32_weights)
    return outs[0], outs[1:]


def _gla_kernel(q_ref, k_ref, v_ref, r_ref, s_ref, wa_ref, ba_ref, gg_ref, *refs, ts, n_cast):
    o_ref, st_ref = refs[n_cast], refs[-1]
    for src, dst in zip(refs[:n_cast], refs[n_cast + 1:2 * n_cast + 1]):
        dst[...] = src[...].astype(BF16)

    @pl.when(pl.program_id(2) == 0)
    def _():
        st_ref[...] = jnp.zeros_like(st_ref)

    pre = jnp.dot(s_ref[0].astype(BF16), wa_ref[...], preferred_element_type=F32) + ba_ref[...]
    la = _log_sigmoid(pre)
    tri = _lower_tri(CHUNK, 1.0 / GLA_GATE_TEMP)
    chunks = [slice(c * CHUNK, (c + 1) * CHUNK) for c in range(ts // CHUNK)]
    cums = [_tri_cumsum(tri, la[sl]) for sl in chunks]
    tots = [cum[CHUNK - 1:CHUNK, :] for cum in cums]
    ups = []
    for sl, cum, tot in zip(chunks, cums, tots):
        kd = (k_ref[0, sl, :].astype(F32) * jnp.exp(tot - cum)).astype(BF16)
        ups.append(_tn_dot(v_ref[0, sl, :], kd))
    st = st_ref[...]
    states = []
    for tot, up in zip(tots, ups):
        st = st * jnp.exp(tot) + up
        states.append(st.astype(BF16))
    st_ref[...] = st
    outs = [_nt_dot(q_ref[0, sl, :], sb) for sl, sb in zip(chunks, states)]
    o = jnp.concatenate(outs, axis=0)
    qs = GLA_DK ** -0.5
    inv = qs * lax.rsqrt(qs * qs * jnp.mean(o * o, axis=-1, keepdims=True) + EPS)
    o = o * inv * gg_ref[0]
    r = r_ref[0].astype(F32)
    o_ref[0] = (o * (r * _sigmoid(r))).astype(BF16)


def _gla(proj3, small3, wa_pad, ba_row, g_gla, f32_weights, ts=1024):
    b, s, _ = proj3.shape
    q_blk = 3 * FOX_HEADS
    k_blk = q_blk + GLA_HEADS
    v_blk = (k_blk + GLA_HEADS) // 2
    r_blk = v_blk + GLA_HEADS
    nt = s // ts
    slabs = _cast_slabs(f32_weights, b * GLA_HEADS * nt,
                        lambda bi, gi, ti: ((bi * GLA_HEADS + gi) * nt + ti, 0))
    outs = pl.pallas_call(
        functools.partial(_gla_kernel, ts=ts, n_cast=len(f32_weights)),
        out_shape=[jax.ShapeDtypeStruct((b, s, GLA_HEADS * GLA_DV), BF16)]
        + [jax.ShapeDtypeStruct(w.shape, BF16) for w in f32_weights],
        grid=(b, GLA_HEADS, nt),
        in_specs=[pl.BlockSpec((1, ts, GLA_DK), lambda bi, gi, ti: (bi, ti, q_blk + gi)),
                  pl.BlockSpec((1, ts, GLA_DK), lambda bi, gi, ti: (bi, ti, k_blk + gi)),
                  pl.BlockSpec((1, ts, GLA_DV), lambda bi, gi, ti: (bi, ti, v_blk + gi)),
                  pl.BlockSpec((1, ts, GLA_DV), lambda bi, gi, ti: (bi, ti, r_blk + gi)),
                  pl.BlockSpec((1, ts, LANE), lambda bi, gi, ti: (bi, ti, 0)),
                  pl.BlockSpec((LANE, GLA_DK), lambda bi, gi, ti: (0, gi)),
                  pl.BlockSpec((1, GLA_DK), lambda bi, gi, ti: (0, gi)),
                  pl.BlockSpec((1, 1, GLA_DV), lambda bi, gi, ti: (gi, 0, 0))] + slabs,
        out_specs=[pl.BlockSpec((1, ts, GLA_DV), lambda bi, gi, ti: (bi, ti, gi))] + slabs,
        scratch_shapes=[pltpu.VMEM((GLA_DV, GLA_DK), F32)],
        compiler_params=pltpu.CompilerParams(
            dimension_semantics=("parallel", "parallel", "arbitrary"),
            vmem_limit_bytes=VMEM_LIMIT),
        name="gla",
    )(proj3, proj3, proj3, proj3, small3, wa_pad, ba_row, g_gla, *f32_weights)
    return outs[0], outs[1:]


def _outproj_kernel(fox_ref, gla_ref, w_ref, x_ref, mod_ref, g_ref, o_ref):
    wide = fox_ref.shape[1]
    gain = mod_ref[0, 2:3, :] * g_ref[...]
    half = x_ref.shape[0] // 2
    for rows in (slice(0, half), slice(half, 2 * half)):
        y = (jnp.dot(fox_ref[rows, :], w_ref[0:wide, :], preferred_element_type=F32)
             + jnp.dot(gla_ref[rows, :], w_ref[wide:, :], preferred_element_type=F32))
        inv = lax.rsqrt(jnp.mean(y * y, axis=-1, keepdims=True) + EPS)
        o_ref[rows, :] = x_ref[rows, :] + y * inv * gain


def _out_proj(fox2, gla2, w_out, x2, mod, g, rows_per_batch, tm=1024):
    t, d = x2.shape
    wide = fox2.shape[1]
    tiles_per_batch = rows_per_batch // tm
    return pl.pallas_call(
        _outproj_kernel,
        out_shape=jax.ShapeDtypeStruct((t, d), F32),
        grid=(t // tm,),
        in_specs=[pl.BlockSpec((tm, wide), lambda i: (i, 0)),
                  pl.BlockSpec((tm, wide), lambda i: (i, 0)),
                  pl.BlockSpec((2 * wide, d), lambda i: (0, 0), pipeline_mode=pl.Buffered(1)),
                  pl.BlockSpec((tm, d), lambda i: (i, 0)),
                  pl.BlockSpec((1, N_MOD, d), lambda i: (i // tiles_per_batch, 0, 0)),
                  pl.BlockSpec((1, d), lambda i: (0, 0))],
        out_specs=pl.BlockSpec((tm, d), lambda i: (i, 0)),
        compiler_params=pltpu.CompilerParams(dimension_semantics=("parallel",),
                                             vmem_limit_bytes=VMEM_LIMIT),
        name="out_proj",
    )(fox2, gla2, w_out, x2, mod, g)


def _mlp_kernel(x_ref, mod_ref, gpre_ref, gpost_ref, w1_ref, w2_ref, o_ref, h_scr):
    f = pl.program_id(1)
    last = pl.num_programs(1) - 1
    half = x_ref.shape[0] // 2
    halves = (slice(0, half), slice(half, 2 * half))

    def hidden_block(hb):
        u = jnp.maximum(jnp.dot(hb, w1_ref[...], preferred_element_type=F32), 0.0)
        return jnp.dot((u * u).astype(BF16), w2_ref[...], preferred_element_type=F32)

    @pl.when(f == 0)
    def _():
        gain = gpre_ref[...] * (1.0 + mod_ref[0, 4:5, :])
        for rows in halves:
            x = x_ref[rows, :]
            inv = lax.rsqrt(jnp.mean(x * x, axis=-1, keepdims=True) + EPS)
            hb = (x * inv * gain + mod_ref[0, 3:4, :]).astype(BF16)
            h_scr[rows, :] = hb
            o_ref[rows, :] = hidden_block(hb)

    @pl.when((f > 0) & (f < last))
    def _():
        for rows in halves:
            o_ref[rows, :] += hidden_block(h_scr[rows, :])

    @pl.when(f == last)
    def _():
        gain = mod_ref[0, 5:6, :] * gpost_ref[...]
        for rows in halves:
            y = o_ref[rows, :] + hidden_block(h_scr[rows, :])
            inv = lax.rsqrt(jnp.mean(y * y, axis=-1, keepdims=True) + EPS)
            o_ref[rows, :] = x_ref[rows, :] + y * inv * gain


def _mlp(x1, mod, g_pre, g_post, w1, w2, rows_per_batch, tm=1024, tf=1024):
    t, d = x1.shape
    dff = w1.shape[1]
    tiles_per_batch = rows_per_batch // tm
    return pl.pallas_call(
        _mlp_kernel,
        out_shape=jax.ShapeDtypeStruct((t, d), F32),
        grid=(t // tm, dff // tf),
        in_specs=[pl.BlockSpec((tm, d), lambda i, f: (i, 0)),
                  pl.BlockSpec((1, N_MOD, d), lambda i, f: (i // tiles_per_batch, 0, 0)),
                  pl.BlockSpec((1, d), lambda i, f: (0, 0)),
                  pl.BlockSpec((1, d), lambda i, f: (0, 0)),
                  pl.BlockSpec((d, tf), lambda i, f: (0, f)),
                  pl.BlockSpec((tf, d), lambda i, f: (f, 0))],
        out_specs=pl.BlockSpec((tm, d), lambda i, f: (i, 0)),
        scratch_shapes=[pltpu.VMEM((tm, d), BF16)],
        compiler_params=pltpu.CompilerParams(dimension_semantics=("parallel", "arbitrary"),
                                             vmem_limit_bytes=VMEM_LIMIT_LARGE),
        name="mlp",
    )(x1, mod, g_pre, g_post, w1, w2)


def _transposed_in_proj_weight(w_in):
    fox_end = 3 * FOX_HEADS * FOX_HEAD_DIM
    gla_start = fox_end + FOX_HEADS
    gla_end = gla_start + 2 * GLA_HEADS * GLA_DK + GLA_HEADS * GLA_DV
    gate_start = gla_end + GLA_GATE_RANK
    wt = jnp.swapaxes(w_in, 0, 1).astype(BF16)
    pad = jnp.zeros((LANE - FOX_HEADS - GLA_GATE_RANK, w_in.shape[0]), BF16)
    small = jnp.concatenate([wt[fox_end:gla_start], wt[gla_end:gate_start], pad], axis=0)
    return wt, small


def kernel(x, c, w_ada, b_ada, g_pre_mix, g_post_mix, w_in, b_fgate, w_gla_a2, b_gla_a2,
           g_fox_out, g_gla_out, w_out, g_pre_mlp, g_post_mlp, w_mlp_in, w_mlp_out):
    b, s, d = x.shape
    depth = w_ada.shape[0]
    row = lambda v: v.reshape(1, -1)
    for i in range(depth):
        c_pad = jnp.concatenate([c, jnp.zeros((8 - b, d), c.dtype)], axis=0)
        mod = _ada(c_pad, w_ada[i], row(b_ada[i]))[:b].reshape(b, N_MOD, d)

        wt, w_small = _transposed_in_proj_weight(w_in[i])
        x2 = x.reshape(b * s, d)
        proj, small = _in_proj(x2, mod, row(g_pre_mix[i]), wt, w_small, s)
        proj3 = proj.reshape(b, s, -1)
        small3 = small.reshape(b, s, LANE)

        fbias = jnp.concatenate([b_fgate[i], jnp.zeros((LANE - FOX_HEADS,), F32)]).reshape(1, LANE)
        fox, (w_out_b, w_mlp_out_b) = _fox_attn(
            proj3, _fox_cum(small3, fbias), g_fox_out[i].reshape(FOX_HEADS, 1, FOX_HEAD_DIM),
            [w_out[i], w_mlp_out[i]])

        kw = GLA_HEADS * GLA_DK
        wa_pad = jnp.concatenate(
            [jnp.zeros((FOX_HEADS, kw), F32), w_gla_a2[i],
             jnp.zeros((LANE - FOX_HEADS - GLA_GATE_RANK, kw), F32)], axis=0).astype(BF16)
        gla, (w_mlp_in_b,) = _gla(
            proj3, small3, wa_pad, row(b_gla_a2[i]), g_gla_out[i].reshape(GLA_HEADS, 1, GLA_DV),
            [w_mlp_in[i]])

        x1 = _out_proj(fox.reshape(b * s, -1), gla.reshape(b * s, -1), w_out_b,
                       x2, mod, row(g_post_mix[i]), s)
        x2 = _mlp(x1, mod, row(g_pre_mlp[i]), row(g_post_mlp[i]), w_mlp_in_b, w_mlp_out_b, s)
        x = x2.reshape(b, s, d)
    return x
```

```python
import functools

import jax
import jax.numpy as jnp
from jax import lax
from jax.experimental import pallas as pl
from jax.experimental.pallas import tpu as pltpu

F32 = jnp.float32
BF16 = jnp.bfloat16

EPS = 1e-6
N_MOD = 6
FOX_HEADS = 8
FOX_HEAD_DIM = 128
GLA_HEADS = 4
GLA_DK = 128
GLA_DV = 256
GLA_GATE_RANK = 16
GLA_GATE_TEMP = 16.0
CHUNK = 64
LANE = 128
BF16_ROWS = 16
VMEM_LIMIT = 56 * 1024 * 1024
VMEM_LIMIT_LARGE = 60 * 1024 * 1024

NEG_BIG = -1e30
LOG2E = 1.4426950408889634
FOX_HEADS_PER_STEP = 4


def _nt_dot(a, b):
    return lax.dot_general(a, b, (((1,), (1,)), ((), ())), preferred_element_type=F32)


def _tn_dot(a, b):
    return lax.dot_general(a, b, (((0,), (0,)), ((), ())), preferred_element_type=F32)


def _log_sigmoid(z):
    return jnp.minimum(z, 0.0) - jnp.log(1.0 + jnp.exp(-jnp.abs(z)))


def _sigmoid(z):
    return 1.0 / (1.0 + jnp.exp(-z))


def _tri_cumsum(tri, v):
    hi = v.astype(BF16)
    lo = (v - hi.astype(F32)).astype(BF16)
    return (jnp.dot(tri, hi, preferred_element_type=F32)
            + jnp.dot(tri, lo, preferred_element_type=F32))


def _lower_tri(n, value=1.0):
    r = lax.broadcasted_iota(jnp.int32, (n, n), 0)
    c = lax.broadcasted_iota(jnp.int32, (n, n), 1)
    return jnp.where(r >= c, value, 0.0).astype(BF16)


def _ada_kernel(c_ref, w_ref, b_ref, o_ref):
    c = c_ref[...]
    act = (c * _sigmoid(c)).astype(BF16)
    o_ref[...] = jnp.dot(act, w_ref[...].astype(BF16), preferred_element_type=F32) + b_ref[...]


def _ada(c_pad, w_ada, b_ada, tn=1024):
    m, d = c_pad.shape
    n = w_ada.shape[1]
    return pl.pallas_call(
        _ada_kernel,
        out_shape=jax.ShapeDtypeStruct((m, n), F32),
        grid=(n // tn,),
        in_specs=[pl.BlockSpec((m, d), lambda j: (0, 0)),
                  pl.BlockSpec((d, tn), lambda j: (0, j)),
                  pl.BlockSpec((1, tn), lambda j: (0, j))],
        out_specs=pl.BlockSpec((m, tn), lambda j: (0, j)),
        compiler_params=pltpu.CompilerParams(dimension_semantics=("parallel",),
                                             vmem_limit_bytes=VMEM_LIMIT),
        name="ada",
    )(c_pad, w_ada, b_ada)


def _inproj_kernel(x_ref, mod_ref, g_ref, wlo_ref, whi_ref, ws_ref, o_ref, os_ref, h_scr):
    j = pl.program_id(1)
    half = wlo_ref.shape[0]

    @pl.when(j == 0)
    def _():
        x = x_ref[...]
        inv = lax.rsqrt(jnp.mean(x * x, axis=-1, keepdims=True) + EPS)
        gain = g_ref[...] * (1.0 + mod_ref[0, 1:2, :])
        hb = (x * inv * gain + mod_ref[0, 0:1, :]).astype(BF16)
        h_scr[...] = hb
        os_ref[...] = _nt_dot(hb, ws_ref[...])
        o_ref[:, :half] = (_nt_dot(hb, wlo_ref[...]) * (FOX_HEAD_DIM ** -0.5 * LOG2E)).astype(BF16)
        o_ref[:, half:] = _nt_dot(hb, whi_ref[...]).astype(BF16)

    @pl.when(j > 0)
    def _():
        o_ref[:, :half] = _nt_dot(h_scr[...], wlo_ref[...]).astype(BF16)
        o_ref[:, half:] = _nt_dot(h_scr[...], whi_ref[...]).astype(BF16)


def _in_proj(x2, mod, g, wt, w_small, rows_per_batch, tm=1024):
    t, d = x2.shape
    half = FOX_HEADS * FOX_HEAD_DIM
    n = wt.shape[0] - FOX_HEADS - GLA_GATE_RANK
    tn = 2 * half
    assert n % tn == 0 and GLA_HEADS * (2 * GLA_DK + GLA_DV) == tn
    tiles_per_batch = rows_per_batch // tm
    lo_start = lambda j: pl.multiple_of(j * tn + jnp.where(j >= 2, FOX_HEADS, 0), FOX_HEADS)
    hi_start = lambda j: pl.multiple_of(j * tn + half + jnp.where(j >= 1, FOX_HEADS, 0)
                                        + jnp.where(j >= 2, GLA_GATE_RANK, 0), FOX_HEADS)
    return pl.pallas_call(
        _inproj_kernel,
        out_shape=(jax.ShapeDtypeStruct((t, n), BF16),
                   jax.ShapeDtypeStruct((t, LANE), F32)),
        grid=(t // tm, n // tn),
        in_specs=[pl.BlockSpec((tm, d), lambda i, j: (i, 0)),
                  pl.BlockSpec((1, N_MOD, d), lambda i, j: (i // tiles_per_batch, 0, 0)),
                  pl.BlockSpec((1, d), lambda i, j: (0, 0)),
                  pl.BlockSpec((pl.Element(half), pl.Element(d)), lambda i, j: (lo_start(j), 0)),
                  pl.BlockSpec((pl.Element(half), pl.Element(d)), lambda i, j: (hi_start(j), 0)),
                  pl.BlockSpec((LANE, d), lambda i, j: (0, 0))],
        out_specs=(pl.BlockSpec((tm, tn), lambda i, j: (i, j)),
                   pl.BlockSpec((tm, LANE), lambda i, j: (i, 0))),
        scratch_shapes=[pltpu.VMEM((tm, d), BF16)],
        compiler_params=pltpu.CompilerParams(dimension_semantics=("parallel", "arbitrary"),
                                             vmem_limit_bytes=VMEM_LIMIT),
        name="in_proj",
    )(x2, mod, g, wt, wt, w_small)


def _cum_kernel(s_ref, b_ref, o_ref, *, blk):
    s = s_ref.shape[1]
    tri = _lower_tri(blk)
    col = lax.broadcasted_iota(jnp.int32, (blk, LANE), 1)
    carry = jnp.zeros((1, LANE), F32)
    for r in range(s // blk):
        rows = slice(r * blk, (r + 1) * blk)
        z = s_ref[0, rows, :] + b_ref[...]
        cs = _tri_cumsum(tri, _log_sigmoid(z)) + carry
        carry = cs[blk - 1:blk, :]
        c2 = jnp.where(col < FOX_HEADS, cs * LOG2E, 0.0)
        hi = c2.astype(BF16).astype(F32)
        rem = c2 - hi
        mid = rem.astype(BF16).astype(F32)
        lo = (rem - mid).astype(BF16).astype(F32)
        pieces = hi + pltpu.roll(mid, FOX_HEADS, 1) + pltpu.roll(lo, 2 * FOX_HEADS, 1)
        o_ref[0, rows, :] = pieces.astype(BF16)


def _fox_cum(small3, bias_row, blk=256):
    b, s, _ = small3.shape
    return pl.pallas_call(
        functools.partial(_cum_kernel, blk=blk),
        out_shape=jax.ShapeDtypeStruct((b, s, LANE), BF16),
        grid=(b,),
        in_specs=[pl.BlockSpec((1, s, LANE), lambda i: (i, 0, 0)),
                  pl.BlockSpec((1, LANE), lambda i: (0, 0))],
        out_specs=pl.BlockSpec((1, s, LANE), lambda i: (i, 0, 0)),
        compiler_params=pltpu.CompilerParams(dimension_semantics=("parallel",),
                                             vmem_limit_bytes=VMEM_LIMIT),
        name="fox_cum",
    )(small3, bias_row)


class _FoxHead:
    def __init__(self, ka, vt, sa, sb, xa, xb, m, l, acc):
        self.ka, self.vt, self.m, self.l, self.acc = ka, vt, m, l, acc
        self.a = (sa, xa)
        self.b = (sb, xb)


def _fox_kernel(q_ref, k_ref, v_ref, a_ref, g_ref, *refs, tq, n_cast):
    o_ref = refs[n_cast]
    scratch = refs[2 * n_cast + 1:]
    heads = [_FoxHead(*scratch[n * 9:(n + 1) * 9]) for n in range(FOX_HEADS_PER_STEP)]
    t = pl.program_id(2)
    lanes = lambda n: slice(n * LANE, (n + 1) * LANE)

    @pl.when(t == 0)
    def _():
        for n, hd in enumerate(heads):
            hd.ka[:, 0:LANE] = k_ref[0, :, lanes(n)]
            hd.ka[:, LANE:] = a_ref[0]
            hd.vt[...] = v_ref[0, :, lanes(n)].T

    for src, dst in zip(refs[:n_cast], refs[n_cast + 1:2 * n_cast + 1]):
        dst[...] = src[...].astype(BF16)

    col = lax.broadcasted_iota(jnp.int32, (tq, LANE), 1)

    def tile_queries(half):
        rows = slice(half * tq, (half + 1) * tq)
        qas = []
        for n in range(len(heads)):
            h = pl.program_id(1) * FOX_HEADS_PER_STEP + n
            pick = (col == h) | (col == h + FOX_HEADS) | (col == h + 2 * FOX_HEADS)
            qas.append(jnp.concatenate([q_ref[0, rows, lanes(n)],
                                        jnp.where(pick, -1.0, 0.0).astype(BF16)], axis=1))
        return qas

    def reset_state():
        for hd in heads:
            hd.m[...] = jnp.full_like(hd.m, NEG_BIG)
            hd.l[...] = jnp.zeros_like(hd.l)
            hd.acc[...] = jnp.zeros_like(hd.acc)

    def logits_to(pick_buf, qas, blk):
        keys = pl.ds(pl.multiple_of(blk * tq, tq), tq)
        for hd, qa in zip(heads, qas):
            s_ref, max_ref = pick_buf(hd)
            st = _nt_dot(hd.ka[keys, :], qa)
            s_ref[...] = st
            max_ref[...] = jnp.max(st, axis=0, keepdims=True)

    def absorb(pick_buf, blk, masked):
        keys = pl.ds(pl.multiple_of(blk * tq, tq), tq)
        for hd in heads:
            s_ref, max_ref = pick_buf(hd)
            st = s_ref[...]
            if masked:
                kr = lax.broadcasted_iota(jnp.int32, (tq, tq), 0)
                qc = lax.broadcasted_iota(jnp.int32, (tq, tq), 1)
                st = jnp.where(kr <= qc, st, NEG_BIG)
                block_max = jnp.max(st, axis=0, keepdims=True)
            else:
                block_max = max_ref[...]
            m = hd.m[...]
            m_new = jnp.maximum(m, block_max)
            alpha = jnp.exp2(m - m_new)
            p = jnp.exp2(st - m_new)
            hd.m[...] = m_new
            hd.l[...] = alpha * hd.l[...] + jnp.sum(p, axis=0, keepdims=True)
            pv = jnp.dot(hd.vt[:, keys], p.astype(BF16), preferred_element_type=F32)
            hd.acc[...] = alpha * hd.acc[...] + pv

    def full_block_pairs(first, second, qas):
        def body(u, carry):
            logits_to(second, qas, 2 * u + 1)
            absorb(first, 2 * u, False)
            logits_to(first, qas, 2 * u + 2)
            absorb(second, 2 * u + 1, False)
            return carry
        lax.fori_loop(0, t, body, 0)

    def finish(half):
        for n, hd in enumerate(heads):
            ot = hd.acc[...] / hd.l[...]
            ot = ot * lax.rsqrt(jnp.mean(ot * ot, axis=0, keepdims=True) + EPS)
            o_ref[0, half * tq:(half + 1) * tq, lanes(n)] = (ot.T * g_ref[n]).astype(BF16)

    buf_a = lambda hd: hd.a
    buf_b = lambda hd: hd.b
    q_even, q_odd = tile_queries(0), tile_queries(1)

    reset_state()
    logits_to(buf_a, q_even, 0)
    full_block_pairs(buf_a, buf_b, q_even)
    logits_to(buf_b, q_odd, 0)
    absorb(buf_a, 2 * t, True)
    finish(0)

    reset_state()
    full_block_pairs(buf_b, buf_a, q_odd)
    logits_to(buf_a, q_odd, 2 * t + 1)
    absorb(buf_b, 2 * t, False)
    absorb(buf_a, 2 * t + 1, True)
    finish(1)


def _cast_slabs(f32_weights, n_steps, step_index):
    assert all(w.shape[0] % (n_steps * BF16_ROWS) == 0 for w in f32_weights)
    return [pl.BlockSpec((w.shape[0] // n_steps, w.shape[1]), step_index) for w in f32_weights]


def _fox_attn(proj3, pieces3, g_fox, f32_weights, tq=512):
    b, s, _ = proj3.shape
    nh = FOX_HEADS_PER_STEP
    groups = FOX_HEADS // nh
    width = nh * FOX_HEAD_DIM
    nt = s // (2 * tq)
    slabs = _cast_slabs(f32_weights, b * groups * nt,
                        lambda bi, gi, ti: ((bi * groups + gi) * nt + ti, 0))
    per_head = [pltpu.VMEM((s, 2 * LANE), BF16), pltpu.VMEM((FOX_HEAD_DIM, s), BF16),
                pltpu.VMEM((tq, tq), F32), pltpu.VMEM((tq, tq), F32),
                pltpu.VMEM((1, tq), F32), pltpu.VMEM((1, tq), F32),
                pltpu.VMEM((1, tq), F32), pltpu.VMEM((1, tq), F32),
                pltpu.VMEM((FOX_HEAD_DIM, tq), F32)]
    outs = pl.pallas_call(
        functools.partial(_fox_kernel, tq=tq, n_cast=len(f32_weights)),
        out_shape=[jax.ShapeDtypeStruct((b, s, FOX_HEADS * FOX_HEAD_DIM), BF16)]
        + [jax.ShapeDtypeStruct(w.shape, BF16) for w in f32_weights],
        grid=(b, groups, nt),
        in_specs=[pl.BlockSpec((1, 2 * tq, width), lambda bi, gi, ti: (bi, ti, gi)),
                  pl.BlockSpec((1, s, width), lambda bi, gi, ti: (bi, 0, groups + gi)),
                  pl.BlockSpec((1, s, width), lambda bi, gi, ti: (bi, 0, 2 * groups + gi)),
                  pl.BlockSpec((1, s, LANE), lambda bi, gi, ti: (bi, 0, 0)),
                  pl.BlockSpec((nh, 1, LANE), lambda bi, gi, ti: (gi, 0, 0))] + slabs,
        out_specs=[pl.BlockSpec((1, 2 * tq, width), lambda bi, gi, ti: (bi, ti, gi))] + slabs,
        scratch_shapes=per_head * nh,
        compiler_params=pltpu.CompilerParams(
            dimension_semantics=("parallel", "parallel", "arbitrary"),
            vmem_limit_bytes=VMEM_LIMIT_LARGE),
        name="fox_attn",
    )(proj3, proj3, proj3, pieces3, g_fox, *f32_weights)
    return outs[0], outs[1:]


def _gla_kernel(q_ref, k_ref, v_ref, r_ref, s_ref, wa_ref, ba_ref, gg_ref, *refs, ts, n_cast):
    o_ref, st_ref = refs[n_cast], refs[-1]
    for src, dst in zip(refs[:n_cast], refs[n_cast + 1:2 * n_cast + 1]):
        dst[...] = src[...].astype(BF16)

    @pl.when(pl.program_id(2) == 0)
    def _():
        st_ref[...] = jnp.zeros_like(st_ref)

    pre = jnp.dot(s_ref[0].astype(BF16), wa_ref[...], preferred_element_type=F32) + ba_ref[...]
    la = _log_sigmoid(pre)
    tri = _lower_tri(CHUNK, 1.0 / GLA_GATE_TEMP)
    chunks = [slice(c * CHUNK, (c + 1) * CHUNK) for c in range(ts // CHUNK)]
    cums = [_tri_cumsum(tri, la[sl]) for sl in chunks]
    tots = [cum[CHUNK - 1:CHUNK, :] for cum in cums]
    ups = []
    for sl, cum, tot in zip(chunks, cums, tots):
        kd = (k_ref[0, sl, :].astype(F32) * jnp.exp(tot - cum)).astype(BF16)
        ups.append(_tn_dot(v_ref[0, sl, :], kd))
    st = st_ref[...]
    states = []
    for tot, up in zip(tots, ups):
        st = st * jnp.exp(tot) + up
        states.append(st.astype(BF16))
    st_ref[...] = st
    outs = [_nt_dot(q_ref[0, sl, :], sb) for sl, sb in zip(chunks, states)]
    o = jnp.concatenate(outs, axis=0)
    qs = GLA_DK ** -0.5
    inv = qs * lax.rsqrt(qs * qs * jnp.mean(o * o, axis=-1, keepdims=True) + EPS)
    o = o * inv * gg_ref[0]
    r = r_ref[0].astype(F32)
    o_ref[0] = (o * (r * _sigmoid(r))).astype(BF16)


def _gla(proj3, small3, wa_pad, ba_row, g_gla, f32_weights, ts=2048):
    b, s, _ = proj3.shape
    q_blk = 3 * FOX_HEADS
    k_blk = q_blk + GLA_HEADS
    v_blk = (k_blk + GLA_HEADS) // 2
    r_blk = v_blk + GLA_HEADS
    nt = s // ts
    slabs = _cast_slabs(f32_weights, b * GLA_HEADS * nt,
                        lambda bi, gi, ti: ((bi * GLA_HEADS + gi) * nt + ti, 0))
    outs = pl.pallas_call(
        functools.partial(_gla_kernel, ts=ts, n_cast=len(f32_weights)),
        out_shape=[jax.ShapeDtypeStruct((b, s, GLA_HEADS * GLA_DV), BF16)]
        + [jax.ShapeDtypeStruct(w.shape, BF16) for w in f32_weights],
        grid=(b, GLA_HEADS, nt),
        in_specs=[pl.BlockSpec((1, ts, GLA_DK), lambda bi, gi, ti: (bi, ti, q_blk + gi)),
                  pl.BlockSpec((1, ts, GLA_DK), lambda bi, gi, ti: (bi, ti, k_blk + gi)),
                  pl.BlockSpec((1, ts, GLA_DV), lambda bi, gi, ti: (bi, ti, v_blk + gi)),
                  pl.BlockSpec((1, ts, GLA_DV), lambda bi, gi, ti: (bi, ti, r_blk + gi)),
                  pl.BlockSpec((1, ts, LANE), lambda bi, gi, ti: (bi, ti, 0)),
                  pl.BlockSpec((LANE, GLA_DK), lambda bi, gi, ti: (0, gi)),
                  pl.BlockSpec((1, GLA_DK), lambda bi, gi, ti: (0, gi)),
                  pl.BlockSpec((1, 1, GLA_DV), lambda bi, gi, ti: (gi, 0, 0))] + slabs,
        out_specs=[pl.BlockSpec((1, ts, GLA_DV), lambda bi, gi, ti: (bi, ti, gi))] + slabs,
        scratch_shapes=[pltpu.VMEM((GLA_DV, GLA_DK), F32)],
        compiler_params=pltpu.CompilerParams(
            dimension_semantics=("parallel", "parallel", "arbitrary"),
            vmem_limit_bytes=VMEM_LIMIT),
        name="gla",
    )(proj3, proj3, proj3, proj3, small3, wa_pad, ba_row, g_gla, *f32_weights)
    return outs[0], outs[1:]


def _outproj_kernel(fox_ref, gla_ref, w_ref, x_ref, mod_ref, g_ref, o_ref):
    wide = fox_ref.shape[1]
    gain = mod_ref[0, 2:3, :] * g_ref[...]
    half = x_ref.shape[0] // 2
    for rows in (slice(0, half), slice(half, 2 * half)):
        y = (jnp.dot(fox_ref[rows, :], w_ref[0:wide, :], preferred_element_type=F32)
             + jnp.dot(gla_ref[rows, :], w_ref[wide:, :], preferred_element_type=F32))
        inv = lax.rsqrt(jnp.mean(y * y, axis=-1, keepdims=True) + EPS)
        o_ref[rows, :] = x_ref[rows, :] + y * inv * gain


def _out_proj(fox2, gla2, w_out, x2, mod, g, rows_per_batch, tm=1024):
    t, d = x2.shape
    wide = fox2.shape[1]
    tiles_per_batch = rows_per_batch // tm
    return pl.pallas_call(
        _outproj_kernel,
        out_shape=jax.ShapeDtypeStruct((t, d), F32),
        grid=(t // tm,),
        in_specs=[pl.BlockSpec((tm, wide), lambda i: (i, 0)),
                  pl.BlockSpec((tm, wide), lambda i: (i, 0)),
                  pl.BlockSpec((2 * wide, d), lambda i: (0, 0), pipeline_mode=pl.Buffered(1)),
                  pl.BlockSpec((tm, d), lambda i: (i, 0)),
                  pl.BlockSpec((1, N_MOD, d), lambda i: (i // tiles_per_batch, 0, 0)),
                  pl.BlockSpec((1, d), lambda i: (0, 0))],
        out_specs=pl.BlockSpec((tm, d), lambda i: (i, 0)),
        compiler_params=pltpu.CompilerParams(dimension_semantics=("parallel",),
                                             vmem_limit_bytes=VMEM_LIMIT),
        name="out_proj",
    )(fox2, gla2, w_out, x2, mod, g)


def _mlp_kernel(x_ref, mod_ref, gpre_ref, gpost_ref, w1_ref, w2_ref, o_ref, h_scr):
    f = pl.program_id(1)
    last = pl.num_programs(1) - 1
    half = x_ref.shape[0] // 2
    halves = (slice(0, half), slice(half, 2 * half))

    def hidden_block(hb):
        u = jnp.maximum(jnp.dot(hb, w1_ref[...], preferred_element_type=F32), 0.0)
        return jnp.dot((u * u).astype(BF16), w2_ref[...], preferred_element_type=F32)

    @pl.when(f == 0)
    def _():
        gain = gpre_ref[...] * (1.0 + mod_ref[0, 4:5, :])
        for rows in halves:
            x = x_ref[rows, :]
            inv = lax.rsqrt(jnp.mean(x * x, axis=-1, keepdims=True) + EPS)
            hb = (x * inv * gain + mod_ref[0, 3:4, :]).astype(BF16)
            h_scr[rows, :] = hb
            o_ref[rows, :] = hidden_block(hb)

    @pl.when((f > 0) & (f < last))
    def _():
        for rows in halves:
            o_ref[rows, :] += hidden_block(h_scr[rows, :])

    @pl.when(f == last)
    def _():
        gain = mod_ref[0, 5:6, :] * gpost_ref[...]
        for rows in halves:
            y = o_ref[rows, :] + hidden_block(h_scr[rows, :])
            inv = lax.rsqrt(jnp.mean(y * y, axis=-1, keepdims=True) + EPS)
            o_ref[rows, :] = x_ref[rows, :] + y * inv * gain


def _mlp(x1, mod, g_pre, g_post, w1, w2, rows_per_batch, tm=1024, tf=1024):
    t, d = x1.shape
    dff = w1.shape[1]
    tiles_per_batch = rows_per_batch // tm
    return pl.pallas_call(
        _mlp_kernel,
        out_shape=jax.ShapeDtypeStruct((t, d), F32),
        grid=(t // tm, dff // tf),
        in_specs=[pl.BlockSpec((tm, d), lambda i, f: (i, 0)),
                  pl.BlockSpec((1, N_MOD, d), lambda i, f: (i // tiles_per_batch, 0, 0)),
                  pl.BlockSpec((1, d), lambda i, f: (0, 0)),
                  pl.BlockSpec((1, d), lambda i, f: (0, 0)),
                  pl.BlockSpec((d, tf), lambda i, f: (0, f)),
                  pl.BlockSpec((tf, d), lambda i, f: (f, 0))],
        out_specs=pl.BlockSpec((tm, d), lambda i, f: (i, 0)),
        scratch_shapes=[pltpu.VMEM((tm, d), BF16)],
        compiler_params=pltpu.CompilerParams(dimension_semantics=("parallel", "arbitrary"),
                                             vmem_limit_bytes=VMEM_LIMIT_LARGE),
        name="mlp",
    )(x1, mod, g_pre, g_post, w1, w2)


def _transposed_in_proj_weight(w_in):
    fox_end = 3 * FOX_HEADS * FOX_HEAD_DIM
    gla_start = fox_end + FOX_HEADS
    gla_end = gla_start + 2 * GLA_HEADS * GLA_DK + GLA_HEADS * GLA_DV
    gate_start = gla_end + GLA_GATE_RANK
    wt = jnp.swapaxes(w_in, 0, 1).astype(BF16)
    pad = jnp.zeros((LANE - FOX_HEADS - GLA_GATE_RANK, w_in.shape[0]), BF16)
    small = jnp.concatenate([wt[fox_end:gla_start], wt[gla_end:gate_start], pad], axis=0)
    return wt, small


def kernel(x, c, w_ada, b_ada, g_pre_mix, g_post_mix, w_in, b_fgate, w_gla_a2, b_gla_a2,
           g_fox_out, g_gla_out, w_out, g_pre_mlp, g_post_mlp, w_mlp_in, w_mlp_out):
    b, s, d = x.shape
    depth = w_ada.shape[0]
    row = lambda v: v.reshape(1, -1)
    for i in range(depth):
        c_pad = jnp.concatenate([c, jnp.zeros((8 - b, d), c.dtype)], axis=0)
        mod = _ada(c_pad, w_ada[i], row(b_ada[i]))[:b].reshape(b, N_MOD, d)

        wt, w_small = _transposed_in_proj_weight(w_in[i])
        x2 = x.reshape(b * s, d)
        proj, small = _in_proj(x2, mod, row(g_pre_mix[i]), wt, w_small, s)
        proj3 = proj.reshape(b, s, -1)
        small3 = small.reshape(b, s, LANE)

        fbias = jnp.concatenate([b_fgate[i], jnp.zeros((LANE - FOX_HEADS,), F32)]).reshape(1, LANE)
        fox, (w_out_b, w_mlp_out_b) = _fox_attn(
            proj3, _fox_cum(small3, fbias), g_fox_out[i].reshape(FOX_HEADS, 1, FOX_HEAD_DIM),
            [w_out[i], w_mlp_out[i]])

        kw = GLA_HEADS * GLA_DK
        wa_pad = jnp.concatenate(
            [jnp.zeros((FOX_HEADS, kw), F32), w_gla_a2[i],
             jnp.zeros((LANE - FOX_HEADS - GLA_GATE_RANK, kw), F32)], axis=0).astype(BF16)
        gla, (w_mlp_in_b,) = _gla(
            proj3, small3, wa_pad, row(b_gla_a2[i]), g_gla_out[i].reshape(GLA_HEADS, 1, GLA_DV),
            [w_mlp_in[i]])

        x1 = _out_proj(fox.reshape(b * s, -1), gla.reshape(b * s, -1), w_out_b,
                       x2, mod, row(g_post_mix[i]), s)
        x2 = _mlp(x1, mod, row(g_pre_mlp[i]), row(g_post_mlp[i]), w_mlp_in_b, w_mlp_out_b, s)
        x = x2.reshape(b, s, d)
    return x
```

```python
import functools

import jax
import jax.numpy as jnp
from jax import lax
from jax.experimental import pallas as pl
from jax.experimental.pallas import tpu as pltpu

F32 = jnp.float32
BF16 = jnp.bfloat16

EPS = 1e-6
N_MOD = 6
FOX_HEADS = 8
FOX_HEAD_DIM = 128
GLA_HEADS = 4
GLA_DK = 128
GLA_DV = 256
GLA_GATE_RANK = 16
GLA_GATE_TEMP = 16.0
CHUNK = 64
LANE = 128
BF16_ROWS = 16
VMEM_LIMIT = 56 * 1024 * 1024
VMEM_LIMIT_LARGE = 60 * 1024 * 1024

NEG_BIG = -1e30
LOG2E = 1.4426950408889634
FOX_HEADS_PER_STEP = 4


def _nt_dot(a, b):
    return lax.dot_general(a, b, (((1,), (1,)), ((), ())), preferred_element_type=F32)


def _tn_dot(a, b):
    return lax.dot_general(a, b, (((0,), (0,)), ((), ())), preferred_element_type=F32)


def _log_sigmoid(z):
    return jnp.minimum(z, 0.0) - jnp.log(1.0 + jnp.exp(-jnp.abs(z)))


def _sigmoid(z):
    return 1.0 / (1.0 + jnp.exp(-z))


def _tri_cumsum(tri, v):
    hi = v.astype(BF16)
    lo = (v - hi.astype(F32)).astype(BF16)
    return (jnp.dot(tri, hi, preferred_element_type=F32)
            + jnp.dot(tri, lo, preferred_element_type=F32))


def _lower_tri(n, value=1.0):
    r = lax.broadcasted_iota(jnp.int32, (n, n), 0)
    c = lax.broadcasted_iota(jnp.int32, (n, n), 1)
    return jnp.where(r >= c, value, 0.0).astype(BF16)


def _ada_kernel(c_ref, w_ref, b_ref, o_ref):
    c = c_ref[...]
    act = (c * _sigmoid(c)).astype(BF16)
    o_ref[...] = jnp.dot(act, w_ref[...].astype(BF16), preferred_element_type=F32) + b_ref[...]


def _ada(c_pad, w_ada, b_ada, tn=1024):
    m, d = c_pad.shape
    n = w_ada.shape[1]
    return pl.pallas_call(
        _ada_kernel,
        out_shape=jax.ShapeDtypeStruct((m, n), F32),
        grid=(n // tn,),
        in_specs=[pl.BlockSpec((m, d), lambda j: (0, 0)),
                  pl.BlockSpec((d, tn), lambda j: (0, j)),
                  pl.BlockSpec((1, tn), lambda j: (0, j))],
        out_specs=pl.BlockSpec((m, tn), lambda j: (0, j)),
        compiler_params=pltpu.CompilerParams(dimension_semantics=("parallel",),
                                             vmem_limit_bytes=VMEM_LIMIT),
        name="ada",
    )(c_pad, w_ada, b_ada)


def _inproj_kernel(x_ref, mod_ref, g_ref, wlo_ref, whi_ref, ws_ref, o_ref, os_ref, h_scr):
    j = pl.program_id(1)
    half = wlo_ref.shape[0]

    @pl.when(j == 0)
    def _():
        x = x_ref[...]
        inv = lax.rsqrt(jnp.mean(x * x, axis=-1, keepdims=True) + EPS)
        gain = g_ref[...] * (1.0 + mod_ref[0, 1:2, :])
        hb = (x * inv * gain + mod_ref[0, 0:1, :]).astype(BF16)
        h_scr[...] = hb
        os_ref[...] = _nt_dot(hb, ws_ref[...])
        o_ref[:, :half] = (_nt_dot(hb, wlo_ref[...]) * (FOX_HEAD_DIM ** -0.5 * LOG2E)).astype(BF16)
        o_ref[:, half:] = _nt_dot(hb, whi_ref[...]).astype(BF16)

    @pl.when(j > 0)
    def _():
        o_ref[:, :half] = _nt_dot(h_scr[...], wlo_ref[...]).astype(BF16)
        o_ref[:, half:] = _nt_dot(h_scr[...], whi_ref[...]).astype(BF16)


def _in_proj(x2, mod, g, wt, w_small, rows_per_batch, tm=1024):
    t, d = x2.shape
    half = FOX_HEADS * FOX_HEAD_DIM
    n = wt.shape[0] - FOX_HEADS - GLA_GATE_RANK
    tn = 2 * half
    assert n % tn == 0 and GLA_HEADS * (2 * GLA_DK + GLA_DV) == tn
    tiles_per_batch = rows_per_batch // tm
    lo_start = lambda j: pl.multiple_of(j * tn + jnp.where(j >= 2, FOX_HEADS, 0), FOX_HEADS)
    hi_start = lambda j: pl.multiple_of(j * tn + half + jnp.where(j >= 1, FOX_HEADS, 0)
                                        + jnp.where(j >= 2, GLA_GATE_RANK, 0), FOX_HEADS)
    return pl.pallas_call(
        _inproj_kernel,
        out_shape=(jax.ShapeDtypeStruct((t, n), BF16),
                   jax.ShapeDtypeStruct((t, LANE), F32)),
        grid=(t // tm, n // tn),
        in_specs=[pl.BlockSpec((tm, d), lambda i, j: (i, 0)),
                  pl.BlockSpec((1, N_MOD, d), lambda i, j: (i // tiles_per_batch, 0, 0)),
                  pl.BlockSpec((1, d), lambda i, j: (0, 0)),
                  pl.BlockSpec((pl.Element(half), pl.Element(d)), lambda i, j: (lo_start(j), 0)),
                  pl.BlockSpec((pl.Element(half), pl.Element(d)), lambda i, j: (hi_start(j), 0)),
                  pl.BlockSpec((LANE, d), lambda i, j: (0, 0))],
        out_specs=(pl.BlockSpec((tm, tn), lambda i, j: (i, j)),
                   pl.BlockSpec((tm, LANE), lambda i, j: (i, 0))),
        scratch_shapes=[pltpu.VMEM((tm, d), BF16)],
        compiler_params=pltpu.CompilerParams(dimension_semantics=("parallel", "arbitrary"),
                                             vmem_limit_bytes=VMEM_LIMIT),
        name="in_proj",
    )(x2, mod, g, wt, wt, w_small)


def _cum_kernel(s_ref, b_ref, o_ref, *, blk):
    s = s_ref.shape[1]
    tri = _lower_tri(blk)
    col = lax.broadcasted_iota(jnp.int32, (blk, LANE), 1)
    carry = jnp.zeros((1, LANE), F32)
    for r in range(s // blk):
        rows = slice(r * blk, (r + 1) * blk)
        z = s_ref[0, rows, :] + b_ref[...]
        cs = _tri_cumsum(tri, _log_sigmoid(z)) + carry
        carry = cs[blk - 1:blk, :]
        c2 = jnp.where(col < FOX_HEADS, cs * LOG2E, 0.0)
        hi = c2.astype(BF16).astype(F32)
        rem = c2 - hi
        mid = rem.astype(BF16).astype(F32)
        lo = (rem - mid).astype(BF16).astype(F32)
        pieces = hi + pltpu.roll(mid, FOX_HEADS, 1) + pltpu.roll(lo, 2 * FOX_HEADS, 1)
        o_ref[0, rows, :] = pieces.astype(BF16)


def _fox_cum(small3, bias_row, blk=256):
    b, s, _ = small3.shape
    return pl.pallas_call(
        functools.partial(_cum_kernel, blk=blk),
        out_shape=jax.ShapeDtypeStruct((b, s, LANE), BF16),
        grid=(b,),
        in_specs=[pl.BlockSpec((1, s, LANE), lambda i: (i, 0, 0)),
                  pl.BlockSpec((1, LANE), lambda i: (0, 0))],
        out_specs=pl.BlockSpec((1, s, LANE), lambda i: (i, 0, 0)),
        compiler_params=pltpu.CompilerParams(dimension_semantics=("parallel",),
                                             vmem_limit_bytes=VMEM_LIMIT),
        name="fox_cum",
    )(small3, bias_row)


class _FoxHead:
    def __init__(self, ka, vt, sa, sb, xa, xb, m, l, acc):
        self.ka, self.vt, self.m, self.l, self.acc = ka, vt, m, l, acc
        self.a = (sa, xa)
        self.b = (sb, xb)


def _fox_kernel(q_ref, k_ref, v_ref, a_ref, g_ref, *refs, tq, n_cast):
    o_ref = refs[n_cast]
    scratch = refs[2 * n_cast + 1:]
    per_head = len(scratch) // FOX_HEADS_PER_STEP
    heads = [_FoxHead(*scratch[n * per_head:(n + 1) * per_head])
             for n in range(FOX_HEADS_PER_STEP)]
    t = pl.program_id(2)
    lanes = lambda n: slice(n * LANE, (n + 1) * LANE)

    @pl.when(t == 0)
    def _():
        for n, hd in enumerate(heads):
            hd.ka[:, 0:LANE] = k_ref[0, :, lanes(n)]
            hd.ka[:, LANE:] = a_ref[0]
            hd.vt[...] = v_ref[0, :, lanes(n)].T

    for src, dst in zip(refs[:n_cast], refs[n_cast + 1:2 * n_cast + 1]):
        dst[...] = src[...].astype(BF16)

    col = lax.broadcasted_iota(jnp.int32, (tq, LANE), 1)

    def tile_queries(half):
        rows = slice(half * tq, (half + 1) * tq)
        qas = []
        for n in range(len(heads)):
            h = pl.program_id(1) * FOX_HEADS_PER_STEP + n
            pick = (col == h) | (col == h + FOX_HEADS) | (col == h + 2 * FOX_HEADS)
            qas.append(jnp.concatenate([q_ref[0, rows, lanes(n)],
                                        jnp.where(pick, -1.0, 0.0).astype(BF16)], axis=1))
        return qas

    def reset_state():
        for hd in heads:
            hd.m[...] = jnp.full_like(hd.m, NEG_BIG)
            hd.l[...] = jnp.zeros_like(hd.l)
            hd.acc[...] = jnp.zeros_like(hd.acc)

    def logits_to(pick_buf, qas, blk):
        keys = pl.ds(pl.multiple_of(blk * tq, tq), tq)
        for hd, qa in zip(heads, qas):
            s_ref, max_ref = pick_buf(hd)
            st = _nt_dot(hd.ka[keys, :], qa)
            s_ref[...] = st
            max_ref[...] = jnp.max(st, axis=0, keepdims=True)

    def absorb(pick_buf, blk, masked):
        keys = pl.ds(pl.multiple_of(blk * tq, tq), tq)
        for hd in heads:
            s_ref, max_ref = pick_buf(hd)
            st = s_ref[...]
            if masked:
                kr = lax.broadcasted_iota(jnp.int32, (tq, tq), 0)
                qc = lax.broadcasted_iota(jnp.int32, (tq, tq), 1)
                st = jnp.where(kr <= qc, st, NEG_BIG)
                block_max = jnp.max(st, axis=0, keepdims=True)
            else:
                block_max = max_ref[...]
            m = hd.m[...]
            m_new = jnp.maximum(m, block_max)
            alpha = jnp.exp2(m - m_new)
            p = jnp.exp2(st - m_new)
            hd.m[...] = m_new
            hd.l[...] = alpha * hd.l[...] + jnp.sum(p, axis=0, keepdims=True)
            pv = jnp.dot(hd.vt[:, keys], p.astype(BF16), preferred_element_type=F32)
            hd.acc[...] = alpha * hd.acc[...] + pv

    def full_block_pairs(first, second, qas):
        def body(u, carry):
            logits_to(second, qas, 2 * u + 1)
            absorb(first, 2 * u, False)
            logits_to(first, qas, 2 * u + 2)
            absorb(second, 2 * u + 1, False)
            return carry
        lax.fori_loop(0, t, body, 0)

    def finish(half):
        for n, hd in enumerate(heads):
            ot = hd.acc[...] / hd.l[...]
            ot = ot * lax.rsqrt(jnp.mean(ot * ot, axis=0, keepdims=True) + EPS)
            o_ref[0, half * tq:(half + 1) * tq, lanes(n)] = (ot.T * g_ref[n]).astype(BF16)

    buf_a = lambda hd: hd.a
    buf_b = lambda hd: hd.b
    q_even, q_odd = tile_queries(0), tile_queries(1)

    reset_state()
    logits_to(buf_a, q_even, 0)
    full_block_pairs(buf_a, buf_b, q_even)
    logits_to(buf_b, q_odd, 0)
    absorb(buf_a, 2 * t, True)
    finish(0)

    reset_state()
    full_block_pairs(buf_b, buf_a, q_odd)
    logits_to(buf_a, q_odd, 2 * t + 1)
    absorb(buf_b, 2 * t, False)
    absorb(buf_a, 2 * t + 1, True)
    finish(1)


def _cast_slabs(f32_weights, n_steps, step_index):
    assert all(w.shape[0] % (n_steps * BF16_ROWS) == 0 for w in f32_weights)
    return [pl.BlockSpec((w.shape[0] // n_steps, w.shape[1]), step_index) for w in f32_weights]


def _fox_attn(proj3, pieces3, g_fox, f32_weights, tq=512):
    b, s, _ = proj3.shape
    nh = FOX_HEADS_PER_STEP
    groups = FOX_HEADS // nh
    width = nh * FOX_HEAD_DIM
    nt = s // (2 * tq)
    slabs = _cast_slabs(f32_weights, b * groups * nt,
                        lambda bi, gi, ti: ((bi * groups + gi) * nt + ti, 0))
    per_head = [pltpu.VMEM((s, 2 * LANE), BF16), pltpu.VMEM((FOX_HEAD_DIM, s), BF16),
                pltpu.VMEM((tq, tq), F32), pltpu.VMEM((tq, tq), F32),
                pltpu.VMEM((1, tq), F32), pltpu.VMEM((1, tq), F32),
                pltpu.VMEM((1, tq), F32), pltpu.VMEM((1, tq), F32),
                pltpu.VMEM((FOX_HEAD_DIM, tq), F32)]
    outs = pl.pallas_call(
        functools.partial(_fox_kernel, tq=tq, n_cast=len(f32_weights)),
        out_shape=[jax.ShapeDtypeStruct((b, s, FOX_HEADS * FOX_HEAD_DIM), BF16)]
        + [jax.ShapeDtypeStruct(w.shape, BF16) for w in f32_weights],
        grid=(b, groups, nt),
        in_specs=[pl.BlockSpec((1, 2 * tq, width), lambda bi, gi, ti: (bi, ti, gi)),
                  pl.BlockSpec((1, s, width), lambda bi, gi, ti: (bi, 0, groups + gi)),
                  pl.BlockSpec((1, s, width), lambda bi, gi, ti: (bi, 0, 2 * groups + gi)),
                  pl.BlockSpec((1, s, LANE), lambda bi, gi, ti: (bi, 0, 0)),
                  pl.BlockSpec((nh, 1, LANE), lambda bi, gi, ti: (gi, 0, 0))] + slabs,
        out_specs=[pl.BlockSpec((1, 2 * tq, width), lambda bi, gi, ti: (bi, ti, gi))] + slabs,
        scratch_shapes=per_head * nh,
        compiler_params=pltpu.CompilerParams(
            dimension_semantics=("parallel", "parallel", "arbitrary"),
            vmem_limit_bytes=VMEM_LIMIT_LARGE),
        name="fox_attn",
    )(proj3, proj3, proj3, pieces3, g_fox, *f32_weights)
    return outs[0], outs[1:]


def _gla_kernel(q_ref, k_ref, v_ref, r_ref, s_ref, wa_ref, ba_ref, gg_ref, *refs, ts, n_cast):
    o_ref, st_ref = refs[n_cast], refs[-1]
    for src, dst in zip(refs[:n_cast], refs[n_cast + 1:2 * n_cast + 1]):
        dst[...] = src[...].astype(BF16)

    @pl.when(pl.program_id(2) == 0)
    def _():
        st_ref[...] = jnp.zeros_like(st_ref)

    pre = jnp.dot(s_ref[0].astype(BF16), wa_ref[...], preferred_element_type=F32) + ba_ref[...]
    la = _log_sigmoid(pre)
    tri = _lower_tri(CHUNK, 1.0 / GLA_GATE_TEMP)
    chunks = [slice(c * CHUNK, (c + 1) * CHUNK) for c in range(ts // CHUNK)]
    cums = [_tri_cumsum(tri, la[sl]) for sl in chunks]
    tots = [cum[CHUNK - 1:CHUNK, :] for cum in cums]
    ups = []
    for sl, cum, tot in zip(chunks, cums, tots):
        kd = (k_ref[0, sl, :].astype(F32) * jnp.exp(tot - cum)).astype(BF16)
        ups.append(_tn_dot(v_ref[0, sl, :], kd))
    st = st_ref[...]
    states = []
    for tot, up in zip(tots, ups):
        st = st * jnp.exp(tot) + up
        states.append(st.astype(BF16))
    st_ref[...] = st
    outs = [_nt_dot(q_ref[0, sl, :], sb) for sl, sb in zip(chunks, states)]
    o = jnp.concatenate(outs, axis=0)
    qs = GLA_DK ** -0.5
    inv = qs * lax.rsqrt(qs * qs * jnp.mean(o * o, axis=-1, keepdims=True) + EPS)
    o = o * inv * gg_ref[0]
    r = r_ref[0].astype(F32)
    o_ref[0] = (o * (r * _sigmoid(r))).astype(BF16)


def _gla(proj3, small3, wa_pad, ba_row, g_gla, f32_weights, ts=4096):
    b, s, _ = proj3.shape
    q_blk = 3 * FOX_HEADS
    k_blk = q_blk + GLA_HEADS
    v_blk = (k_blk + GLA_HEADS) // 2
    r_blk = v_blk + GLA_HEADS
    nt = s // ts
    slabs = _cast_slabs(f32_weights, b * GLA_HEADS * nt,
                        lambda bi, gi, ti: ((bi * GLA_HEADS + gi) * nt + ti, 0))
    outs = pl.pallas_call(
        functools.partial(_gla_kernel, ts=ts, n_cast=len(f32_weights)),
        out_shape=[jax.ShapeDtypeStruct((b, s, GLA_HEADS * GLA_DV), BF16)]
        + [jax.ShapeDtypeStruct(w.shape, BF16) for w in f32_weights],
        grid=(b, GLA_HEADS, nt),
        in_specs=[pl.BlockSpec((1, ts, GLA_DK), lambda bi, gi, ti: (bi, ti, q_blk + gi)),
                  pl.BlockSpec((1, ts, GLA_DK), lambda bi, gi, ti: (bi, ti, k_blk + gi)),
                  pl.BlockSpec((1, ts, GLA_DV), lambda bi, gi, ti: (bi, ti, v_blk + gi)),
                  pl.BlockSpec((1, ts, GLA_DV), lambda bi, gi, ti: (bi, ti, r_blk + gi)),
                  pl.BlockSpec((1, ts, LANE), lambda bi, gi, ti: (bi, ti, 0)),
                  pl.BlockSpec((LANE, GLA_DK), lambda bi, gi, ti: (0, gi)),
                  pl.BlockSpec((1, GLA_DK), lambda bi, gi, ti: (0, gi)),
                  pl.BlockSpec((1, 1, GLA_DV), lambda bi, gi, ti: (gi, 0, 0))] + slabs,
        out_specs=[pl.BlockSpec((1, ts, GLA_DV), lambda bi, gi, ti: (bi, ti, gi))] + slabs,
        scratch_shapes=[pltpu.VMEM((GLA_DV, GLA_DK), F32)],
        compiler_params=pltpu.CompilerParams(
            dimension_semantics=("parallel", "parallel", "arbitrary"),
            vmem_limit_bytes=VMEM_LIMIT),
        name="gla",
    )(proj3, proj3, proj3, proj3, small3, wa_pad, ba_row, g_gla, *f32_weights)
    return outs[0], outs[1:]


def _outproj_kernel(fox_ref, gla_ref, w_ref, x_ref, mod_ref, g_ref, o_ref):
    wide = fox_ref.shape[1]
    gain = mod_ref[0, 2:3, :] * g_ref[...]
    half = x_ref.shape[0] // 2
    for rows in (slice(0, half), slice(half, 2 * half)):
        y = (jnp.dot(fox_ref[rows, :], w_ref[0:wide, :], preferred_element_type=F32)
             + jnp.dot(gla_ref[rows, :], w_ref[wide:, :], preferred_element_type=F32))
        inv = lax.rsqrt(jnp.mean(y * y, axis=-1, keepdims=True) + EPS)
        o_ref[rows, :] = x_ref[rows, :] + y * inv * gain


def _out_proj(fox2, gla2, w_out, x2, mod, g, rows_per_batch, tm=1024):
    t, d = x2.shape
    wide = fox2.shape[1]
    tiles_per_batch = rows_per_batch // tm
    return pl.pallas_call(
        _outproj_kernel,
        out_shape=jax.ShapeDtypeStruct((t, d), F32),
        grid=(t // tm,),
        in_specs=[pl.BlockSpec((tm, wide), lambda i: (i, 0)),
                  pl.BlockSpec((tm, wide), lambda i: (i, 0)),
                  pl.BlockSpec((2 * wide, d), lambda i: (0, 0), pipeline_mode=pl.Buffered(1)),
                  pl.BlockSpec((tm, d), lambda i: (i, 0)),
                  pl.BlockSpec((1, N_MOD, d), lambda i: (i // tiles_per_batch, 0, 0)),
                  pl.BlockSpec((1, d), lambda i: (0, 0))],
        out_specs=pl.BlockSpec((tm, d), lambda i: (i, 0)),
        compiler_params=pltpu.CompilerParams(dimension_semantics=("parallel",),
                                             vmem_limit_bytes=VMEM_LIMIT),
        name="out_proj",
    )(fox2, gla2, w_out, x2, mod, g)


def _mlp_kernel(x_ref, mod_ref, gpre_ref, gpost_ref, w1_ref, w2_ref, o_ref, h_scr):
    f = pl.program_id(1)
    last = pl.num_programs(1) - 1
    half = x_ref.shape[0] // 2
    halves = (slice(0, half), slice(half, 2 * half))

    def hidden_block(hb):
        u = jnp.maximum(jnp.dot(hb, w1_ref[...], preferred_element_type=F32), 0.0)
        return jnp.dot((u * u).astype(BF16), w2_ref[...], preferred_element_type=F32)

    @pl.when(f == 0)
    def _():
        gain = gpre_ref[...] * (1.0 + mod_ref[0, 4:5, :])
        for rows in halves:
            x = x_ref[rows, :]
            inv = lax.rsqrt(jnp.mean(x * x, axis=-1, keepdims=True) + EPS)
            hb = (x * inv * gain + mod_ref[0, 3:4, :]).astype(BF16)
            h_scr[rows, :] = hb
            o_ref[rows, :] = hidden_block(hb)

    @pl.when((f > 0) & (f < last))
    def _():
        for rows in halves:
            o_ref[rows, :] += hidden_block(h_scr[rows, :])

    @pl.when(f == last)
    def _():
        gain = mod_ref[0, 5:6, :] * gpost_ref[...]
        for rows in halves:
            y = o_ref[rows, :] + hidden_block(h_scr[rows, :])
            inv = lax.rsqrt(jnp.mean(y * y, axis=-1, keepdims=True) + EPS)
            o_ref[rows, :] = x_ref[rows, :] + y * inv * gain


def _mlp(x1, mod, g_pre, g_post, w1, w2, rows_per_batch, tm=1024, tf=1024):
    t, d = x1.shape
    dff = w1.shape[1]
    tiles_per_batch = rows_per_batch // tm
    return pl.pallas_call(
        _mlp_kernel,
        out_shape=jax.ShapeDtypeStruct((t, d), F32),
        grid=(t // tm, dff // tf),
        in_specs=[pl.BlockSpec((tm, d), lambda i, f: (i, 0)),
                  pl.BlockSpec((1, N_MOD, d), lambda i, f: (i // tiles_per_batch, 0, 0)),
                  pl.BlockSpec((1, d), lambda i, f: (0, 0)),
                  pl.BlockSpec((1, d), lambda i, f: (0, 0)),
                  pl.BlockSpec((d, tf), lambda i, f: (0, f)),
                  pl.BlockSpec((tf, d), lambda i, f: (f, 0))],
        out_specs=pl.BlockSpec((tm, d), lambda i, f: (i, 0)),
        scratch_shapes=[pltpu.VMEM((tm, d), BF16)],
        compiler_params=pltpu.CompilerParams(dimension_semantics=("parallel", "arbitrary"),
                                             vmem_limit_bytes=VMEM_LIMIT_LARGE),
        name="mlp",
    )(x1, mod, g_pre, g_post, w1, w2)


def _transposed_in_proj_weight(w_in):
    fox_end = 3 * FOX_HEADS * FOX_HEAD_DIM
    gla_start = fox_end + FOX_HEADS
    gla_end = gla_start + 2 * GLA_HEADS * GLA_DK + GLA_HEADS * GLA_DV
    gate_start = gla_end + GLA_GATE_RANK
    wt = jnp.swapaxes(w_in, 0, 1).astype(BF16)
    pad = jnp.zeros((LANE - FOX_HEADS - GLA_GATE_RANK, w_in.shape[0]), BF16)
    small = jnp.concatenate([wt[fox_end:gla_start], wt[gla_end:gate_start], pad], axis=0)
    return wt, small


def kernel(x, c, w_ada, b_ada, g_pre_mix, g_post_mix, w_in, b_fgate, w_gla_a2, b_gla_a2,
           g_fox_out, g_gla_out, w_out, g_pre_mlp, g_post_mlp, w_mlp_in, w_mlp_out):
    b, s, d = x.shape
    depth = w_ada.shape[0]
    row = lambda v: v.reshape(1, -1)
    for i in range(depth):
        c_pad = jnp.concatenate([c, jnp.zeros((8 - b, d), c.dtype)], axis=0)
        mod = _ada(c_pad, w_ada[i], row(b_ada[i]))[:b].reshape(b, N_MOD, d)

        wt, w_small = _transposed_in_proj_weight(w_in[i])
        x2 = x.reshape(b * s, d)
        proj, small = _in_proj(x2, mod, row(g_pre_mix[i]), wt, w_small, s)
        proj3 = proj.reshape(b, s, -1)
        small3 = small.reshape(b, s, LANE)

        fbias = jnp.concatenate([b_fgate[i], jnp.zeros((LANE - FOX_HEADS,), F32)]).reshape(1, LANE)
        fox, (w_out_b, w_mlp_out_b) = _fox_attn(
            proj3, _fox_cum(small3, fbias), g_fox_out[i].reshape(FOX_HEADS, 1, FOX_HEAD_DIM),
            [w_out[i], w_mlp_out[i]])

        kw = GLA_HEADS * GLA_DK
        wa_pad = jnp.concatenate(
            [jnp.zeros((FOX_HEADS, kw), F32), w_gla_a2[i],
             jnp.zeros((LANE - FOX_HEADS - GLA_GATE_RANK, kw), F32)], axis=0).astype(BF16)
        gla, (w_mlp_in_b,) = _gla(
            proj3, small3, wa_pad, row(b_gla_a2[i]), g_gla_out[i].reshape(GLA_HEADS, 1, GLA_DV),
            [w_mlp_in[i]])

        x1 = _out_proj(fox.reshape(b * s, -1), gla.reshape(b * s, -1), w_out_b,
                       x2, mod, row(g_post_mix[i]), s)
        x2 = _mlp(x1, mod, row(g_pre_mlp[i]), row(g_post_mlp[i]), w_mlp_in_b, w_mlp_out_b, s)
        x = x2.reshape(b, s, d)
    return x
```

```python
import functools

import jax
import jax.numpy as jnp
from jax import lax
from jax.experimental import pallas as pl
from jax.experimental.pallas import tpu as pltpu

F32 = jnp.float32
BF16 = jnp.bfloat16

EPS = 1e-6
N_MOD = 6
FOX_HEADS = 8
FOX_HEAD_DIM = 128
GLA_HEADS = 4
GLA_DK = 128
GLA_DV = 256
GLA_GATE_RANK = 16
GLA_GATE_TEMP = 16.0
CHUNK = 64
LANE = 128
BF16_ROWS = 16
VMEM_LIMIT = 56 * 1024 * 1024
VMEM_LIMIT_LARGE = 60 * 1024 * 1024

NEG_BIG = -1e30
LOG2E = 1.4426950408889634
FOX_HEADS_PER_STEP = 4


def _nt_dot(a, b):
    return lax.dot_general(a, b, (((1,), (1,)), ((), ())), preferred_element_type=F32)


def _tn_dot(a, b):
    return lax.dot_general(a, b, (((0,), (0,)), ((), ())), preferred_element_type=F32)


def _log_sigmoid(z):
    return jnp.minimum(z, 0.0) - jnp.log(1.0 + jnp.exp(-jnp.abs(z)))


def _sigmoid(z):
    return 1.0 / (1.0 + jnp.exp(-z))


def _tri_cumsum(tri, v):
    hi = v.astype(BF16)
    lo = (v - hi.astype(F32)).astype(BF16)
    return (jnp.dot(tri, hi, preferred_element_type=F32)
            + jnp.dot(tri, lo, preferred_element_type=F32))


def _lower_tri(n, value=1.0):
    r = lax.broadcasted_iota(jnp.int32, (n, n), 0)
    c = lax.broadcasted_iota(jnp.int32, (n, n), 1)
    return jnp.where(r >= c, value, 0.0).astype(BF16)


def _ada_kernel(c_ref, w_ref, b_ref, o_ref):
    c = c_ref[...]
    act = (c * _sigmoid(c)).astype(BF16)
    o_ref[...] = jnp.dot(act, w_ref[...].astype(BF16), preferred_element_type=F32) + b_ref[...]


def _ada(c_pad, w_ada, b_ada, tn=1024):
    m, d = c_pad.shape
    n = w_ada.shape[1]
    return pl.pallas_call(
        _ada_kernel,
        out_shape=jax.ShapeDtypeStruct((m, n), F32),
        grid=(n // tn,),
        in_specs=[pl.BlockSpec((m, d), lambda j: (0, 0)),
                  pl.BlockSpec((d, tn), lambda j: (0, j)),
                  pl.BlockSpec((1, tn), lambda j: (0, j))],
        out_specs=pl.BlockSpec((m, tn), lambda j: (0, j)),
        compiler_params=pltpu.CompilerParams(dimension_semantics=("parallel",),
                                             vmem_limit_bytes=VMEM_LIMIT),
        name="ada",
    )(c_pad, w_ada, b_ada)


def _inproj_kernel(x_ref, mod_ref, g_ref, wlo_ref, whi_ref, ws_ref, o_ref, os_ref, h_scr):
    j = pl.program_id(1)
    half = wlo_ref.shape[0]

    @pl.when(j == 0)
    def _():
        x = x_ref[...]
        inv = lax.rsqrt(jnp.mean(x * x, axis=-1, keepdims=True) + EPS)
        gain = g_ref[...] * (1.0 + mod_ref[0, 1:2, :])
        hb = (x * inv * gain + mod_ref[0, 0:1, :]).astype(BF16)
        h_scr[...] = hb
        os_ref[...] = _nt_dot(hb, ws_ref[...])
        o_ref[:, :half] = (_nt_dot(hb, wlo_ref[...]) * (FOX_HEAD_DIM ** -0.5 * LOG2E)).astype(BF16)
        o_ref[:, half:] = _nt_dot(hb, whi_ref[...]).astype(BF16)

    @pl.when(j > 0)
    def _():
        o_ref[:, :half] = _nt_dot(h_scr[...], wlo_ref[...]).astype(BF16)
        o_ref[:, half:] = _nt_dot(h_scr[...], whi_ref[...]).astype(BF16)


def _in_proj(x2, mod, g, wt, w_small, rows_per_batch, tm=1024):
    t, d = x2.shape
    half = FOX_HEADS * FOX_HEAD_DIM
    n = wt.shape[0] - FOX_HEADS - GLA_GATE_RANK
    tn = 2 * half
    assert n % tn == 0 and GLA_HEADS * (2 * GLA_DK + GLA_DV) == tn
    tiles_per_batch = rows_per_batch // tm
    lo_start = lambda j: pl.multiple_of(j * tn + jnp.where(j >= 2, FOX_HEADS, 0), FOX_HEADS)
    hi_start = lambda j: pl.multiple_of(j * tn + half + jnp.where(j >= 1, FOX_HEADS, 0)
                                        + jnp.where(j >= 2, GLA_GATE_RANK, 0), FOX_HEADS)
    return pl.pallas_call(
        _inproj_kernel,
        out_shape=(jax.ShapeDtypeStruct((t, n), BF16),
                   jax.ShapeDtypeStruct((t, LANE), F32)),
        grid=(t // tm, n // tn),
        in_specs=[pl.BlockSpec((tm, d), lambda i, j: (i, 0)),
                  pl.BlockSpec((1, N_MOD, d), lambda i, j: (i // tiles_per_batch, 0, 0)),
                  pl.BlockSpec((1, d), lambda i, j: (0, 0)),
                  pl.BlockSpec((pl.Element(half), pl.Element(d)), lambda i, j: (lo_start(j), 0)),
                  pl.BlockSpec((pl.Element(half), pl.Element(d)), lambda i, j: (hi_start(j), 0)),
                  pl.BlockSpec((LANE, d), lambda i, j: (0, 0))],
        out_specs=(pl.BlockSpec((tm, tn), lambda i, j: (i, j)),
                   pl.BlockSpec((tm, LANE), lambda i, j: (i, 0))),
        scratch_shapes=[pltpu.VMEM((tm, d), BF16)],
        compiler_params=pltpu.CompilerParams(dimension_semantics=("parallel", "arbitrary"),
                                             vmem_limit_bytes=VMEM_LIMIT),
        name="in_proj",
    )(x2, mod, g, wt, wt, w_small)


def _cum_kernel(s_ref, b_ref, o_ref, *, blk):
    s = s_ref.shape[1]
    tri = _lower_tri(blk)
    col = lax.broadcasted_iota(jnp.int32, (blk, LANE), 1)
    carry = jnp.zeros((1, LANE), F32)
    for r in range(s // blk):
        rows = slice(r * blk, (r + 1) * blk)
        z = s_ref[0, rows, :] + b_ref[...]
        cs = _tri_cumsum(tri, _log_sigmoid(z)) + carry
        carry = cs[blk - 1:blk, :]
        c2 = jnp.where(col < FOX_HEADS, cs * LOG2E, 0.0)
        hi = c2.astype(BF16).astype(F32)
        rem = c2 - hi
        mid = rem.astype(BF16).astype(F32)
        lo = (rem - mid).astype(BF16).astype(F32)
        pieces = hi + pltpu.roll(mid, FOX_HEADS, 1) + pltpu.roll(lo, 2 * FOX_HEADS, 1)
        o_ref[0, rows, :] = pieces.astype(BF16)


def _fox_cum(small3, bias_row, blk=256):
    b, s, _ = small3.shape
    return pl.pallas_call(
        functools.partial(_cum_kernel, blk=blk),
        out_shape=jax.ShapeDtypeStruct((b, s, LANE), BF16),
        grid=(b,),
        in_specs=[pl.BlockSpec((1, s, LANE), lambda i: (i, 0, 0)),
                  pl.BlockSpec((1, LANE), lambda i: (0, 0))],
        out_specs=pl.BlockSpec((1, s, LANE), lambda i: (i, 0, 0)),
        compiler_params=pltpu.CompilerParams(dimension_semantics=("parallel",),
                                             vmem_limit_bytes=VMEM_LIMIT),
        name="fox_cum",
    )(small3, bias_row)


class _FoxHead:
    def __init__(self, ka, vt, sa, sb, xa, xb, m, l, acc):
        self.ka, self.vt, self.m, self.l, self.acc = ka, vt, m, l, acc
        self.a = (sa, xa)
        self.b = (sb, xb)


def _fox_kernel(q_ref, k_ref, v_ref, a_ref, g_ref, *refs, tq, n_cast, n_pairs):
    o_ref = refs[n_cast]
    scratch = refs[2 * n_cast + 1:]
    per_head = len(scratch) // FOX_HEADS_PER_STEP
    heads = [_FoxHead(*scratch[n * per_head:(n + 1) * per_head])
             for n in range(FOX_HEADS_PER_STEP)]
    t = pl.program_id(2)
    lanes = lambda n: slice(n * LANE, (n + 1) * LANE)

    @pl.when(t == 0)
    def _():
        for n, hd in enumerate(heads):
            hd.ka[:, 0:LANE] = k_ref[0, :, lanes(n)]
            hd.ka[:, LANE:] = a_ref[0]
            hd.vt[...] = v_ref[0, :, lanes(n)].T

    for src, dst in zip(refs[:n_cast], refs[n_cast + 1:2 * n_cast + 1]):
        dst[...] = src[...].astype(BF16)

    col = lax.broadcasted_iota(jnp.int32, (tq, LANE), 1)

    def tile_queries(half):
        rows = slice(half * tq, (half + 1) * tq)
        qas = []
        for n in range(len(heads)):
            h = pl.program_id(1) * FOX_HEADS_PER_STEP + n
            pick = (col == h) | (col == h + FOX_HEADS) | (col == h + 2 * FOX_HEADS)
            qas.append(jnp.concatenate([q_ref[0, rows, lanes(n)],
                                        jnp.where(pick, -1.0, 0.0).astype(BF16)], axis=1))
        return qas

    def reset_state():
        for hd in heads:
            hd.m[...] = jnp.full_like(hd.m, NEG_BIG)
            hd.l[...] = jnp.zeros_like(hd.l)
            hd.acc[...] = jnp.zeros_like(hd.acc)

    def logits_to(pick_buf, qas, blk):
        keys = pl.ds(blk * tq, tq)
        for hd, qa in zip(heads, qas):
            s_ref, max_ref = pick_buf(hd)
            st = _nt_dot(hd.ka[keys, :], qa)
            s_ref[...] = st
            max_ref[...] = jnp.max(st, axis=0, keepdims=True)

    def absorb(pick_buf, blk, masked):
        keys = pl.ds(blk * tq, tq)
        for hd in heads:
            s_ref, max_ref = pick_buf(hd)
            st = s_ref[...]
            if masked:
                kr = lax.broadcasted_iota(jnp.int32, (tq, tq), 0)
                qc = lax.broadcasted_iota(jnp.int32, (tq, tq), 1)
                st = jnp.where(kr <= qc, st, NEG_BIG)
                block_max = jnp.max(st, axis=0, keepdims=True)
            else:
                block_max = max_ref[...]
            m = hd.m[...]
            m_new = jnp.maximum(m, block_max)
            alpha = jnp.exp2(m - m_new)
            p = jnp.exp2(st - m_new)
            hd.m[...] = m_new
            hd.l[...] = alpha * hd.l[...] + jnp.sum(p, axis=0, keepdims=True)
            pv = jnp.dot(hd.vt[:, keys], p.astype(BF16), preferred_element_type=F32)
            hd.acc[...] = alpha * hd.acc[...] + pv

    def full_block_pairs(first, second, qas, t):
        for u in range(t):
            logits_to(second, qas, 2 * u + 1)
            absorb(first, 2 * u, False)
            logits_to(first, qas, 2 * u + 2)
            absorb(second, 2 * u + 1, False)

    def finish(half):
        for n, hd in enumerate(heads):
            ot = hd.acc[...] / hd.l[...]
            ot = ot * lax.rsqrt(jnp.mean(ot * ot, axis=0, keepdims=True) + EPS)
            o_ref[0, half * tq:(half + 1) * tq, lanes(n)] = (ot.T * g_ref[n]).astype(BF16)

    buf_a = lambda hd: hd.a
    buf_b = lambda hd: hd.b

    def tile_pair(t):
        q_even, q_odd = tile_queries(0), tile_queries(1)

        reset_state()
        logits_to(buf_a, q_even, 0)
        full_block_pairs(buf_a, buf_b, q_even, t)
        logits_to(buf_b, q_odd, 0)
        absorb(buf_a, 2 * t, True)
        finish(0)

        reset_state()
        full_block_pairs(buf_b, buf_a, q_odd, t)
        logits_to(buf_a, q_odd, 2 * t + 1)
        absorb(buf_b, 2 * t, False)
        absorb(buf_a, 2 * t + 1, True)
        finish(1)

    for static_t in range(n_pairs):
        pl.when(t == static_t)(functools.partial(tile_pair, static_t))


def _cast_slabs(f32_weights, n_steps, step_index):
    assert all(w.shape[0] % (n_steps * BF16_ROWS) == 0 for w in f32_weights)
    return [pl.BlockSpec((w.shape[0] // n_steps, w.shape[1]), step_index) for w in f32_weights]


def _fox_attn(proj3, pieces3, g_fox, f32_weights, tq=512):
    b, s, _ = proj3.shape
    nh = FOX_HEADS_PER_STEP
    groups = FOX_HEADS // nh
    width = nh * FOX_HEAD_DIM
    nt = s // (2 * tq)
    slabs = _cast_slabs(f32_weights, b * groups * nt,
                        lambda bi, gi, ti: ((bi * groups + gi) * nt + ti, 0))
    per_head = [pltpu.VMEM((s, 2 * LANE), BF16), pltpu.VMEM((FOX_HEAD_DIM, s), BF16),
                pltpu.VMEM((tq, tq), F32), pltpu.VMEM((tq, tq), F32),
                pltpu.VMEM((1, tq), F32), pltpu.VMEM((1, tq), F32),
                pltpu.VMEM((1, tq), F32), pltpu.VMEM((1, tq), F32),
                pltpu.VMEM((FOX_HEAD_DIM, tq), F32)]
    outs = pl.pallas_call(
        functools.partial(_fox_kernel, tq=tq, n_cast=len(f32_weights), n_pairs=nt),
        out_shape=[jax.ShapeDtypeStruct((b, s, FOX_HEADS * FOX_HEAD_DIM), BF16)]
        + [jax.ShapeDtypeStruct(w.shape, BF16) for w in f32_weights],
        grid=(b, groups, nt),
        in_specs=[pl.BlockSpec((1, 2 * tq, width), lambda bi, gi, ti: (bi, ti, gi)),
                  pl.BlockSpec((1, s, width), lambda bi, gi, ti: (bi, 0, groups + gi)),
                  pl.BlockSpec((1, s, width), lambda bi, gi, ti: (bi, 0, 2 * groups + gi)),
                  pl.BlockSpec((1, s, LANE), lambda bi, gi, ti: (bi, 0, 0)),
                  pl.BlockSpec((nh, 1, LANE), lambda bi, gi, ti: (gi, 0, 0))] + slabs,
        out_specs=[pl.BlockSpec((1, 2 * tq, width), lambda bi, gi, ti: (bi, ti, gi))] + slabs,
        scratch_shapes=per_head * nh,
        compiler_params=pltpu.CompilerParams(
            dimension_semantics=("parallel", "parallel", "arbitrary"),
            vmem_limit_bytes=VMEM_LIMIT_LARGE),
        name="fox_attn",
    )(proj3, proj3, proj3, pieces3, g_fox, *f32_weights)
    return outs[0], outs[1:]


def _gla_kernel(q_ref, k_ref, v_ref, r_ref, s_ref, wa_ref, ba_ref, gg_ref, *refs, ts, n_cast):
    o_ref, st_ref = refs[n_cast], refs[-1]
    for src, dst in zip(refs[:n_cast], refs[n_cast + 1:2 * n_cast + 1]):
        dst[...] = src[...].astype(BF16)

    @pl.when(pl.program_id(2) == 0)
    def _():
        st_ref[...] = jnp.zeros_like(st_ref)

    pre = jnp.dot(s_ref[0].astype(BF16), wa_ref[...], preferred_element_type=F32) + ba_ref[...]
    la = _log_sigmoid(pre)
    tri = _lower_tri(CHUNK, 1.0 / GLA_GATE_TEMP)
    chunks = [slice(c * CHUNK, (c + 1) * CHUNK) for c in range(ts // CHUNK)]
    cums = [_tri_cumsum(tri, la[sl]) for sl in chunks]
    tots = [cum[CHUNK - 1:CHUNK, :] for cum in cums]
    ups = []
    for sl, cum, tot in zip(chunks, cums, tots):
        kd = (k_ref[0, sl, :].astype(F32) * jnp.exp(tot - cum)).astype(BF16)
        ups.append(_tn_dot(v_ref[0, sl, :], kd))
    st = st_ref[...]
    states = []
    for tot, up in zip(tots, ups):
        st = st * jnp.exp(tot) + up
        states.append(st.astype(BF16))
    st_ref[...] = st
    outs = [_nt_dot(q_ref[0, sl, :], sb) for sl, sb in zip(chunks, states)]
    o = jnp.concatenate(outs, axis=0)
    qs = GLA_DK ** -0.5
    inv = qs * lax.rsqrt(qs * qs * jnp.mean(o * o, axis=-1, keepdims=True) + EPS)
    o = o * inv * gg_ref[0]
    r = r_ref[0].astype(F32)
    o_ref[0] = (o * (r * _sigmoid(r))).astype(BF16)


def _gla(proj3, small3, wa_pad, ba_row, g_gla, f32_weights, ts=4096):
    b, s, _ = proj3.shape
    q_blk = 3 * FOX_HEADS
    k_blk = q_blk + GLA_HEADS
    v_blk = (k_blk + GLA_HEADS) // 2
    r_blk = v_blk + GLA_HEADS
    nt = s // ts
    slabs = _cast_slabs(f32_weights, b * GLA_HEADS * nt,
                        lambda bi, gi, ti: ((bi * GLA_HEADS + gi) * nt + ti, 0))
    outs = pl.pallas_call(
        functools.partial(_gla_kernel, ts=ts, n_cast=len(f32_weights)),
        out_shape=[jax.ShapeDtypeStruct((b, s, GLA_HEADS * GLA_DV), BF16)]
        + [jax.ShapeDtypeStruct(w.shape, BF16) for w in f32_weights],
        grid=(b, GLA_HEADS, nt),
        in_specs=[pl.BlockSpec((1, ts, GLA_DK), lambda bi, gi, ti: (bi, ti, q_blk + gi)),
                  pl.BlockSpec((1, ts, GLA_DK), lambda bi, gi, ti: (bi, ti, k_blk + gi)),
                  pl.BlockSpec((1, ts, GLA_DV), lambda bi, gi, ti: (bi, ti, v_blk + gi)),
                  pl.BlockSpec((1, ts, GLA_DV), lambda bi, gi, ti: (bi, ti, r_blk + gi)),
                  pl.BlockSpec((1, ts, LANE), lambda bi, gi, ti: (bi, ti, 0)),
                  pl.BlockSpec((LANE, GLA_DK), lambda bi, gi, ti: (0, gi)),
                  pl.BlockSpec((1, GLA_DK), lambda bi, gi, ti: (0, gi)),
                  pl.BlockSpec((1, 1, GLA_DV), lambda bi, gi, ti: (gi, 0, 0))] + slabs,
        out_specs=[pl.BlockSpec((1, ts, GLA_DV), lambda bi, gi, ti: (bi, ti, gi))] + slabs,
        scratch_shapes=[pltpu.VMEM((GLA_DV, GLA_DK), F32)],
        compiler_params=pltpu.CompilerParams(
            dimension_semantics=("parallel", "parallel", "arbitrary"),
            vmem_limit_bytes=VMEM_LIMIT),
        name="gla",
    )(proj3, proj3, proj3, proj3, small3, wa_pad, ba_row, g_gla, *f32_weights)
    return outs[0], outs[1:]


def _outproj_kernel(fox_ref, gla_ref, w_ref, x_ref, mod_ref, g_ref, o_ref):
    wide = fox_ref.shape[1]
    gain = mod_ref[0, 2:3, :] * g_ref[...]
    half = x_ref.shape[0] // 2
    for rows in (slice(0, half), slice(half, 2 * half)):
        y = (jnp.dot(fox_ref[rows, :], w_ref[0:wide, :], preferred_element_type=F32)
             + jnp.dot(gla_ref[rows, :], w_ref[wide:, :], preferred_element_type=F32))
        inv = lax.rsqrt(jnp.mean(y * y, axis=-1, keepdims=True) + EPS)
        o_ref[rows, :] = x_ref[rows, :] + y * inv * gain


def _out_proj(fox2, gla2, w_out, x2, mod, g, rows_per_batch, tm=1024):
    t, d = x2.shape
    wide = fox2.shape[1]
    tiles_per_batch = rows_per_batch // tm
    return pl.pallas_call(
        _outproj_kernel,
        out_shape=jax.ShapeDtypeStruct((t, d), F32),
        grid=(t // tm,),
        in_specs=[pl.BlockSpec((tm, wide), lambda i: (i, 0)),
                  pl.BlockSpec((tm, wide), lambda i: (i, 0)),
                  pl.BlockSpec((2 * wide, d), lambda i: (0, 0), pipeline_mode=pl.Buffered(1)),
                  pl.BlockSpec((tm, d), lambda i: (i, 0)),
                  pl.BlockSpec((1, N_MOD, d), lambda i: (i // tiles_per_batch, 0, 0)),
                  pl.BlockSpec((1, d), lambda i: (0, 0))],
        out_specs=pl.BlockSpec((tm, d), lambda i: (i, 0)),
        compiler_params=pltpu.CompilerParams(dimension_semantics=("parallel",),
                                             vmem_limit_bytes=VMEM_LIMIT),
        name="out_proj",
    )(fox2, gla2, w_out, x2, mod, g)


def _mlp_kernel(x_ref, mod_ref, gpre_ref, gpost_ref, w1_ref, w2_ref, o_ref, h_scr):
    f = pl.program_id(1)
    last = pl.num_programs(1) - 1
    half = x_ref.shape[0] // 2
    halves = (slice(0, half), slice(half, 2 * half))

    def hidden_block(hb):
        u = jnp.maximum(jnp.dot(hb, w1_ref[...], preferred_element_type=F32), 0.0)
        return jnp.dot((u * u).astype(BF16), w2_ref[...], preferred_element_type=F32)

    @pl.when(f == 0)
    def _():
        gain = gpre_ref[...] * (1.0 + mod_ref[0, 4:5, :])
        for rows in halves:
            x = x_ref[rows, :]
            inv = lax.rsqrt(jnp.mean(x * x, axis=-1, keepdims=True) + EPS)
            hb = (x * inv * gain + mod_ref[0, 3:4, :]).astype(BF16)
            h_scr[rows, :] = hb
            o_ref[rows, :] = hidden_block(hb)

    @pl.when((f > 0) & (f < last))
    def _():
        for rows in halves:
            o_ref[rows, :] += hidden_block(h_scr[rows, :])

    @pl.when(f == last)
    def _():
        gain = mod_ref[0, 5:6, :] * gpost_ref[...]
        for rows in halves:
            y = o_ref[rows, :] + hidden_block(h_scr[rows, :])
            inv = lax.rsqrt(jnp.mean(y * y, axis=-1, keepdims=True) + EPS)
            o_ref[rows, :] = x_ref[rows, :] + y * inv * gain


def _mlp(x1, mod, g_pre, g_post, w1, w2, rows_per_batch, tm=1024, tf=1024):
    t, d = x1.shape
    dff = w1.shape[1]
    tiles_per_batch = rows_per_batch // tm
    return pl.pallas_call(
        _mlp_kernel,
        out_shape=jax.ShapeDtypeStruct((t, d), F32),
        grid=(t // tm, dff // tf),
        in_specs=[pl.BlockSpec((tm, d), lambda i, f: (i, 0)),
                  pl.BlockSpec((1, N_MOD, d), lambda i, f: (i // tiles_per_batch, 0, 0)),
                  pl.BlockSpec((1, d), lambda i, f: (0, 0)),
                  pl.BlockSpec((1, d), lambda i, f: (0, 0)),
                  pl.BlockSpec((d, tf), lambda i, f: (0, f)),
                  pl.BlockSpec((tf, d), lambda i, f: (f, 0))],
        out_specs=pl.BlockSpec((tm, d), lambda i, f: (i, 0)),
        scratch_shapes=[pltpu.VMEM((tm, d), BF16)],
        compiler_params=pltpu.CompilerParams(dimension_semantics=("parallel", "arbitrary"),
                                             vmem_limit_bytes=VMEM_LIMIT_LARGE),
        name="mlp",
    )(x1, mod, g_pre, g_post, w1, w2)


def _transposed_in_proj_weight(w_in):
    fox_end = 3 * FOX_HEADS * FOX_HEAD_DIM
    gla_start = fox_end + FOX_HEADS
    gla_end = gla_start + 2 * GLA_HEADS * GLA_DK + GLA_HEADS * GLA_DV
    gate_start = gla_end + GLA_GATE_RANK
    wt = jnp.swapaxes(w_in, 0, 1).astype(BF16)
    pad = jnp.zeros((LANE - FOX_HEADS - GLA_GATE_RANK, w_in.shape[0]), BF16)
    small = jnp.concatenate([wt[fox_end:gla_start], wt[gla_end:gate_start], pad], axis=0)
    return wt, small


def kernel(x, c, w_ada, b_ada, g_pre_mix, g_post_mix, w_in, b_fgate, w_gla_a2, b_gla_a2,
           g_fox_out, g_gla_out, w_out, g_pre_mlp, g_post_mlp, w_mlp_in, w_mlp_out):
    b, s, d = x.shape
    depth = w_ada.shape[0]
    row = lambda v: v.reshape(1, -1)
    for i in range(depth):
        c_pad = jnp.concatenate([c, jnp.zeros((8 - b, d), c.dtype)], axis=0)
        mod = _ada(c_pad, w_ada[i], row(b_ada[i]))[:b].reshape(b, N_MOD, d)

        wt, w_small = _transposed_in_proj_weight(w_in[i])
        x2 = x.reshape(b * s, d)
        proj, small = _in_proj(x2, mod, row(g_pre_mix[i]), wt, w_small, s)
        proj3 = proj.reshape(b, s, -1)
        small3 = small.reshape(b, s, LANE)

        fbias = jnp.concatenate([b_fgate[i], jnp.zeros((LANE - FOX_HEADS,), F32)]).reshape(1, LANE)
        fox, (w_out_b, w_mlp_out_b) = _fox_attn(
            proj3, _fox_cum(small3, fbias), g_fox_out[i].reshape(FOX_HEADS, 1, FOX_HEAD_DIM),
            [w_out[i], w_mlp_out[i]])

        kw = GLA_HEADS * GLA_DK
        wa_pad = jnp.concatenate(
            [jnp.zeros((FOX_HEADS, kw), F32), w_gla_a2[i],
             jnp.zeros((LANE - FOX_HEADS - GLA_GATE_RANK, kw), F32)], axis=0).astype(BF16)
        gla, (w_mlp_in_b,) = _gla(
            proj3, small3, wa_pad, row(b_gla_a2[i]), g_gla_out[i].reshape(GLA_HEADS, 1, GLA_DV),
            [w_mlp_in[i]])

        x1 = _out_proj(fox.reshape(b * s, -1), gla.reshape(b * s, -1), w_out_b,
                       x2, mod, row(g_post_mix[i]), s)
        x2 = _mlp(x1, mod, row(g_pre_mlp[i]), row(g_post_mlp[i]), w_mlp_in_b, w_mlp_out_b, s)
        x = x2.reshape(b, s, d)
    return x
```

```python
import functools

import jax
import jax.numpy as jnp
from jax import lax
from jax.experimental import pallas as pl
from jax.experimental.pallas import tpu as pltpu

F32 = jnp.float32
BF16 = jnp.bfloat16

EPS = 1e-6
N_MOD = 6
FOX_HEADS = 8
FOX_HEAD_DIM = 128
GLA_HEADS = 4
GLA_DK = 128
GLA_DV = 256
GLA_GATE_RANK = 16
GLA_GATE_TEMP = 16.0
CHUNK = 64
LANE = 128
BF16_ROWS = 16
VMEM_LIMIT = 56 * 1024 * 1024
VMEM_LIMIT_LARGE = 60 * 1024 * 1024

NEG_BIG = -1e30
LOG2E = 1.4426950408889634
FOX_HEADS_PER_STEP = 4


def _nt_dot(a, b):
    return lax.dot_general(a, b, (((1,), (1,)), ((), ())), preferred_element_type=F32)


def _tn_dot(a, b):
    return lax.dot_general(a, b, (((0,), (0,)), ((), ())), preferred_element_type=F32)


def _log_sigmoid(z):
    return jnp.minimum(z, 0.0) - jnp.log(1.0 + jnp.exp(-jnp.abs(z)))


def _sigmoid(z):
    return 1.0 / (1.0 + jnp.exp(-z))


def _tri_cumsum(tri, v):
    hi = v.astype(BF16)
    lo = (v - hi.astype(F32)).astype(BF16)
    return (jnp.dot(tri, hi, preferred_element_type=F32)
            + jnp.dot(tri, lo, preferred_element_type=F32))


def _lower_tri(n, value=1.0):
    r = lax.broadcasted_iota(jnp.int32, (n, n), 0)
    c = lax.broadcasted_iota(jnp.int32, (n, n), 1)
    return jnp.where(r >= c, value, 0.0).astype(BF16)


def _ada_kernel(c_ref, w_ref, b_ref, o_ref):
    c = c_ref[...]
    act = (c * _sigmoid(c)).astype(BF16)
    o_ref[...] = jnp.dot(act, w_ref[...].astype(BF16), preferred_element_type=F32) + b_ref[...]


def _ada(c_pad, w_ada, b_ada, tn=1024):
    m, d = c_pad.shape
    n = w_ada.shape[1]
    return pl.pallas_call(
        _ada_kernel,
        out_shape=jax.ShapeDtypeStruct((m, n), F32),
        grid=(n // tn,),
        in_specs=[pl.BlockSpec((m, d), lambda j: (0, 0)),
                  pl.BlockSpec((d, tn), lambda j: (0, j)),
                  pl.BlockSpec((1, tn), lambda j: (0, j))],
        out_specs=pl.BlockSpec((m, tn), lambda j: (0, j)),
        compiler_params=pltpu.CompilerParams(dimension_semantics=("parallel",),
                                             vmem_limit_bytes=VMEM_LIMIT),
        name="ada",
    )(c_pad, w_ada, b_ada)


def _inproj_kernel(x_ref, mod_ref, g_ref, wlo_ref, whi_ref, ws_ref, o_ref, os_ref, h_scr):
    j = pl.program_id(1)
    half = wlo_ref.shape[0]

    @pl.when(j == 0)
    def _():
        x = x_ref[...]
        inv = lax.rsqrt(jnp.mean(x * x, axis=-1, keepdims=True) + EPS)
        gain = g_ref[...] * (1.0 + mod_ref[0, 1:2, :])
        hb = (x * inv * gain + mod_ref[0, 0:1, :]).astype(BF16)
        h_scr[...] = hb
        os_ref[...] = _nt_dot(hb, ws_ref[...])
        o_ref[:, :half] = (_nt_dot(hb, wlo_ref[...]) * (FOX_HEAD_DIM ** -0.5 * LOG2E)).astype(BF16)
        o_ref[:, half:] = _nt_dot(hb, whi_ref[...]).astype(BF16)

    @pl.when(j > 0)
    def _():
        o_ref[:, :half] = _nt_dot(h_scr[...], wlo_ref[...]).astype(BF16)
        o_ref[:, half:] = _nt_dot(h_scr[...], whi_ref[...]).astype(BF16)


def _in_proj(x2, mod, g, wt, w_small, rows_per_batch, tm=1024):
    t, d = x2.shape
    half = FOX_HEADS * FOX_HEAD_DIM
    n = wt.shape[0] - FOX_HEADS - GLA_GATE_RANK
    tn = 2 * half
    assert n % tn == 0 and GLA_HEADS * (2 * GLA_DK + GLA_DV) == tn
    tiles_per_batch = rows_per_batch // tm
    lo_start = lambda j: pl.multiple_of(j * tn + jnp.where(j >= 2, FOX_HEADS, 0), FOX_HEADS)
    hi_start = lambda j: pl.multiple_of(j * tn + half + jnp.where(j >= 1, FOX_HEADS, 0)
                                        + jnp.where(j >= 2, GLA_GATE_RANK, 0), FOX_HEADS)
    return pl.pallas_call(
        _inproj_kernel,
        out_shape=(jax.ShapeDtypeStruct((t, n), BF16),
                   jax.ShapeDtypeStruct((t, LANE), F32)),
        grid=(t // tm, n // tn),
        in_specs=[pl.BlockSpec((tm, d), lambda i, j: (i, 0)),
                  pl.BlockSpec((1, N_MOD, d), lambda i, j: (i // tiles_per_batch, 0, 0)),
                  pl.BlockSpec((1, d), lambda i, j: (0, 0)),
                  pl.BlockSpec((pl.Element(half), pl.Element(d)), lambda i, j: (lo_start(j), 0)),
                  pl.BlockSpec((pl.Element(half), pl.Element(d)), lambda i, j: (hi_start(j), 0)),
                  pl.BlockSpec((LANE, d), lambda i, j: (0, 0))],
        out_specs=(pl.BlockSpec((tm, tn), lambda i, j: (i, j)),
                   pl.BlockSpec((tm, LANE), lambda i, j: (i, 0))),
        scratch_shapes=[pltpu.VMEM((tm, d), BF16)],
        compiler_params=pltpu.CompilerParams(dimension_semantics=("parallel", "arbitrary"),
                                             vmem_limit_bytes=VMEM_LIMIT),
        name="in_proj",
    )(x2, mod, g, wt, wt, w_small)


def _cum_kernel(s_ref, b_ref, o_ref, *, blk):
    s = s_ref.shape[1]
    tri = _lower_tri(blk)
    col = lax.broadcasted_iota(jnp.int32, (blk, LANE), 1)
    carry = jnp.zeros((1, LANE), F32)
    for r in range(s // blk):
        rows = slice(r * blk, (r + 1) * blk)
        z = s_ref[0, rows, :] + b_ref[...]
        cs = _tri_cumsum(tri, _log_sigmoid(z)) + carry
        carry = cs[blk - 1:blk, :]
        c2 = jnp.where(col < FOX_HEADS, cs * LOG2E, 0.0)
        hi = c2.astype(BF16).astype(F32)
        rem = c2 - hi
        mid = rem.astype(BF16).astype(F32)
        lo = (rem - mid).astype(BF16).astype(F32)
        pieces = hi + pltpu.roll(mid, FOX_HEADS, 1) + pltpu.roll(lo, 2 * FOX_HEADS, 1)
        o_ref[0, rows, :] = pieces.astype(BF16)


def _fox_cum(small3, bias_row, blk=256):
    b, s, _ = small3.shape
    return pl.pallas_call(
        functools.partial(_cum_kernel, blk=blk),
        out_shape=jax.ShapeDtypeStruct((b, s, LANE), BF16),
        grid=(b,),
        in_specs=[pl.BlockSpec((1, s, LANE), lambda i: (i, 0, 0)),
                  pl.BlockSpec((1, LANE), lambda i: (0, 0))],
        out_specs=pl.BlockSpec((1, s, LANE), lambda i: (i, 0, 0)),
        compiler_params=pltpu.CompilerParams(dimension_semantics=("parallel",),
                                             vmem_limit_bytes=VMEM_LIMIT),
        name="fox_cum",
    )(small3, bias_row)


class _FoxHead:
    def __init__(self, ka, vt, sa, sb, xa, xb, m, l, acc):
        self.ka, self.vt, self.m, self.l, self.acc = ka, vt, m, l, acc
        self.a = (sa, xa)
        self.b = (sb, xb)


def _fox_kernel(q_ref, k_ref, v_ref, a_ref, g_ref, *refs, tq, n_cast):
    o_ref = refs[n_cast]
    scratch = refs[2 * n_cast + 1:]
    per_head = len(scratch) // FOX_HEADS_PER_STEP
    heads = [_FoxHead(*scratch[n * per_head:(n + 1) * per_head])
             for n in range(FOX_HEADS_PER_STEP)]
    t = pl.program_id(2)
    lanes = lambda n: slice(n * LANE, (n + 1) * LANE)

    @pl.when(t == 0)
    def _():
        for n, hd in enumerate(heads):
            hd.ka[:, 0:LANE] = k_ref[0, :, lanes(n)]
            hd.ka[:, LANE:] = a_ref[0]
            hd.vt[...] = v_ref[0, :, lanes(n)].T

    for src, dst in zip(refs[:n_cast], refs[n_cast + 1:2 * n_cast + 1]):
        dst[...] = src[...].astype(BF16)

    col = lax.broadcasted_iota(jnp.int32, (tq, LANE), 1)

    def tile_queries(half):
        rows = slice(half * tq, (half + 1) * tq)
        qas = []
        for n in range(len(heads)):
            h = pl.program_id(1) * FOX_HEADS_PER_STEP + n
            pick = (col == h) | (col == h + FOX_HEADS) | (col == h + 2 * FOX_HEADS)
            qas.append(jnp.concatenate([q_ref[0, rows, lanes(n)],
                                        jnp.where(pick, -1.0, 0.0).astype(BF16)], axis=1))
        return qas

    def reset_state():
        for hd in heads:
            hd.m[...] = jnp.full_like(hd.m, NEG_BIG)
            hd.l[...] = jnp.zeros_like(hd.l)
            hd.acc[...] = jnp.zeros_like(hd.acc)

    def logits_to(pick_buf, qas, blk):
        keys = pl.ds(pl.multiple_of(blk * tq, tq), tq)
        for hd, qa in zip(heads, qas):
            s_ref, max_ref = pick_buf(hd)
            st = _nt_dot(hd.ka[keys, :], qa)
            s_ref[...] = st
            max_ref[...] = jnp.max(st, axis=0, keepdims=True)

    def absorb(pick_buf, blk, masked):
        keys = pl.ds(pl.multiple_of(blk * tq, tq), tq)
        for hd in heads:
            s_ref, max_ref = pick_buf(hd)
            st = s_ref[...]
            if masked:
                kr = lax.broadcasted_iota(jnp.int32, (tq, tq), 0)
                qc = lax.broadcasted_iota(jnp.int32, (tq, tq), 1)
                st = jnp.where(kr <= qc, st, NEG_BIG)
                block_max = jnp.max(st, axis=0, keepdims=True)
            else:
                block_max = max_ref[...]
            m = hd.m[...]
            m_new = jnp.maximum(m, block_max)
            alpha = jnp.exp2(m - m_new)
            p = jnp.exp2(st - m_new)
            hd.m[...] = m_new
            hd.l[...] = alpha * hd.l[...] + jnp.sum(p, axis=0, keepdims=True)
            pv = jnp.dot(hd.vt[:, keys], p.astype(BF16), preferred_element_type=F32)
            hd.acc[...] = alpha * hd.acc[...] + pv

    def full_block_pairs(first, second, qas):
        def body(u, carry):
            logits_to(second, qas, 2 * u + 1)
            absorb(first, 2 * u, False)
            logits_to(first, qas, 2 * u + 2)
            absorb(second, 2 * u + 1, False)
            return carry
        lax.fori_loop(0, t, body, 0)

    def finish(half):
        for n, hd in enumerate(heads):
            ot = hd.acc[...] / hd.l[...]
            ot = ot * lax.rsqrt(jnp.mean(ot * ot, axis=0, keepdims=True) + EPS)
            o_ref[0, half * tq:(half + 1) * tq, lanes(n)] = (ot.T * g_ref[n]).astype(BF16)

    buf_a = lambda hd: hd.a
    buf_b = lambda hd: hd.b
    q_even, q_odd = tile_queries(0), tile_queries(1)

    reset_state()
    logits_to(buf_a, q_even, 0)
    full_block_pairs(buf_a, buf_b, q_even)
    logits_to(buf_b, q_odd, 0)
    absorb(buf_a, 2 * t, True)
    finish(0)

    reset_state()
    full_block_pairs(buf_b, buf_a, q_odd)
    logits_to(buf_a, q_odd, 2 * t + 1)
    absorb(buf_b, 2 * t, False)
    absorb(buf_a, 2 * t + 1, True)
    finish(1)


def _cast_slabs(f32_weights, n_steps, step_index):
    assert all(w.shape[0] % (n_steps * BF16_ROWS) == 0 for w in f32_weights)
    return [pl.BlockSpec((w.shape[0] // n_steps, w.shape[1]), step_index) for w in f32_weights]


def _fox_attn(proj3, pieces3, g_fox, f32_weights, tq=512):
    b, s, _ = proj3.shape
    nh = FOX_HEADS_PER_STEP
    groups = FOX_HEADS // nh
    width = nh * FOX_HEAD_DIM
    nt = s // (2 * tq)
    slabs = _cast_slabs(f32_weights, b * groups * nt,
                        lambda bi, gi, ti: ((bi * groups + gi) * nt + ti, 0))
    per_head = [pltpu.VMEM((s, 2 * LANE), BF16), pltpu.VMEM((FOX_HEAD_DIM, s), BF16),
                pltpu.VMEM((tq, tq), F32), pltpu.VMEM((tq, tq), F32),
                pltpu.VMEM((1, tq), F32), pltpu.VMEM((1, tq), F32),
                pltpu.VMEM((1, tq), F32), pltpu.VMEM((1, tq), F32),
                pltpu.VMEM((FOX_HEAD_DIM, tq), F32)]
    outs = pl.pallas_call(
        functools.partial(_fox_kernel, tq=tq, n_cast=len(f32_weights)),
        out_shape=[jax.ShapeDtypeStruct((b, s, FOX_HEADS * FOX_HEAD_DIM), BF16)]
        + [jax.ShapeDtypeStruct(w.shape, BF16) for w in f32_weights],
        grid=(b, groups, nt),
        in_specs=[pl.BlockSpec((1, 2 * tq, width), lambda bi, gi, ti: (bi, ti, gi)),
                  pl.BlockSpec((1, s, width), lambda bi, gi, ti: (bi, 0, groups + gi)),
                  pl.BlockSpec((1, s, width), lambda bi, gi, ti: (bi, 0, 2 * groups + gi)),
                  pl.BlockSpec((1, s, LANE), lambda bi, gi, ti: (bi, 0, 0)),
                  pl.BlockSpec((nh, 1, LANE), lambda bi, gi, ti: (gi, 0, 0))] + slabs,
        out_specs=[pl.BlockSpec((1, 2 * tq, width), lambda bi, gi, ti: (bi, ti, gi))] + slabs,
        scratch_shapes=per_head * nh,
        compiler_params=pltpu.CompilerParams(
            dimension_semantics=("parallel", "parallel", "arbitrary"),
            vmem_limit_bytes=VMEM_LIMIT_LARGE),
        name="fox_attn",
    )(proj3, proj3, proj3, pieces3, g_fox, *f32_weights)
    return outs[0], outs[1:]


def _gla_kernel(q_ref, k_ref, v_ref, r_ref, s_ref, wa_ref, ba_ref, gg_ref, *refs, ts, n_cast):
    o_ref, st_ref = refs[n_cast], refs[-1]
    for src, dst in zip(refs[:n_cast], refs[n_cast + 1:2 * n_cast + 1]):
        dst[...] = src[...].astype(BF16)

    @pl.when(pl.program_id(2) == 0)
    def _():
        st_ref[...] = jnp.zeros_like(st_ref)

    pre = jnp.dot(s_ref[0].astype(BF16), wa_ref[...], preferred_element_type=F32) + ba_ref[...]
    la = _log_sigmoid(pre)
    tri = _lower_tri(CHUNK, 1.0 / GLA_GATE_TEMP)
    chunks = [slice(c * CHUNK, (c + 1) * CHUNK) for c in range(ts // CHUNK)]
    cums = [_tri_cumsum(tri, la[sl]) for sl in chunks]
    tots = [cum[CHUNK - 1:CHUNK, :] for cum in cums]
    ups = []
    for sl, cum, tot in zip(chunks, cums, tots):
        kd = (k_ref[0, sl, :].astype(F32) * jnp.exp(tot - cum)).astype(BF16)
        ups.append(_tn_dot(v_ref[0, sl, :], kd))
    st = st_ref[...]
    states = []
    for tot, up in zip(tots, ups):
        st = st * jnp.exp(tot) + up
        states.append(st.astype(BF16))
    st_ref[...] = st
    outs = [_nt_dot(q_ref[0, sl, :], sb) for sl, sb in zip(chunks, states)]
    o = jnp.concatenate(outs, axis=0)
    qs = GLA_DK ** -0.5
    inv = qs * lax.rsqrt(qs * qs * jnp.mean(o * o, axis=-1, keepdims=True) + EPS)
    o = o * inv * gg_ref[0]
    r = r_ref[0].astype(F32)
    o_ref[0] = (o * (r * _sigmoid(r))).astype(BF16)


def _gla(proj3, small3, wa_pad, ba_row, g_gla, f32_weights, ts=4096):
    b, s, _ = proj3.shape
    q_blk = 3 * FOX_HEADS
    k_blk = q_blk + GLA_HEADS
    v_blk = (k_blk + GLA_HEADS) // 2
    r_blk = v_blk + GLA_HEADS
    nt = s // ts
    slabs = _cast_slabs(f32_weights, b * GLA_HEADS * nt,
                        lambda bi, gi, ti: ((bi * GLA_HEADS + gi) * nt + ti, 0))
    outs = pl.pallas_call(
        functools.partial(_gla_kernel, ts=ts, n_cast=len(f32_weights)),
        out_shape=[jax.ShapeDtypeStruct((b, s, GLA_HEADS * GLA_DV), BF16)]
        + [jax.ShapeDtypeStruct(w.shape, BF16) for w in f32_weights],
        grid=(b, GLA_HEADS, nt),
        in_specs=[pl.BlockSpec((1, ts, GLA_DK), lambda bi, gi, ti: (bi, ti, q_blk + gi)),
                  pl.BlockSpec((1, ts, GLA_DK), lambda bi, gi, ti: (bi, ti, k_blk + gi)),
                  pl.BlockSpec((1, ts, GLA_DV), lambda bi, gi, ti: (bi, ti, v_blk + gi)),
                  pl.BlockSpec((1, ts, GLA_DV), lambda bi, gi, ti: (bi, ti, r_blk + gi)),
                  pl.BlockSpec((1, ts, LANE), lambda bi, gi, ti: (bi, ti, 0)),
                  pl.BlockSpec((LANE, GLA_DK), lambda bi, gi, ti: (0, gi)),
                  pl.BlockSpec((1, GLA_DK), lambda bi, gi, ti: (0, gi)),
                  pl.BlockSpec((1, 1, GLA_DV), lambda bi, gi, ti: (gi, 0, 0))] + slabs,
        out_specs=[pl.BlockSpec((1, ts, GLA_DV), lambda bi, gi, ti: (bi, ti, gi))] + slabs,
        scratch_shapes=[pltpu.VMEM((GLA_DV, GLA_DK), F32)],
        compiler_params=pltpu.CompilerParams(
            dimension_semantics=("parallel", "parallel", "arbitrary"),
            vmem_limit_bytes=VMEM_LIMIT),
        name="gla",
    )(proj3, proj3, proj3, proj3, small3, wa_pad, ba_row, g_gla, *f32_weights)
    return outs[0], outs[1:]


def _outproj_kernel(fox_ref, gla_ref, w_ref, x_ref, mod_ref, g_ref, o_ref):
    wide = fox_ref.shape[1]
    gain = mod_ref[0, 2:3, :] * g_ref[...]
    half = x_ref.shape[0] // 2
    for rows in (slice(0, half), slice(half, 2 * half)):
        y = (jnp.dot(fox_ref[rows, :], w_ref[0:wide, :], preferred_element_type=F32)
             + jnp.dot(gla_ref[rows, :], w_ref[wide:, :], preferred_element_type=F32))
        inv = lax.rsqrt(jnp.mean(y * y, axis=-1, keepdims=True) + EPS)
        o_ref[rows, :] = x_ref[rows, :] + y * inv * gain


def _out_proj(fox2, gla2, w_out, x2, mod, g, rows_per_batch, tm=1024):
    t, d = x2.shape
    wide = fox2.shape[1]
    tiles_per_batch = rows_per_batch // tm
    return pl.pallas_call(
        _outproj_kernel,
        out_shape=jax.ShapeDtypeStruct((t, d), F32),
        grid=(t // tm,),
        in_specs=[pl.BlockSpec((tm, wide), lambda i: (i, 0)),
                  pl.BlockSpec((tm, wide), lambda i: (i, 0)),
                  pl.BlockSpec((2 * wide, d), lambda i: (0, 0), pipeline_mode=pl.Buffered(1)),
                  pl.BlockSpec((tm, d), lambda i: (i, 0)),
                  pl.BlockSpec((1, N_MOD, d), lambda i: (i // tiles_per_batch, 0, 0)),
                  pl.BlockSpec((1, d), lambda i: (0, 0))],
        out_specs=pl.BlockSpec((tm, d), lambda i: (i, 0)),
        compiler_params=pltpu.CompilerParams(dimension_semantics=("parallel",),
                                             vmem_limit_bytes=VMEM_LIMIT),
        name="out_proj",
    )(fox2, gla2, w_out, x2, mod, g)


def _mlp_kernel(x_ref, mod_ref, gpre_ref, gpost_ref, w1_ref, w2_ref, o_ref, h_scr):
    f = pl.program_id(1)
    last = pl.num_programs(1) - 1
    half = x_ref.shape[0] // 2
    halves = (slice(0, half), slice(half, 2 * half))
    quarters = tuple(slice(n * half // 2, (n + 1) * half // 2) for n in range(4))

    def hidden_block(hb):
        u = jnp.maximum(jnp.dot(hb, w1_ref[...], preferred_element_type=F32), 0.0)
        return jnp.dot((u * u).astype(BF16), w2_ref[...], preferred_element_type=F32)

    @pl.when(f == 0)
    def _():
        gain = gpre_ref[...] * (1.0 + mod_ref[0, 4:5, :])
        for rows in quarters:
            x = x_ref[rows, :]
            inv = lax.rsqrt(jnp.mean(x * x, axis=-1, keepdims=True) + EPS)
            hb = (x * inv * gain + mod_ref[0, 3:4, :]).astype(BF16)
            h_scr[rows, :] = hb
            o_ref[rows, :] = hidden_block(hb)

    @pl.when((f > 0) & (f < last))
    def _():
        for rows in halves:
            o_ref[rows, :] += hidden_block(h_scr[rows, :])

    @pl.when(f == last)
    def _():
        gain = mod_ref[0, 5:6, :] * gpost_ref[...]
        for rows in quarters:
            y = o_ref[rows, :] + hidden_block(h_scr[rows, :])
            inv = lax.rsqrt(jnp.mean(y * y, axis=-1, keepdims=True) + EPS)
            o_ref[rows, :] = x_ref[rows, :] + y * inv * gain


def _mlp(x1, mod, g_pre, g_post, w1, w2, rows_per_batch, tm=1024, tf=1024):
    t, d = x1.shape
    dff = w1.shape[1]
    tiles_per_batch = rows_per_batch // tm
    return pl.pallas_call(
        _mlp_kernel,
        out_shape=jax.ShapeDtypeStruct((t, d), F32),
        grid=(t // tm, dff // tf),
        in_specs=[pl.BlockSpec((tm, d), lambda i, f: (i, 0)),
                  pl.BlockSpec((1, N_MOD, d), lambda i, f: (i // tiles_per_batch, 0, 0)),
                  pl.BlockSpec((1, d), lambda i, f: (0, 0)),
                  pl.BlockSpec((1, d), lambda i, f: (0, 0)),
                  pl.BlockSpec((d, tf), lambda i, f: (0, f)),
                  pl.BlockSpec((tf, d), lambda i, f: (f, 0))],
        out_specs=pl.BlockSpec((tm, d), lambda i, f: (i, 0)),
        scratch_shapes=[pltpu.VMEM((tm, d), BF16)],
        compiler_params=pltpu.CompilerParams(dimension_semantics=("parallel", "arbitrary"),
                                             vmem_limit_bytes=VMEM_LIMIT_LARGE),
        name="mlp",
    )(x1, mod, g_pre, g_post, w1, w2)


def _transposed_in_proj_weight(w_in):
    fox_end = 3 * FOX_HEADS * FOX_HEAD_DIM
    gla_start = fox_end + FOX_HEADS
    gla_end = gla_start + 2 * GLA_HEADS * GLA_DK + GLA_HEADS * GLA_DV
    gate_start = gla_end + GLA_GATE_RANK
    wt = jnp.swapaxes(w_in, 0, 1).astype(BF16)
    pad = jnp.zeros((LANE - FOX_HEADS - GLA_GATE_RANK, w_in.shape[0]), BF16)
    small = jnp.concatenate([wt[fox_end:gla_start], wt[gla_end:gate_start], pad], axis=0)
    return wt, small


def kernel(x, c, w_ada, b_ada, g_pre_mix, g_post_mix, w_in, b_fgate, w_gla_a2, b_gla_a2,
           g_fox_out, g_gla_out, w_out, g_pre_mlp, g_post_mlp, w_mlp_in, w_mlp_out):
    b, s, d = x.shape
    depth = w_ada.shape[0]
    row = lambda v: v.reshape(1, -1)
    for i in range(depth):
        c_pad = jnp.concatenate([c, jnp.zeros((8 - b, d), c.dtype)], axis=0)
        mod = _ada(c_pad, w_ada[i], row(b_ada[i]))[:b].reshape(b, N_MOD, d)

        wt, w_small = _transposed_in_proj_weight(w_in[i])
        x2 = x.reshape(b * s, d)
        proj, small = _in_proj(x2, mod, row(g_pre_mix[i]), wt, w_small, s)
        proj3 = proj.reshape(b, s, -1)
        small3 = small.reshape(b, s, LANE)

        fbias = jnp.concatenate([b_fgate[i], jnp.zeros((LANE - FOX_HEADS,), F32)]).reshape(1, LANE)
        fox, (w_out_b, w_mlp_out_b) = _fox_attn(
            proj3, _fox_cum(small3, fbias), g_fox_out[i].reshape(FOX_HEADS, 1, FOX_HEAD_DIM),
            [w_out[i], w_mlp_out[i]])

        kw = GLA_HEADS * GLA_DK
        wa_pad = jnp.concatenate(
            [jnp.zeros((FOX_HEADS, kw), F32), w_gla_a2[i],
             jnp.zeros((LANE - FOX_HEADS - GLA_GATE_RANK, kw), F32)], axis=0).astype(BF16)
        gla, (w_mlp_in_b,) = _gla(
            proj3, small3, wa_pad, row(b_gla_a2[i]), g_gla_out[i].reshape(GLA_HEADS, 1, GLA_DV),
            [w_mlp_in[i]])

        x1 = _out_proj(fox.reshape(b * s, -1), gla.reshape(b * s, -1), w_out_b,
                       x2, mod, row(g_post_mix[i]), s)
        x2 = _mlp(x1, mod, row(g_pre_mlp[i]), row(g_post_mlp[i]), w_mlp_in_b, w_mlp_out_b, s)
        x = x2.reshape(b, s, d)
    return x
```
